```python
import math
import jax, jax.numpy as jnp
from jax import lax
import numpy as np

D_MODEL = 1024
BATCH = 1
SEQ = 16384
DEPTH = 2
DEC_BATCH = 8
DEC_SEQ = 32
PAST_LEN = 4096

CHUNK = 64
D_HEAD = 64
H_A = 4
H_B = 4
H_C = 4
DK_C = 128
DV_C = 128
W_A = H_A * D_HEAD
W_B = H_B * D_HEAD
QK_C = H_C * DK_C
W_C = H_C * DV_C
D_MIX = W_A + W_B + W_C
N_PREV_CHUNKS = 8
BAND = N_PREV_CHUNKS + 1
REL_CLIP = 128
N_REL = 2 * REL_CLIP + 1
D_FF = 4 * D_MODEL
Q_BLOCK = 128
EPS = 1e-6
NEG_BIG = -1e30
LB_FLOOR = 1e-30
_SIZES = [W_A, W_A, W_A, W_B, W_B, W_B, QK_C, QK_C, W_C, W_C]
D_IN = sum(_SIZES)
SPLITS = [int(s) for s in np.cumsum(_SIZES)[:-1]]

kernel_name = "hymba_style_sb_band_hgrn2_encoder_step"


def rmsnorm(x, g):
    xf = x.astype(jnp.float32)
    y = xf * lax.rsqrt(jnp.mean(xf * xf, axis=-1, keepdims=True) + EPS)
    return (y * g.astype(jnp.float32)).astype(x.dtype)


def project(xn, w_in, qn_a, kn_a, qn_b, kn_b, lb):
    B, T, _ = xn.shape
    p = jnp.einsum('btd,de->bte', xn, w_in)
    qa, ka, va, qb, kb, vb, qc, fc, ic, gc = jnp.split(p, SPLITS, axis=-1)
    qa = rmsnorm(qa.reshape(B, T, H_A, D_HEAD), qn_a)
    ka = rmsnorm(ka.reshape(B, T, H_A, D_HEAD), kn_a)
    va = va.reshape(B, T, H_A, D_HEAD)
    qb = rmsnorm(qb.reshape(B, T, H_B, D_HEAD), qn_b)
    kb = rmsnorm(kb.reshape(B, T, H_B, D_HEAD), kn_b)
    vb = vb.reshape(B, T, H_B, D_HEAD)
    f_raw = fc.astype(jnp.float32)
    log_f = jnp.logaddexp(jnp.log(jnp.maximum(lb, LB_FLOOR)), jnp.log1p(-lb) + jax.nn.log_sigmoid(f_raw))
    k_c = (1.0 - lb) * jax.nn.sigmoid(-f_raw)
    q_c = qc.astype(jnp.float32) * (DK_C ** -0.5)
    q_c = q_c.reshape(B, T, H_C, DK_C)
    k_c = k_c.reshape(B, T, H_C, DK_C)
    log_f = log_f.reshape(B, T, H_C, DK_C)
    v_c = ic.reshape(B, T, H_C, DV_C)
    return (qa, ka, va), (qb, kb, vb), (q_c, k_c, v_c, log_f), gc


def stick_breaking_block(q, k, v, q_pos, k_pos):
    z = jnp.einsum('bqhd,bkhd->bhqk', q, k).astype(jnp.float32) / math.sqrt(D_HEAD)
    mask = k_pos[None, :] < q_pos[:, None]
    log_1m = jnp.where(mask, -jax.nn.softplus(z), 0.0)
    later = lax.cumsum(log_1m, axis=3, reverse=True) - log_1m
    w = jnp.where(mask, jnp.exp(jax.nn.log_sigmoid(z) + later), 0.0)
    return jnp.einsum('bhqk,bkhd->bqhd', w.astype(v.dtype), v)


def sb_prompt(q, k, v):
    B, T = q.shape[:2]
    nb = T // Q_BLOCK
    pos = jnp.arange(T)
    qb = jnp.moveaxis(q.reshape(B, nb, Q_BLOCK, H_A, D_HEAD), 1, 0)
    pb = pos.reshape(nb, Q_BLOCK)
    out = lax.map(lambda a: stick_breaking_block(a[0], k, v, a[1], pos), (qb, pb))
    return jnp.moveaxis(out, 0, 1).reshape(B, T, H_A, D_HEAD)


def rel_bias(table, dist):
    idx = jnp.clip(dist, -REL_CLIP, REL_CLIP) + REL_CLIP
    return table.astype(jnp.float32)[:, idx]


def chunk_band_prompt(q, k, v, table):
    B, T = q.shape[:2]
    nc = T // CHUNK
    pad = ((0, 0), (N_PREV_CHUNKS, 0), (0, 0), (0, 0), (0, 0))
    qc = q.reshape(B, nc, CHUNK, H_B, D_HEAD)
    kp = jnp.pad(k.reshape(B, nc, CHUNK, H_B, D_HEAD), pad)
    vp = jnp.pad(v.reshape(B, nc, CHUNK, H_B, D_HEAD), pad)
    idx = jnp.arange(nc)[:, None] + jnp.arange(BAND)[None, :]
    kb = kp[:, idx].reshape(B, nc, BAND * CHUNK, H_B, D_HEAD)
    vb = vp[:, idx].reshape(B, nc, BAND * CHUNK, H_B, D_HEAD)
    s = jnp.einsum('bcqhd,bckhd->bchqk', qc, kb).astype(jnp.float32) / math.sqrt(D_HEAD)
    i = jnp.arange(CHUNK)
    m = jnp.arange(BAND * CHUNK)
    dist = N_PREV_CHUNKS * CHUNK + i[:, None] - m[None, :]
    valid = (jnp.arange(nc)[:, None] - N_PREV_CHUNKS + m[None, :] // CHUNK) >= 0
    s = s + rel_bias(table, dist)[None, None]
    s = jnp.where(valid[None, :, None, None, :], s, NEG_BIG)
    p = jax.nn.softmax(s, axis=-1)
    o = jnp.einsum('bchqk,bckhd->bcqhd', p.astype(v.dtype), vb)
    return o.reshape(B, T, H_B, D_HEAD)


def chunk_band_sample(q, k_new, v_new, cache_k, cache_v, table, past_len):
    Lb = cache_k.shape[1]
    T = q.shape[1]
    k = jnp.concatenate([cache_k, k_new], axis=1)
    v = jnp.concatenate([cache_v, v_new], axis=1)
    q_pos = past_len + jnp.arange(T)
    k_pos = jnp.concatenate([past_len - Lb + jnp.arange(Lb), q_pos])
    s = jnp.einsum('bqhd,bkhd->bhqk', q, k).astype(jnp.float32) / math.sqrt(D_HEAD)
    qch = q_pos // CHUNK
    kch = k_pos // CHUNK
    valid = (kch[None, :] >= qch[:, None] - N_PREV_CHUNKS) & (kch[None, :] <= qch[:, None])
    s = s + rel_bias(table, q_pos[:, None] - k_pos[None, :])[None]
    s = jnp.where(valid[None, None], s, NEG_BIG)
    p = jax.nn.softmax(s, axis=-1)
    return jnp.einsum('bhqk,bkhd->bqhd', p.astype(v.dtype), v)


def hgrn2_block(S, xs):
    q, k, v, log_f = xs
    L = q.shape[1]
    G = jnp.cumsum(log_f, axis=1)
    causal = jnp.tril(jnp.ones((L, L), dtype=bool))[None, :, :, None, None]
    diff = G[:, :, None] - G[:, None, :]
    decay = jnp.where(causal, jnp.exp(jnp.minimum(diff, 0.0)), 0.0)
    A = jnp.einsum('bihk,bjhk,bijhk->bhij', q, k, decay)
    o = jnp.einsum('bhij,bjhv->bihv', A, v) + jnp.einsum('bihk,bhkv->bihv', q * jnp.exp(G), S)
    G_last = G[:, -1]
    k_dec = k * jnp.exp(G_last[:, None] - G)
    S_new = jnp.exp(G_last)[..., None] * S + jnp.einsum('bjhk,bjhv->bhkv', k_dec, v)
    return S_new, o


def hgrn2_scan(S0, q, k, v, log_f, block):
    B, T = q.shape[:2]
    nb = T // block

    def blocks(a):
        return jnp.moveaxis(a.reshape(B, nb, block, *a.shape[2:]), 1, 0)

    S, o = lax.scan(hgrn2_block, S0.astype(jnp.float32),
                    (blocks(q), blocks(k), blocks(v.astype(jnp.float32)), blocks(log_f)))
    o = jnp.moveaxis(o, 0, 1).reshape(B, T, H_C, DV_C)
    return o.astype(v.dtype), S


def merge_and_ffn(x, oa, ob, oc, gc, onorm_c, w_o, norm_ffn, w_up, w_down):
    B, T, _ = x.shape
    oc = rmsnorm(oc, onorm_c).reshape(B, T, W_C) * jax.nn.silu(gc)
    mixed = jnp.concatenate([oa.reshape(B, T, W_A), ob.reshape(B, T, W_B), oc.astype(x.dtype)], axis=-1)
    h = x + jnp.einsum('bte,ed->btd', mixed, w_o)
    u = jax.nn.relu(jnp.einsum('btd,df->btf', rmsnorm(h, norm_ffn), w_up))
    return h + jnp.einsum('btf,fd->btd', u * u, w_down)


def setup_inputs(seed: int = 0) -> dict:
    key = jax.random.key(seed)
    ks = jax.random.split(key, 20)
    f32 = jnp.float32

    def nrm(k, shape, scale):
        return scale * jax.random.normal(k, shape, f32)

    band_rows = min(N_PREV_CHUNKS * CHUNK, PAST_LEN)
    return {
        "x_prompt": nrm(ks[0], (BATCH, SEQ, D_MODEL), 1.0),
        "x_sample": nrm(ks[1], (DEC_BATCH, DEC_SEQ, D_MODEL), 1.0),
        "cache_a_k": nrm(ks[2], (DEPTH, DEC_BATCH, PAST_LEN, H_A, D_HEAD), 1.0),
        "cache_a_v": nrm(ks[3], (DEPTH, DEC_BATCH, PAST_LEN, H_A, D_HEAD), 1.0),
        "cache_b_k": nrm(ks[4], (DEPTH, DEC_BATCH, band_rows, H_B, D_HEAD), 1.0),
        "cache_b_v": nrm(ks[5], (DEPTH, DEC_BATCH, band_rows, H_B, D_HEAD), 1.0),
        "state_c": nrm(ks[6], (DEPTH, DEC_BATCH, H_C, DK_C, DV_C), 0.5),
        "norm_mix": 1.0 + nrm(ks[7], (DEPTH, D_MODEL), 0.02),
        "w_in": nrm(ks[8], (DEPTH, D_MODEL, D_IN), D_MODEL ** -0.5),
        "qnorm_a": 1.0 + nrm(ks[9], (DEPTH, D_HEAD), 0.02),
        "knorm_a": 1.0 + nrm(ks[10], (DEPTH, D_HEAD), 0.02),
        "qnorm_b": 1.0 + nrm(ks[11], (DEPTH, D_HEAD), 0.02),
        "knorm_b": 1.0 + nrm(ks[12], (DEPTH, D_HEAD), 0.02),
        "rel_bias_b": nrm(ks[13], (DEPTH, H_B, N_REL), 0.1),
        "lower_bounds": nrm(ks[14], (DEPTH, QK_C), 0.1),
        "onorm_c": 1.0 + nrm(ks[15], (DEPTH, DV_C), 0.02),
        "w_o": nrm(ks[16], (DEPTH, D_MIX, D_MODEL), D_MIX ** -0.5),
        "norm_ffn": 1.0 + nrm(ks[17], (DEPTH, D_MODEL), 0.02),
        "w_up": nrm(ks[18], (DEPTH, D_MODEL, D_FF), D_MODEL ** -0.5),
        "w_down": nrm(ks[19], (DEPTH, D_FF, D_MODEL), D_FF ** -0.5),
    }


def reference(x_prompt, x_sample, cache_a_k, cache_a_v, cache_b_k, cache_b_v, state_c,
              norm_mix, w_in, qnorm_a, knorm_a, qnorm_b, knorm_b, rel_bias_b, lower_bounds,
              onorm_c, w_o, norm_ffn, w_up, w_down):
    lb_sm = jax.nn.softmax(lower_bounds.astype(jnp.float32), axis=0)
    lb_all = jnp.cumsum(lb_sm, axis=0) - lb_sm[0]
    past_len = cache_a_k.shape[2]
    Bp, Tp = x_prompt.shape[:2]
    Bs, Ts = x_sample.shape[:2]
    band_rows_p = min(N_PREV_CHUNKS * CHUNK, Tp)
    q_pos_s = past_len + jnp.arange(Ts)
    k_pos_s = jnp.arange(past_len + Ts)

    xp, xs = x_prompt, x_sample
    akp, avp, bkp, bvp, csp = [], [], [], [], []
    aks, avs, bks, bvs, css = [], [], [], [], []
    for l in range(DEPTH):
        (qa, ka, va), (qb, kb, vb), (qc, kc, vc, lf), gc = project(
            rmsnorm(xp, norm_mix[l]), w_in[l], qnorm_a[l], knorm_a[l], qnorm_b[l], knorm_b[l], lb_all[l])
        oa = sb_prompt(qa, ka, va)
        ob = chunk_band_prompt(qb, kb, vb, rel_bias_b[l])
        oc, Sp = hgrn2_scan(jnp.zeros((Bp, H_C, DK_C, DV_C), jnp.float32), qc, kc, vc, lf, CHUNK)
        xp = merge_and_ffn(xp, oa, ob, oc, gc, onorm_c[l], w_o[l], norm_ffn[l], w_up[l], w_down[l])
        akp.append(ka)
        avp.append(va)
        bkp.append(kb[:, Tp - band_rows_p:])
        bvp.append(vb[:, Tp - band_rows_p:])
        csp.append(Sp.astype(x_prompt.dtype))

        (qa, ka, va), (qb, kb, vb), (qc, kc, vc, lf), gc = project(
            rmsnorm(xs, norm_mix[l]), w_in[l], qnorm_a[l], knorm_a[l], qnorm_b[l], knorm_b[l], lb_all[l])
        oa = stick_breaking_block(qa, jnp.concatenate([cache_a_k[l], ka], axis=1),
                                  jnp.concatenate([cache_a_v[l], va], axis=1), q_pos_s, k_pos_s)
        ob = chunk_band_sample(qb, kb, vb, cache_b_k[l], cache_b_v[l], rel_bias_b[l], past_len)
        oc, Ss = hgrn2_scan(state_c[l], qc, kc, vc, lf, Ts)
        xs = merge_and_ffn(xs, oa, ob, oc, gc, onorm_c[l], w_o[l], norm_ffn[l], w_up[l], w_down[l])
        aks.append(ka)
        avs.append(va)
        bks.append(kb)
        bvs.append(vb)
        css.append(Ss.astype(x_sample.dtype))

    a_k_prompt = jnp.stack(akp)
    a_v_prompt = jnp.stack(avp)
    b_k_prompt = jnp.stack(bkp)
    b_v_prompt = jnp.stack(bvp)
    c_state_prompt = jnp.stack(csp)
    a_k_sample = jnp.stack(aks)
    a_v_sample = jnp.stack(avs)
    b_k_sample = jnp.stack(bks)
    b_v_sample = jnp.stack(bvs)
    c_state_sample = jnp.stack(css)
    return (xp, xs, a_k_prompt, a_v_prompt, b_k_prompt, b_v_prompt, c_state_prompt,
            a_k_sample, a_v_sample, b_k_sample, b_v_sample, c_state_sample)
```

```python
import functools
import math

import jax
import jax.numpy as jnp
from jax import lax
from jax.experimental import pallas as pl
from jax.experimental.pallas import tpu as pltpu

F32 = jnp.float32
BF16 = jnp.bfloat16

D_HEAD = 64
N_HEADS = 4
W_ATT = N_HEADS * D_HEAD
DK_C = 128
W_C = N_HEADS * DK_C
CHUNK = 64
N_PREV_CHUNKS = 8
REL_CLIP = 128
N_REL = 2 * REL_CLIP + 1
EPS = 1e-6
NEG_BIG = -1e30
LB_FLOOR = 1e-30
SB_UNDERFLOW = 88.0
SB_TK = 128
HGRN_SUB = 16
VMEM_LIMIT = 56 * 1024 * 1024


def _cparams(n_axes):
    return pltpu.CompilerParams(dimension_semantics=("arbitrary",) * n_axes,
                                vmem_limit_bytes=VMEM_LIMIT)


def _resident(a):
    zeros = (0,) * a.ndim
    return pl.BlockSpec(a.shape, lambda *_: zeros, pipeline_mode=pl.Buffered(1))


def _split3(x):
    h1 = x.astype(BF16)
    r1 = x - h1.astype(F32)
    h2 = r1.astype(BF16)
    h3 = (r1 - h2.astype(F32)).astype(BF16)
    return h1, h2, h3


def _dot(a, b):
    return jnp.dot(a, b, preferred_element_type=F32)


def _dot_nt(a, b):
    return lax.dot_general(a, b, (((1,), (1,)), ((), ())), preferred_element_type=F32)


def _dot_tn(a, b):
    return lax.dot_general(a, b, (((0,), (0,)), ((), ())), preferred_element_type=F32)


def _softplus(z):
    return jnp.maximum(z, 0.0) + jnp.log1p(jnp.exp(-jnp.abs(z)))


def _iota(shape, dim):
    return lax.broadcasted_iota(jnp.int32, shape, dim)


def _proj_kernel(x_ref, nm_ref, w_ref, qna_ref, kna_ref, qnb_ref, knb_ref, lb_ref,
                 qa_ref, ka_ref, va_ref, ka16_ref, va16_ref,
                 qb_ref, kb_ref, vb_ref, kb16_ref, vb16_ref,
                 qc_ref, kc_ref, lf_ref, ic_ref, gc_ref, *, layer):
    x = x_ref[...]
    ms = jnp.mean(x * x, axis=-1, keepdims=True)
    xn = (x * lax.rsqrt(ms + EPS) * nm_ref[...]).astype(BF16)

    def seg(a, b):
        return _dot(xn, w_ref[:, a:b])

    same_head = (_iota((W_ATT, W_ATT), 0) // D_HEAD) == (_iota((W_ATT, W_ATT), 1) // D_HEAD)
    seg_mean = jnp.where(same_head, 1.0 / D_HEAD, 0.0).astype(BF16)

    def head_norm(p, g_ref):
        sq = p * p
        hi = sq.astype(BF16)
        lo = (sq - hi.astype(F32)).astype(BF16)
        m = _dot(hi, seg_mean) + _dot(lo, seg_mean)
        return p * lax.rsqrt(m + EPS) * g_ref[...]

    inv_sqrt_d = 1.0 / math.sqrt(D_HEAD)
    o = 0
    qa_ref[...] = (head_norm(seg(o, o + W_ATT), qna_ref) * inv_sqrt_d).astype(BF16)
    o += W_ATT
    ka = head_norm(seg(o, o + W_ATT), kna_ref)
    ka_ref[...] = ka
    ka16_ref[...] = ka.astype(BF16)
    o += W_ATT
    va = seg(o, o + W_ATT)
    va_ref[...] = va
    va16_ref[...] = va.astype(BF16)
    o += W_ATT
    qb_ref[...] = (head_norm(seg(o, o + W_ATT), qnb_ref) * inv_sqrt_d).astype(BF16)
    o += W_ATT
    kb = head_norm(seg(o, o + W_ATT), knb_ref)
    kb_ref[...] = kb
    kb16_ref[...] = kb.astype(BF16)
    o += W_ATT
    vb = seg(o, o + W_ATT)
    vb_ref[...] = vb
    vb16_ref[...] = vb.astype(BF16)
    o += W_ATT

    lbr = lb_ref[...]
    e = jnp.exp(lbr - jnp.max(lbr, axis=0, keepdims=True))
    sm = e / jnp.sum(e, axis=0, keepdims=True)
    lb = jnp.sum(sm[0:layer + 1], axis=0, keepdims=True) - sm[0:1]

    qc_ref[...] = seg(o, o + W_C) * (DK_C ** -0.5)
    o += W_C
    f_raw = seg(o, o + W_C)
    o += W_C
    log_sig = jnp.minimum(f_raw, 0.0) - jnp.log1p(jnp.exp(-jnp.abs(f_raw)))
    a = jnp.log(jnp.maximum(lb, LB_FLOOR))
    b = jnp.log1p(-lb) + log_sig
    lf_ref[...] = jnp.maximum(a, b) + jnp.log1p(jnp.exp(-jnp.abs(a - b)))
    kc_ref[...] = (1.0 - lb) * (1.0 / (1.0 + jnp.exp(f_raw)))
    ic_ref[...] = seg(o, o + W_C)
    o += W_C
    gc_ref[...] = seg(o, o + W_C)


def _proj(x2d, nm, w16, qna, kna, qnb, knb, lower_bounds, *, layer, tm):
    m, d = x2d.shape
    assert m % tm == 0
    row = lambda w: pl.BlockSpec((tm, w), lambda i: (i, 0))
    full = _resident
    outs = [
        (W_ATT, BF16), (W_ATT, F32), (W_ATT, F32), (W_ATT, BF16), (W_ATT, BF16),
        (W_ATT, BF16), (W_ATT, F32), (W_ATT, F32), (W_ATT, BF16), (W_ATT, BF16),
        (W_C, F32), (W_C, F32), (W_C, F32), (W_C, F32), (W_C, F32),
    ]
    return pl.pallas_call(
        functools.partial(_proj_kernel, layer=layer),
        grid=(m // tm,),
        in_specs=[row(d), full(nm), full(w16), full(qna), full(kna), full(qnb), full(knb),
                  full(lower_bounds)],
        out_specs=[row(w) for w, _ in outs],
        out_shape=[jax.ShapeDtypeStruct((m, w), dt) for w, dt in outs],
        compiler_params=_cparams(1),
        name="proj",
    )(x2d, nm, w16, qna, kna, qnb, knb, lower_bounds)


def _head_masks(width):
    lane_head = _iota((1, width), 1) // D_HEAD
    return [lane_head == h for h in range(N_HEADS)]


def _sb_block(qh, kblk, vblk, mask, tri, head_lane, carry_ref, acc_ref):
    for h in range(N_HEADS):
        z = _dot_nt(qh[h], kblk)
        sp = _softplus(z)
        log_1m = -sp if mask is None else jnp.where(mask, -sp, 0.0)
        l1, l2, l3 = _split3(log_1m)
        later = _dot(l1, tri) + _dot(l2, tri) + _dot(l3, tri)
        carry = carry_ref[h]
        w = jnp.exp((z - sp) + later + carry[:, :z.shape[1]])
        if mask is not None:
            w = jnp.where(mask, w, 0.0)
        pv = _dot(w.astype(BF16), vblk)
        acc_ref[...] += jnp.where(head_lane[h], pv, 0.0)
        carry_ref[h] = carry + jnp.sum(log_1m, axis=-1, keepdims=True)


def _sb_live(carry_ref):
    top = jnp.max(carry_ref[0])
    for h in range(1, N_HEADS):
        top = jnp.maximum(top, jnp.max(carry_ref[h]))
    return (top > -SB_UNDERFLOW).astype(jnp.int32)


def _strict_upper(n):
    return jnp.where(_iota((n, n), 0) > _iota((n, n), 1), 1.0, 0.0).astype(BF16)


def _sb_prompt_kernel(q_ref, k_ref, v_ref, o_ref, carry_ref, acc_ref, *, tq):
    i = pl.program_id(0)
    q = q_ref[...]
    head_lane = _head_masks(W_ATT)
    qh = [jnp.where(head_lane[h], q, jnp.zeros_like(q)) for h in range(N_HEADS)]
    tri = _strict_upper(SB_TK)
    carry_ref[...] = jnp.zeros_like(carry_ref)
    acc_ref[...] = jnp.zeros_like(acc_ref)
    q_pos = i * tq + _iota((tq, SB_TK), 0)
    blocks_per_q = tq // SB_TK

    def cond(c):
        j, live = c
        return jnp.logical_and(j >= 0, live > 0)

    def body(c):
        j, _ = c
        r0 = pl.multiple_of(j * SB_TK, SB_TK)
        kblk = k_ref[pl.ds(r0, SB_TK), :]
        vblk = v_ref[pl.ds(r0, SB_TK), :]
        mask = (j * SB_TK + _iota((tq, SB_TK), 1)) < q_pos
        _sb_block(qh, kblk, vblk, mask, tri, head_lane, carry_ref, acc_ref)
        return j - 1, _sb_live(carry_ref)

    lax.while_loop(cond, body, ((i + 1) * blocks_per_q - 1, jnp.int32(1)))
    o_ref[...] = acc_ref[...].astype(o_ref.dtype)


def _sb_prompt(q16, k16, v16, *, tq):
    t = q16.shape[0]
    assert t % tq == 0 and tq % SB_TK == 0
    return pl.pallas_call(
        functools.partial(_sb_prompt_kernel, tq=tq),
        grid=(t // tq,),
        in_specs=[pl.BlockSpec((tq, W_ATT), lambda i: (i, 0)),
                  _resident(k16), _resident(v16)],
        out_specs=pl.BlockSpec((tq, W_ATT), lambda i: (i, 0)),
        out_shape=jax.ShapeDtypeStruct((t, W_ATT), BF16),
        scratch_shapes=[pltpu.VMEM((N_HEADS, tq, SB_TK), F32), pltpu.VMEM((tq, W_ATT), F32)],
        compiler_params=_cparams(1),
        name="sb_prompt",
    )(q16, k16, v16)


def _sb_sample_kernel(q_ref, kn_ref, vn_ref, ck_ref, cv_ref, o_ref, carry_ref, acc_ref, *, ts, past):
    q = q_ref[...]
    head_lane = _head_masks(W_ATT)
    qh = [jnp.where(head_lane[h], q, jnp.zeros_like(q)) for h in range(N_HEADS)]
    carry_ref[...] = jnp.zeros_like(carry_ref)
    acc_ref[...] = jnp.zeros_like(acc_ref)
    mask = _iota((ts, ts), 1) < _iota((ts, ts), 0)
    _sb_block(qh, kn_ref[...], vn_ref[...], mask, _strict_upper(ts), head_lane, carry_ref, acc_ref)
    tri = _strict_upper(SB_TK)

    def cond(c):
        j, live = c
        return jnp.logical_and(j >= 0, live > 0)

    def body(c):
        j, _ = c
        r0 = pl.multiple_of(j * SB_TK, SB_TK)
        kblk = ck_ref[0, 0, pl.ds(r0, SB_TK), :].astype(BF16)
        vblk = cv_ref[0, 0, pl.ds(r0, SB_TK), :].astype(BF16)
        _sb_block(qh, kblk, vblk, None, tri, head_lane, carry_ref, acc_ref)
        return j - 1, _sb_live(carry_ref)

    lax.while_loop(cond, body, (past // SB_TK - 1, _sb_live(carry_ref)))
    o_ref[...] = acc_ref[...].astype(o_ref.dtype)


def _sb_sample(q16, kn16, vn16, cache_k, cache_v, *, layer, nb, ts):
    past = cache_k.shape[2]
    assert past % SB_TK == 0
    row = pl.BlockSpec((ts, W_ATT), lambda b: (b, 0))
    cache = pl.BlockSpec((1, 1, past, W_ATT), lambda b: (layer, b, 0, 0))
    return pl.pallas_call(
        functools.partial(_sb_sample_kernel, ts=ts, past=past),
        grid=(nb,),
        in_specs=[row, row, row, cache, cache],
        out_specs=row,
        out_shape=jax.ShapeDtypeStruct((nb * ts, W_ATT), BF16),
        scratch_shapes=[pltpu.VMEM((N_HEADS, ts, SB_TK), F32), pltpu.VMEM((ts, W_ATT), F32)],
        compiler_params=_cparams(1),
        name="sb_sample",
    )(q16, kn16, vn16, cache_k, cache_v)


def _rel_bias(table_ref, h, dist):
    idx = jnp.clip(dist, -REL_CLIP, REL_CLIP) + REL_CLIP

    def body(r, b):
        return jnp.where(idx == r, table_ref[h, r], b)

    return lax.fori_loop(0, N_REL, body, jnp.zeros(dist.shape, F32))


def _band_prompt_kernel(table_ref, q_ref, k0_ref, k1_ref, k2_ref, v0_ref, v1_ref, v2_ref,
                        o_ref, bias_ref, *, tq):
    i = pl.program_id(0)
    n_kb = 3
    tk = n_kb * tq
    rows = _iota((tq, tk), 0)
    cols = _iota((tq, tk), 1)
    back = (n_kb - 1) * tq

    @pl.when(i == 0)
    def _():
        for h in range(N_HEADS):
            bias_ref[h] = _rel_bias(table_ref, h, back + rows - cols)

    q = q_ref[...]
    head_lane = _head_masks(W_ATT)
    kcat = jnp.concatenate([k0_ref[...], k1_ref[...], k2_ref[...]], axis=0)
    vcat = jnp.concatenate([v0_ref[...], v1_ref[...], v2_ref[...]], axis=0)
    q_chunk = rows // CHUNK + (n_kb - 1) * (tq // CHUNK)
    k_chunk = cols // CHUNK
    in_band = jnp.logical_and(k_chunk >= q_chunk - N_PREV_CHUNKS, k_chunk <= q_chunk)
    allowed = jnp.logical_and(in_band, (i - (n_kb - 1)) * tq + cols >= 0)
    out = jnp.zeros((tq, W_ATT), F32)
    for h in range(N_HEADS):
        qh = jnp.where(head_lane[h], q, jnp.zeros_like(q))
        s = jnp.where(allowed, _dot_nt(qh, kcat) + bias_ref[h], NEG_BIG)
        e = jnp.exp(s - jnp.max(s, axis=-1, keepdims=True))
        pv = _dot(e.astype(BF16), vcat) / jnp.sum(e, axis=-1, keepdims=True)
        out = jnp.where(head_lane[h], pv, out)
    o_ref[...] = out.astype(o_ref.dtype)


def _band_prompt(q16, k16, v16, table, *, tq):
    t = q16.shape[0]
    assert t % tq == 0 and tq % CHUNK == 0 and 2 * tq >= N_PREV_CHUNKS * CHUNK
    blk = lambda back: pl.BlockSpec((tq, W_ATT), lambda i: (jnp.maximum(i - back, 0), 0))
    return pl.pallas_call(
        functools.partial(_band_prompt_kernel, tq=tq),
        grid=(t // tq,),
        in_specs=[pl.BlockSpec(memory_space=pltpu.SMEM),
                  blk(0), blk(2), blk(1), blk(0), blk(2), blk(1), blk(0)],
        out_specs=blk(0),
        out_shape=jax.ShapeDtypeStruct((t, W_ATT), BF16),
        scratch_shapes=[pltpu.VMEM((N_HEADS, tq, 3 * tq), F32)],
        compiler_params=_cparams(1),
        name="band_prompt",
    )(table, q16, k16, k16, k16, v16, v16, v16)


def _band_sample_kernel(table_ref, q_ref, kn_ref, vn_ref, ck_ref, cv_ref, o_ref, *, ts, past):
    lb = ck_ref.shape[2]
    q = q_ref[...]
    head_lane = _head_masks(W_ATT)
    kc = ck_ref[0, 0].astype(BF16)
    vc = cv_ref[0, 0].astype(BF16)
    kn = kn_ref[...]
    vn = vn_ref[...]
    q_pos_c = past + _iota((ts, lb), 0)
    k_pos_c = past - lb + _iota((ts, lb), 1)
    q_pos_n = past + _iota((ts, ts), 0)
    k_pos_n = past + _iota((ts, ts), 1)

    def allowed(q_pos, k_pos):
        qc, kc_ = q_pos // CHUNK, k_pos // CHUNK
        return jnp.logical_and(kc_ >= qc - N_PREV_CHUNKS, kc_ <= qc)

    ok_c = allowed(q_pos_c, k_pos_c)
    ok_n = allowed(q_pos_n, k_pos_n)
    out = jnp.zeros((ts, W_ATT), F32)
    for h in range(N_HEADS):
        qh = jnp.where(head_lane[h], q, jnp.zeros_like(q))
        s_c = jnp.where(ok_c, _dot_nt(qh, kc) + _rel_bias(table_ref, h, q_pos_c - k_pos_c), NEG_BIG)
        s_n = jnp.where(ok_n, _dot_nt(qh, kn) + _rel_bias(table_ref, h, q_pos_n - k_pos_n), NEG_BIG)
        m = jnp.maximum(jnp.max(s_c, axis=-1, keepdims=True), jnp.max(s_n, axis=-1, keepdims=True))
        e_c = jnp.exp(s_c - m)
        e_n = jnp.exp(s_n - m)
        den = jnp.sum(e_c, axis=-1, keepdims=True) + jnp.sum(e_n, axis=-1, keepdims=True)
        pv = (_dot(e_c.astype(BF16), vc) + _dot(e_n.astype(BF16), vn)) / den
        out = jnp.where(head_lane[h], pv, out)
    o_ref[...] = out.astype(o_ref.dtype)


def _band_sample(q16, kn16, vn16, cache_k, cache_v, table, *, layer, nb, ts, past):
    lb = cache_k.shape[2]
    row = pl.BlockSpec((ts, W_ATT), lambda b: (b, 0))
    cache = pl.BlockSpec((1, 1, lb, W_ATT), lambda b: (layer, b, 0, 0))
    return pl.pallas_call(
        functools.partial(_band_sample_kernel, ts=ts, past=past),
        grid=(nb,),
        in_specs=[pl.BlockSpec(memory_space=pltpu.SMEM), row, row, row, cache, cache],
        out_specs=row,
        out_shape=jax.ShapeDtypeStruct((nb * ts, W_ATT), BF16),
        compiler_params=_cparams(1),
        name="band_sample",
    )(table, q16, kn16, vn16, cache_k, cache_v)


def _hgrn_kernel(q_ref, k_ref, lf_ref, v_ref, s0_ref, o_ref, sout_ref, st_ref, *, tr):
    r = pl.program_id(1)

    @pl.when(r == 0)
    def _():
        for h in range(N_HEADS):
            st_ref[h] = s0_ref[0, h].T

    sub = HGRN_SUB
    tri = jnp.where(_iota((sub, sub), 0) >= _iota((sub, sub), 1), 1.0, 0.0).astype(BF16)
    row_id = _iota((sub, 1), 0)

    def block(n, carry):
        r0 = pl.multiple_of(n * sub, sub)
        for h in range(N_HEADS):
            cs = slice(h * DK_C, (h + 1) * DK_C)
            q = q_ref[pl.ds(r0, sub), cs]
            k = k_ref[pl.ds(r0, sub), cs]
            v = v_ref[pl.ds(r0, sub), cs]
            f1, f2, f3 = _split3(lf_ref[pl.ds(r0, sub), cs])
            g = _dot(tri, f1) + _dot(tri, f2) + _dot(tri, f3)
            g_last = g[sub - 1:sub, :]
            st = st_ref[h]
            o = _dot_nt((q * jnp.exp(g)).astype(BF16), st.astype(BF16))
            for j in range(sub):
                d = jnp.exp(jnp.minimum(g - g[j:j + 1, :], 0.0))
                a_col = jnp.sum(q * (k[j:j + 1, :] * d), axis=-1, keepdims=True)
                o = o + jnp.where(row_id >= j, a_col, 0.0) * v[j:j + 1, :]
            o_ref[pl.ds(r0, sub), cs] = o
            k_dec = k * jnp.exp(g_last - g)
            st_ref[h] = st * jnp.exp(g_last) + _dot_tn(v.astype(BF16), k_dec.astype(BF16))
        return carry

    lax.fori_loop(0, tr // sub, block, 0)

    @pl.when(r == pl.num_programs(1) - 1)
    def _():
        for h in range(N_HEADS):
            sout_ref[0, h] = st_ref[h].T


def _hgrn(qc, kc, lf, ic, s0, *, nb, t, tr):
    assert t % tr == 0 and tr % HGRN_SUB == 0
    nr = t // tr
    row = pl.BlockSpec((tr, W_C), lambda b, r: (b * nr + r, 0))
    state = pl.BlockSpec((1, N_HEADS, DK_C, DK_C), lambda b, r: (b, 0, 0, 0))
    return pl.pallas_call(
        functools.partial(_hgrn_kernel, tr=tr),
        grid=(nb, nr),
        in_specs=[row, row, row, row, state],
        out_specs=[row, state],
        out_shape=[jax.ShapeDtypeStruct((nb * t, W_C), F32),
                   jax.ShapeDtypeStruct((nb, N_HEADS, DK_C, DK_C), F32)],
        scratch_shapes=[pltpu.VMEM((N_HEADS, DK_C, DK_C), F32)],
        compiler_params=_cparams(2),
        name="hgrn",
    )(qc, kc, lf, ic, s0)


def _merge_kernel(x_ref, oa_ref, ob_ref, oc_ref, gc_ref, on_ref, wo_ref, nf_ref, wu_ref, wd_ref,
                  y_ref, *, f_chunk):
    oc = oc_ref[...]
    gc = gc_ref[...]
    parts = []
    for h in range(N_HEADS):
        cs = slice(h * DK_C, (h + 1) * DK_C)
        och = oc[:, cs]
        ms = jnp.mean(och * och, axis=-1, keepdims=True)
        parts.append(och * lax.rsqrt(ms + EPS) * on_ref[...])
    ocn = jnp.concatenate(parts, axis=-1) * (gc * (1.0 / (1.0 + jnp.exp(-gc))))
    mixed = jnp.concatenate([oa_ref[...], ob_ref[...], ocn.astype(BF16)], axis=-1)
    h_res = x_ref[...] + _dot(mixed, wo_ref[...])
    ms = jnp.mean(h_res * h_res, axis=-1, keepdims=True)
    hn = (h_res * lax.rsqrt(ms + EPS) * nf_ref[...]).astype(BF16)
    y = h_res
    d_ff = wu_ref.shape[1]
    for c in range(0, d_ff, f_chunk):
        u = jnp.maximum(_dot(hn, wu_ref[:, c:c + f_chunk]), 0.0)
        y = y + _dot((u * u).astype(BF16), wd_ref[c:c + f_chunk, :])
    y_ref[...] = y


def _merge(x2d, oa, ob, oc, gc, onorm, wo16, nf, wu16, wd16, *, tm):
    m, d = x2d.shape
    assert m % tm == 0
    row = lambda w: pl.BlockSpec((tm, w), lambda i: (i, 0))
    full = _resident
    return pl.pallas_call(
        functools.partial(_merge_kernel, f_chunk=1024),
        grid=(m // tm,),
        in_specs=[row(d), row(W_ATT), row(W_ATT), row(W_C), row(W_C),
                  full(onorm), full(wo16), full(nf), full(wu16), full(wd16)],
        out_specs=row(d),
        out_shape=jax.ShapeDtypeStruct((m, d), F32),
        compiler_params=_cparams(1),
        name="merge_ffn",
    )(x2d, oa, ob, oc, gc, onorm, wo16, nf, wu16, wd16)


def kernel(x_prompt, x_sample, cache_a_k, cache_a_v, cache_b_k, cache_b_v, state_c, norm_mix, w_in, qnorm_a, knorm_a, qnorm_b, knorm_b, rel_bias_b, lower_bounds, onorm_c, w_o, norm_ffn, w_up, w_down):
    depth = w_in.shape[0]
    bp, tp, d = x_prompt.shape
    bs, ts, _ = x_sample.shape
    past = cache_a_k.shape[2]
    band_rows_p = min(N_PREV_CHUNKS * CHUNK, tp)
    assert bp == 1

    xp = x_prompt.reshape(bp * tp, d)
    xs = x_sample.reshape(bs * ts, d)
    cak = cache_a_k.reshape(depth, bs, past, W_ATT)
    cav = cache_a_v.reshape(depth, bs, past, W_ATT)
    cbk = cache_b_k.reshape(depth, bs, cache_b_k.shape[2], W_ATT)
    cbv = cache_b_v.reshape(depth, bs, cache_b_v.shape[2], W_ATT)
    lbounds = lower_bounds.astype(F32)
    zero_state = jnp.zeros((bp, N_HEADS, DK_C, DK_C), F32)

    tm_p = min(512, tp)
    tq_band = min(256, tp)
    tr_p = min(256, tp)

    outs_p = [[] for _ in range(5)]
    outs_s = [[] for _ in range(5)]
    for l in range(depth):
        w16 = w_in[l].astype(BF16)
        wo16 = w_o[l].astype(BF16)
        wu16 = w_up[l].astype(BF16)
        wd16 = w_down[l].astype(BF16)
        nm = norm_mix[l].reshape(1, d)
        nf = norm_ffn[l].reshape(1, d)
        tile_h = lambda g: jnp.tile(g.reshape(1, D_HEAD), (1, N_HEADS))
        qna, kna, qnb, knb = (tile_h(g[l]) for g in (qnorm_a, knorm_a, qnorm_b, knorm_b))
        onorm = onorm_c[l].reshape(1, DK_C)
        table = rel_bias_b[l].astype(F32)

        (qa, ka, va, ka16, va16, qb, kb, vb, kb16, vb16, qc, kc, lf, ic, gc) = _proj(
            xp, nm, w16, qna, kna, qnb, knb, lbounds, layer=l, tm=tm_p)
        oa = _sb_prompt(qa, ka16, va16, tq=SB_TK)
        ob = _band_prompt(qb, kb16, vb16, table, tq=tq_band)
        oc, s_p = _hgrn(qc, kc, lf, ic, zero_state, nb=bp, t=tp, tr=tr_p)
        xp = _merge(xp, oa, ob, oc, gc, onorm, wo16, nf, wu16, wd16, tm=tm_p)
        head = lambda a, n: a.reshape(n, -1, N_HEADS, D_HEAD)
        outs_p[0].append(head(ka, bp))
        outs_p[1].append(head(va, bp))
        outs_p[2].append(head(kb, bp)[:, tp - band_rows_p:])
        outs_p[3].append(head(vb, bp)[:, tp - band_rows_p:])
        outs_p[4].append(s_p)

        (qa, ka, va, ka16, va16, qb, kb, vb, kb16, vb16, qc, kc, lf, ic, gc) = _proj(
            xs, nm, w16, qna, kna, qnb, knb, lbounds, layer=l, tm=bs * ts)
        oa = _sb_sample(qa, ka16, va16, cak, cav, layer=l, nb=bs, ts=ts)
        ob = _band_sample(qb, kb16, vb16, cbk, cbv, table, layer=l, nb=bs, ts=ts, past=past)
        oc, s_s = _hgrn(qc, kc, lf, ic, state_c[l].astype(F32), nb=bs, t=ts, tr=ts)
        xs = _merge(xs, oa, ob, oc, gc, onorm, wo16, nf, wu16, wd16, tm=bs * ts)
        outs_s[0].append(head(ka, bs))
        outs_s[1].append(head(va, bs))
        outs_s[2].append(head(kb, bs))
        outs_s[3].append(head(vb, bs))
        outs_s[4].append(s_s)

    stack = lambda xs_: jnp.stack(xs_)
    return (xp.reshape(bp, tp, d), xs.reshape(bs, ts, d),
            stack(outs_p[0]), stack(outs_p[1]), stack(outs_p[2]), stack(outs_p[3]), stack(outs_p[4]),
            stack(outs_s[0]), stack(outs_s[1]), stack(outs_s[2]), stack(outs_s[3]), stack(outs_s[4]))
```

```python
import functools
import math

import jax
import jax.numpy as jnp
from jax import lax
from jax.experimental import pallas as pl
from jax.experimental.pallas import tpu as pltpu

F32 = jnp.float32
BF16 = jnp.bfloat16

D_HEAD = 64
N_HEADS = 4
W_ATT = N_HEADS * D_HEAD
DK_C = 128
W_C = N_HEADS * DK_C
CHUNK = 64
N_PREV_CHUNKS = 8
REL_CLIP = 128
N_REL = 2 * REL_CLIP + 1
EPS = 1e-6
NEG_BIG = -1e30
LB_FLOOR = 1e-30
SB_UNDERFLOW = 88.0
SB_TK = 128
SB_TQ = 256
HGRN_SUB = 16
VMEM_LIMIT = 56 * 1024 * 1024


def _cparams(n_axes):
    return pltpu.CompilerParams(dimension_semantics=("arbitrary",) * n_axes,
                                vmem_limit_bytes=VMEM_LIMIT)


def _resident(a):
    zeros = (0,) * a.ndim
    return pl.BlockSpec(a.shape, lambda *_: zeros, pipeline_mode=pl.Buffered(1))


def _split3(x):
    h1 = x.astype(BF16)
    r1 = x - h1.astype(F32)
    h2 = r1.astype(BF16)
    h3 = (r1 - h2.astype(F32)).astype(BF16)
    return h1, h2, h3


def _dot(a, b):
    return jnp.dot(a, b, preferred_element_type=F32)


def _dot_nt(a, b):
    return lax.dot_general(a, b, (((1,), (1,)), ((), ())), preferred_element_type=F32)


def _dot_tn(a, b):
    return lax.dot_general(a, b, (((0,), (0,)), ((), ())), preferred_element_type=F32)


def _iota(shape, dim):
    return lax.broadcasted_iota(jnp.int32, shape, dim)


def _proj_kernel(x_ref, nm_ref, w_ref, qna_ref, kna_ref, qnb_ref, knb_ref, lb_ref,
                 qa_ref, ka_ref, va_ref, ka16_ref, va16_ref,
                 qb_ref, kb_ref, vb_ref, kb16_ref, vb16_ref,
                 qc_ref, kc_ref, lf_ref, ic_ref, gc_ref, *, layer):
    x = x_ref[...]
    ms = jnp.mean(x * x, axis=-1, keepdims=True)
    xn = (x * lax.rsqrt(ms + EPS) * nm_ref[...]).astype(BF16)

    def seg(a, b):
        return _dot(xn, w_ref[:, a:b])

    same_head = (_iota((W_ATT, W_ATT), 0) // D_HEAD) == (_iota((W_ATT, W_ATT), 1) // D_HEAD)
    seg_mean = jnp.where(same_head, 1.0 / D_HEAD, 0.0).astype(BF16)

    def head_norm(p, g_ref):
        sq = p * p
        hi = sq.astype(BF16)
        lo = (sq - hi.astype(F32)).astype(BF16)
        m = _dot(hi, seg_mean) + _dot(lo, seg_mean)
        return p * lax.rsqrt(m + EPS) * g_ref[...]

    inv_sqrt_d = 1.0 / math.sqrt(D_HEAD)
    o = 0
    qa_ref[...] = (head_norm(seg(o, o + W_ATT), qna_ref) * inv_sqrt_d).astype(BF16)
    o += W_ATT
    ka = head_norm(seg(o, o + W_ATT), kna_ref)
    ka_ref[...] = ka
    ka16_ref[...] = ka.astype(BF16)
    o += W_ATT
    va = seg(o, o + W_ATT)
    va_ref[...] = va
    va16_ref[...] = va.astype(BF16)
    o += W_ATT
    qb_ref[...] = (head_norm(seg(o, o + W_ATT), qnb_ref) * inv_sqrt_d).astype(BF16)
    o += W_ATT
    kb = head_norm(seg(o, o + W_ATT), knb_ref)
    kb_ref[...] = kb
    kb16_ref[...] = kb.astype(BF16)
    o += W_ATT
    vb = seg(o, o + W_ATT)
    vb_ref[...] = vb
    vb16_ref[...] = vb.astype(BF16)
    o += W_ATT

    lbr = lb_ref[...]
    e = jnp.exp(lbr - jnp.max(lbr, axis=0, keepdims=True))
    sm = e / jnp.sum(e, axis=0, keepdims=True)
    lb = jnp.sum(sm[0:layer + 1], axis=0, keepdims=True) - sm[0:1]

    qc_ref[...] = seg(o, o + W_C) * (DK_C ** -0.5)
    o += W_C
    f_raw = seg(o, o + W_C)
    o += W_C
    log_sig = jnp.minimum(f_raw, 0.0) - jnp.log1p(jnp.exp(-jnp.abs(f_raw)))
    a = jnp.log(jnp.maximum(lb, LB_FLOOR))
    b = jnp.log1p(-lb) + log_sig
    lf_ref[...] = jnp.maximum(a, b) + jnp.log1p(jnp.exp(-jnp.abs(a - b)))
    kc_ref[...] = (1.0 - lb) * (1.0 / (1.0 + jnp.exp(f_raw)))
    ic_ref[...] = seg(o, o + W_C)
    o += W_C
    gc_ref[...] = seg(o, o + W_C)


def _proj(x2d, nm, w16, qna, kna, qnb, knb, lower_bounds, *, layer, tm):
    m, d = x2d.shape
    assert m % tm == 0
    row = lambda w: pl.BlockSpec((tm, w), lambda i: (i, 0))
    full = _resident
    outs = [
        (W_ATT, BF16), (W_ATT, F32), (W_ATT, F32), (W_ATT, BF16), (W_ATT, BF16),
        (W_ATT, BF16), (W_ATT, F32), (W_ATT, F32), (W_ATT, BF16), (W_ATT, BF16),
        (W_C, F32), (W_C, F32), (W_C, F32), (W_C, F32), (W_C, F32),
    ]
    return pl.pallas_call(
        functools.partial(_proj_kernel, layer=layer),
        grid=(m // tm,),
        in_specs=[row(d), full(nm), full(w16), full(qna), full(kna), full(qnb), full(knb),
                  full(lower_bounds)],
        out_specs=[row(w) for w, _ in outs],
        out_shape=[jax.ShapeDtypeStruct((m, w), dt) for w, dt in outs],
        compiler_params=_cparams(1),
        name="proj",
    )(x2d, nm, w16, qna, kna, qnb, knb, lower_bounds)


def _head_masks(width):
    lane_head = _iota((1, width), 1) // D_HEAD
    return [lane_head == h for h in range(N_HEADS)]


def _stack_heads(x, head_lane):
    return jnp.concatenate([jnp.where(m, x, jnp.zeros_like(x)) for m in head_lane], axis=0)


def _sb_block(q_stack, kblk, vblk, mask, tri, head_lane, carry_ref, acc_ref):
    tq = acc_ref.shape[0]
    tk = kblk.shape[0]
    z = _dot_nt(q_stack, kblk)
    t = jnp.log(1.0 + jnp.exp(-jnp.abs(z)))
    log_1m = jnp.minimum(-z, 0.0) - t
    log_beta = jnp.minimum(z, 0.0) - t
    if mask is not None:
        log_1m = jnp.where(mask, log_1m, 0.0)
    hi = log_1m.astype(BF16)
    lo = (log_1m - hi.astype(F32)).astype(BF16)
    later = _dot(jnp.concatenate([hi, lo], axis=1), tri)
    carry = carry_ref[...]
    w = jnp.exp(log_beta + later + carry[:, :tk])
    if mask is not None:
        w = jnp.where(mask, w, 0.0)
    w = w.astype(BF16)
    w_heads = jnp.concatenate([w[h * tq:(h + 1) * tq] for h in range(N_HEADS)], axis=1)
    acc_ref[...] += _dot(w_heads, _stack_heads(vblk, head_lane))
    carry_ref[...] = carry + jnp.sum(log_1m, axis=-1, keepdims=True)


def _sb_live(carry_ref):
    return (jnp.max(carry_ref[...]) > -SB_UNDERFLOW).astype(jnp.int32)


def _strict_upper(n):
    return jnp.where(_iota((2 * n, n), 0) % n > _iota((2 * n, n), 1), 1.0, 0.0).astype(BF16)


def _sb_prompt_kernel(q_ref, k_ref, v_ref, o_ref, carry_ref, acc_ref, *, tq):
    i = pl.program_id(0)
    head_lane = _head_masks(W_ATT)
    q_stack = _stack_heads(q_ref[...], head_lane)
    tri = _strict_upper(SB_TK)
    carry_ref[...] = jnp.zeros_like(carry_ref)
    acc_ref[...] = jnp.zeros_like(acc_ref)

    def add_block(j, mask):
        r0 = pl.multiple_of(j * SB_TK, SB_TK)
        _sb_block(q_stack, k_ref[pl.ds(r0, SB_TK), :], v_ref[pl.ds(r0, SB_TK), :], mask, tri,
                  head_lane, carry_ref, acc_ref)

    shape = (N_HEADS * tq, SB_TK)
    diag_blocks = tq // SB_TK
    for d in reversed(range(diag_blocks)):
        add_block(i * diag_blocks + d, d * SB_TK + _iota(shape, 1) < _iota(shape, 0) % tq)

    def cond(c):
        j, live = c
        return jnp.logical_and(j >= 0, live > 0)

    def body(c):
        j, _ = c
        add_block(j, None)
        return j - 1, _sb_live(carry_ref)

    lax.while_loop(cond, body, (i * diag_blocks - 1, _sb_live(carry_ref)))
    o_ref[...] = acc_ref[...].astype(o_ref.dtype)


def _sb_prompt(q16, k16, v16, *, tq):
    t = q16.shape[0]
    assert t % tq == 0 and tq % SB_TK == 0
    return pl.pallas_call(
        functools.partial(_sb_prompt_kernel, tq=tq),
        grid=(t // tq,),
        in_specs=[pl.BlockSpec((tq, W_ATT), lambda i: (i, 0)),
                  _resident(k16), _resident(v16)],
        out_specs=pl.BlockSpec((tq, W_ATT), lambda i: (i, 0)),
        out_shape=jax.ShapeDtypeStruct((t, W_ATT), BF16),
        scratch_shapes=[pltpu.VMEM((N_HEADS * tq, SB_TK), F32), pltpu.VMEM((tq, W_ATT), F32)],
        compiler_params=_cparams(1),
        name="sb_prompt",
    )(q16, k16, v16)


def _sb_sample_kernel(q_ref, kn_ref, vn_ref, ck_ref, cv_ref, o_ref, carry_ref, acc_ref, *, ts, past):
    head_lane = _head_masks(W_ATT)
    q_stack = _stack_heads(q_ref[...], head_lane)
    carry_ref[...] = jnp.zeros_like(carry_ref)
    acc_ref[...] = jnp.zeros_like(acc_ref)
    shape = (N_HEADS * ts, ts)
    mask = _iota(shape, 1) < _iota(shape, 0) % ts
    _sb_block(q_stack, kn_ref[...], vn_ref[...], mask, _strict_upper(ts), head_lane,
              carry_ref, acc_ref)
    tri = _strict_upper(SB_TK)

    def cond(c):
        j, live = c
        return jnp.logical_and(j >= 0, live > 0)

    def body(c):
        j, _ = c
        r0 = pl.multiple_of(j * SB_TK, SB_TK)
        kblk = ck_ref[0, 0, pl.ds(r0, SB_TK), :].astype(BF16)
        vblk = cv_ref[0, 0, pl.ds(r0, SB_TK), :].astype(BF16)
        _sb_block(q_stack, kblk, vblk, None, tri, head_lane, carry_ref, acc_ref)
        return j - 1, _sb_live(carry_ref)

    lax.while_loop(cond, body, (past // SB_TK - 1, _sb_live(carry_ref)))
    o_ref[...] = acc_ref[...].astype(o_ref.dtype)


def _sb_sample(q16, kn16, vn16, cache_k, cache_v, *, layer, nb, ts):
    past = cache_k.shape[2]
    assert past % SB_TK == 0
    row = pl.BlockSpec((ts, W_ATT), lambda b: (b, 0))
    cache = pl.BlockSpec((1, 1, past, W_ATT), lambda b: (layer, b, 0, 0))
    return pl.pallas_call(
        functools.partial(_sb_sample_kernel, ts=ts, past=past),
        grid=(nb,),
        in_specs=[row, row, row, cache, cache],
        out_specs=row,
        out_shape=jax.ShapeDtypeStruct((nb * ts, W_ATT), BF16),
        scratch_shapes=[pltpu.VMEM((N_HEADS * ts, SB_TK), F32), pltpu.VMEM((ts, W_ATT), F32)],
        compiler_params=_cparams(1),
        name="sb_sample",
    )(q16, kn16, vn16, cache_k, cache_v)


def _rel_bias(table_ref, h, dist):
    idx = jnp.clip(dist, -REL_CLIP, REL_CLIP) + REL_CLIP

    def body(r, b):
        return jnp.where(idx == r, table_ref[h, r], b)

    return lax.fori_loop(0, N_REL, body, jnp.zeros(dist.shape, F32))


def _row_reduce(x, combine, reduce):
    width = x.shape[1]
    acc = x[:, :128]
    for c in range(128, width, 128):
        acc = combine(acc, x[:, c:c + 128])
    return reduce(acc, axis=-1, keepdims=True)


def _toeplitz_bias(table_ref, h, rows, back, width):
    lane = _iota((8, width), 1)
    by_diag = _rel_bias(table_ref, h, back + rows - lane)
    tiled = jnp.concatenate([by_diag] * (rows // 8), axis=0)
    return pltpu.roll(tiled, width - rows, 1, stride=1, stride_axis=0)


BAND_TQ = 128
BAND_BACK = N_PREV_CHUNKS * CHUNK
BAND_WIN = BAND_BACK + BAND_TQ


def _band_prompt_kernel(table_ref, q_ref, k0_ref, k1_ref, k2_ref, v0_ref, v1_ref, v2_ref,
                        o_ref, bias_ref, *, tq):
    i = pl.program_id(0)
    n_kb = 3
    rows = _iota((BAND_TQ, BAND_WIN), 0)
    cols = _iota((BAND_TQ, BAND_WIN), 1)

    @pl.when(i == 0)
    def _():
        q_chunk = rows // CHUNK + N_PREV_CHUNKS
        k_chunk = cols // CHUNK
        in_band = jnp.logical_and(k_chunk >= q_chunk - N_PREV_CHUNKS, k_chunk <= q_chunk)
        for h in range(N_HEADS):
            bias = _toeplitz_bias(table_ref, h, BAND_TQ, BAND_BACK, BAND_WIN + BAND_TQ)
            bias_ref[h] = jnp.where(in_band, bias[:, :BAND_WIN], NEG_BIG)

    q = q_ref[...]
    head_lane = _head_masks(W_ATT)
    kcat = jnp.concatenate([k0_ref[...], k1_ref[...], k2_ref[...]], axis=0)
    vcat = jnp.concatenate([v0_ref[...], v1_ref[...], v2_ref[...]], axis=0)
    parts = [p * BAND_TQ for p in range(tq // BAND_TQ)]
    starts = [r0 + (n_kb - 1) * tq - BAND_BACK for r0 in parts]
    pairs = [(p, h) for p in range(len(parts)) for h in range(N_HEADS)]
    k_t = kcat.T
    scores = []
    for p, h in pairs:
        qp = q[parts[p]:parts[p] + BAND_TQ]
        qh = jnp.where(head_lane[h], qp, jnp.zeros_like(qp))
        scores.append(_dot(qh, k_t[:, starts[p]:starts[p] + BAND_WIN]))
    weights = []
    inv_den = []
    for (p, h), s in zip(pairs, scores):
        exists = (i - (n_kb - 1)) * tq + starts[p] + cols >= 0
        s = jnp.where(exists, s + bias_ref[h], NEG_BIG)
        e = jnp.exp(s - _row_reduce(s, jnp.maximum, jnp.max))
        weights.append(e.astype(BF16))
        inv_den.append(1.0 / _row_reduce(e, jnp.add, jnp.sum))
    for p in range(len(parts)):
        out = jnp.zeros((BAND_TQ, W_ATT), F32)
        for h in range(N_HEADS):
            n = p * N_HEADS + h
            pv = _dot(weights[n], vcat[starts[p]:starts[p] + BAND_WIN]) * inv_den[n]
            out = jnp.where(head_lane[h], pv, out)
        o_ref[parts[p]:parts[p] + BAND_TQ, :] = out.astype(o_ref.dtype)


def _band_prompt(q16, k16, v16, table, *, tq):
    t = q16.shape[0]
    assert t % tq == 0 and tq % BAND_TQ == 0 and 2 * tq >= BAND_BACK
    blk = lambda back: pl.BlockSpec((tq, W_ATT), lambda i: (jnp.maximum(i - back, 0), 0))
    return pl.pallas_call(
        functools.partial(_band_prompt_kernel, tq=tq),
        grid=(t // tq,),
        in_specs=[pl.BlockSpec(memory_space=pltpu.SMEM),
                  blk(0), blk(2), blk(1), blk(0), blk(2), blk(1), blk(0)],
        out_specs=blk(0),
        out_shape=jax.ShapeDtypeStruct((t, W_ATT), BF16),
        scratch_shapes=[pltpu.VMEM((N_HEADS, BAND_TQ, BAND_WIN), F32)],
        compiler_params=_cparams(1),
        name="band_prompt",
    )(table, q16, k16, k16, k16, v16, v16, v16)


def _band_sample_kernel(table_ref, q_ref, kn_ref, vn_ref, ck_ref, cv_ref, o_ref, bias_ref, *, ts, past):
    lb = ck_ref.shape[2]

    @pl.when(pl.program_id(0) == 0)
    def _():
        for h in range(N_HEADS):
            bias_ref[h] = _toeplitz_bias(table_ref, h, ts, lb, bias_ref.shape[2])

    q = q_ref[...]
    head_lane = _head_masks(W_ATT)
    kc = ck_ref[0, 0].astype(BF16)
    vc = cv_ref[0, 0].astype(BF16)
    kn = kn_ref[...]
    vn = vn_ref[...]
    q_pos_c = past + _iota((ts, lb), 0)
    k_pos_c = past - lb + _iota((ts, lb), 1)
    q_pos_n = past + _iota((ts, ts), 0)
    k_pos_n = past + _iota((ts, ts), 1)

    def allowed(q_pos, k_pos):
        qc, kc_ = q_pos // CHUNK, k_pos // CHUNK
        return jnp.logical_and(kc_ >= qc - N_PREV_CHUNKS, kc_ <= qc)

    ok_c = allowed(q_pos_c, k_pos_c)
    ok_n = allowed(q_pos_n, k_pos_n)
    out = jnp.zeros((ts, W_ATT), F32)
    for h in range(N_HEADS):
        qh = jnp.where(head_lane[h], q, jnp.zeros_like(q))
        bias = bias_ref[h]
        s_c = jnp.where(ok_c, _dot_nt(qh, kc) + bias[:, :lb], NEG_BIG)
        s_n = jnp.where(ok_n, _dot_nt(qh, kn) + bias[:, lb:lb + ts], NEG_BIG)
        m = jnp.maximum(jnp.max(s_c, axis=-1, keepdims=True), jnp.max(s_n, axis=-1, keepdims=True))
        e_c = jnp.exp(s_c - m)
        e_n = jnp.exp(s_n - m)
        den = jnp.sum(e_c, axis=-1, keepdims=True) + jnp.sum(e_n, axis=-1, keepdims=True)
        pv = (_dot(e_c.astype(BF16), vc) + _dot(e_n.astype(BF16), vn)) / den
        out = jnp.where(head_lane[h], pv, out)
    o_ref[...] = out.astype(o_ref.dtype)


def _band_sample(q16, kn16, vn16, cache_k, cache_v, table, *, layer, nb, ts, past):
    lb = cache_k.shape[2]
    row = pl.BlockSpec((ts, W_ATT), lambda b: (b, 0))
    cache = pl.BlockSpec((1, 1, lb, W_ATT), lambda b: (layer, b, 0, 0))
    return pl.pallas_call(
        functools.partial(_band_sample_kernel, ts=ts, past=past),
        grid=(nb,),
        in_specs=[pl.BlockSpec(memory_space=pltpu.SMEM), row, row, row, cache, cache],
        out_specs=row,
        out_shape=jax.ShapeDtypeStruct((nb * ts, W_ATT), BF16),
        scratch_shapes=[pltpu.VMEM((N_HEADS, ts, pl.cdiv(lb + 2 * ts, 128) * 128), F32)],
        compiler_params=_cparams(1),
        name="band_sample",
    )(table, q16, kn16, vn16, cache_k, cache_v)


def _hgrn_kernel(q_ref, k_ref, lf_ref, v_ref, s0_ref, o_ref, sout_ref,
                 st_ref, g_ref, dec_ref, qt_ref, kd_ref, *, tr):
    r = pl.program_id(1)

    @pl.when(r == 0)
    def _():
        for h in range(N_HEADS):
            st_ref[h] = s0_ref[0, h].T

    sub = HGRN_SUB
    half = sub // 2
    heads = [slice(h * DK_C, (h + 1) * DK_C) for h in range(N_HEADS)]

    same_sub = (_iota((tr, tr), 0) // sub) == (_iota((tr, tr), 1) // sub)
    lower = _iota((tr, tr), 0) >= _iota((tr, tr), 1)
    tri = jnp.where(jnp.logical_and(same_sub, lower), 1.0, 0.0).astype(BF16)
    f1, f2, f3 = _split3(lf_ref[...])
    g_all = _dot(tri, f1) + _dot(tri, f2) + _dot(tri, f3)
    g_3d = g_all.reshape(tr // sub, sub, W_C)
    g_end = jnp.broadcast_to(g_3d[:, sub - 1:sub, :], g_3d.shape).reshape(tr, W_C)
    g_ref[...] = g_all
    dec_ref[...] = jnp.exp(g_end)
    qt_ref[...] = (q_ref[...] * jnp.exp(g_all)).astype(BF16)
    kd_ref[...] = (k_ref[...] * jnp.exp(g_end - g_all)).astype(BF16)
    row_id = _iota((half, 1), 0)

    def block(n, carry):
        r0 = pl.multiple_of(n * sub, sub)
        rows = pl.ds(r0, sub)
        v = v_ref[rows, :]
        v16 = v.astype(BF16)
        kd = kd_ref[rows, :]
        qt = qt_ref[rows, :]
        upd = [_dot_tn(v16[:, hs], kd[:, hs]) for hs in heads]
        o_top = []
        o_bot = []
        for h, hs in enumerate(heads):
            o_h = _dot_nt(qt[:, hs], st_ref[h].astype(BF16))
            o_top.append(o_h[:half])
            o_bot.append(o_h[half:])
        g = g_ref[rows, :]
        q = q_ref[rows, :]
        k = k_ref[rows, :]
        g_t, g_b = g[:half], g[half:]
        q_t, q_b = q[:half], q[half:]
        for j in range(sub):
            gj, kj, vj = g[j:j + 1, :], k[j:j + 1, :], v[j:j + 1, :]
            p_b = q_b * (kj * jnp.exp(g_b - gj))
            if j < half:
                p_t = q_t * (kj * jnp.exp(g_t - gj))
            for h, hs in enumerate(heads):
                a_b = jnp.sum(p_b[:, hs], axis=-1, keepdims=True)
                if j < half:
                    a_t = jnp.sum(p_t[:, hs], axis=-1, keepdims=True)
                    o_top[h] = o_top[h] + jnp.where(row_id >= j, a_t, 0.0) * vj[:, hs]
                else:
                    a_b = jnp.where(row_id >= j - half, a_b, 0.0)
                o_bot[h] = o_bot[h] + a_b * vj[:, hs]
        o_ref[rows, :] = jnp.concatenate(
            [jnp.concatenate(o_top, axis=-1), jnp.concatenate(o_bot, axis=-1)], axis=0)
        dec = dec_ref[pl.ds(r0, 1), :]
        for h, hs in enumerate(heads):
            st_ref[h] = st_ref[h] * dec[:, hs] + upd[h]
        return carry

    lax.fori_loop(0, tr // sub, block, 0, unroll=2)

    @pl.when(r == pl.num_programs(1) - 1)
    def _():
        for h in range(N_HEADS):
            sout_ref[0, h] = st_ref[h].T


def _hgrn(qc, kc, lf, ic, s0, *, nb, t, tr):
    assert t % tr == 0 and tr % HGRN_SUB == 0
    nr = t // tr
    row = pl.BlockSpec((tr, W_C), lambda b, r: (b * nr + r, 0))
    state = pl.BlockSpec((1, N_HEADS, DK_C, DK_C), lambda b, r: (b, 0, 0, 0))
    return pl.pallas_call(
        functools.partial(_hgrn_kernel, tr=tr),
        grid=(nb, nr),
        in_specs=[row, row, row, row, state],
        out_specs=[row, state],
        out_shape=[jax.ShapeDtypeStruct((nb * t, W_C), F32),
                   jax.ShapeDtypeStruct((nb, N_HEADS, DK_C, DK_C), F32)],
        scratch_shapes=[pltpu.VMEM((N_HEADS, DK_C, DK_C), F32),
                        pltpu.VMEM((tr, W_C), F32), pltpu.VMEM((tr, W_C), F32),
                        pltpu.VMEM((tr, W_C), BF16), pltpu.VMEM((tr, W_C), BF16)],
        compiler_params=_cparams(2),
        name="hgrn",
    )(qc, kc, lf, ic, s0)


def _merge_kernel(x_ref, oa_ref, ob_ref, oc_ref, gc_ref, on_ref, wo_ref, nf_ref, wu_ref, wd_ref,
                  y_ref, *, f_chunk):
    oc = oc_ref[...]
    gc = gc_ref[...]
    parts = []
    for h in range(N_HEADS):
        cs = slice(h * DK_C, (h + 1) * DK_C)
        och = oc[:, cs]
        ms = jnp.mean(och * och, axis=-1, keepdims=True)
        parts.append(och * lax.rsqrt(ms + EPS) * on_ref[...])
    ocn = jnp.concatenate(parts, axis=-1) * (gc * (1.0 / (1.0 + jnp.exp(-gc))))
    mixed = jnp.concatenate([oa_ref[...], ob_ref[...], ocn.astype(BF16)], axis=-1)
    h_res = x_ref[...] + _dot(mixed, wo_ref[...])
    ms = jnp.mean(h_res * h_res, axis=-1, keepdims=True)
    hn = (h_res * lax.rsqrt(ms + EPS) * nf_ref[...]).astype(BF16)
    y = h_res
    d_ff = wu_ref.shape[1]
    for c in range(0, d_ff, f_chunk):
        u = jnp.maximum(_dot(hn, wu_ref[:, c:c + f_chunk]), 0.0)
        y = y + _dot((u * u).astype(BF16), wd_ref[c:c + f_chunk, :])
    y_ref[...] = y


def _merge(x2d, oa, ob, oc, gc, onorm, wo16, nf, wu16, wd16, *, tm):
    m, d = x2d.shape
    assert m % tm == 0
    row = lambda w: pl.BlockSpec((tm, w), lambda i: (i, 0))
    full = _resident
    return pl.pallas_call(
        functools.partial(_merge_kernel, f_chunk=1024),
        grid=(m // tm,),
        in_specs=[row(d), row(W_ATT), row(W_ATT), row(W_C), row(W_C),
                  full(onorm), full(wo16), full(nf), full(wu16), full(wd16)],
        out_specs=row(d),
        out_shape=jax.ShapeDtypeStruct((m, d), F32),
        compiler_params=_cparams(1),
        name="merge_ffn",
    )(x2d, oa, ob, oc, gc, onorm, wo16, nf, wu16, wd16)


def kernel(x_prompt, x_sample, cache_a_k, cache_a_v, cache_b_k, cache_b_v, state_c, norm_mix, w_in, qnorm_a, knorm_a, qnorm_b, knorm_b, rel_bias_b, lower_bounds, onorm_c, w_o, norm_ffn, w_up, w_down):
    depth = w_in.shape[0]
    bp, tp, d = x_prompt.shape
    bs, ts, _ = x_sample.shape
    past = cache_a_k.shape[2]
    band_rows_p = min(N_PREV_CHUNKS * CHUNK, tp)
    assert bp == 1

    xp = x_prompt.reshape(bp * tp, d)
    xs = x_sample.reshape(bs * ts, d)
    cak = cache_a_k.reshape(depth, bs, past, W_ATT)
    cav = cache_a_v.reshape(depth, bs, past, W_ATT)
    cbk = cache_b_k.reshape(depth, bs, cache_b_k.shape[2], W_ATT)
    cbv = cache_b_v.reshape(depth, bs, cache_b_v.shape[2], W_ATT)
    lbounds = lower_bounds.astype(F32)
    zero_state = jnp.zeros((bp, N_HEADS, DK_C, DK_C), F32)

    tm_p = min(512, tp)
    tq_band = min(256, tp)
    tr_p = min(256, tp)

    outs_p = [[] for _ in range(5)]
    outs_s = [[] for _ in range(5)]
    for l in range(depth):
        w16 = w_in[l].astype(BF16)
        wo16 = w_o[l].astype(BF16)
        wu16 = w_up[l].astype(BF16)
        wd16 = w_down[l].astype(BF16)
        nm = norm_mix[l].reshape(1, d)
        nf = norm_ffn[l].reshape(1, d)
        tile_h = lambda g: jnp.tile(g.reshape(1, D_HEAD), (1, N_HEADS))
        qna, kna, qnb, knb = (tile_h(g[l]) for g in (qnorm_a, knorm_a, qnorm_b, knorm_b))
        onorm = onorm_c[l].reshape(1, DK_C)
        table = rel_bias_b[l].astype(F32)

        (qa, ka, va, ka16, va16, qb, kb, vb, kb16, vb16, qc, kc, lf, ic, gc) = _proj(
            xp, nm, w16, qna, kna, qnb, knb, lbounds, layer=l, tm=tm_p)
        oa = _sb_prompt(qa, ka16, va16, tq=min(SB_TQ, tp))
        ob = _band_prompt(qb, kb16, vb16, table, tq=tq_band)
        oc, s_p = _hgrn(qc, kc, lf, ic, zero_state, nb=bp, t=tp, tr=tr_p)
        xp = _merge(xp, oa, ob, oc, gc, onorm, wo16, nf, wu16, wd16, tm=tm_p)
        head = lambda a, n: a.reshape(n, -1, N_HEADS, D_HEAD)
        outs_p[0].append(head(ka, bp))
        outs_p[1].append(head(va, bp))
        outs_p[2].append(head(kb, bp)[:, tp - band_rows_p:])
        outs_p[3].append(head(vb, bp)[:, tp - band_rows_p:])
        outs_p[4].append(s_p)

        (qa, ka, va, ka16, va16, qb, kb, vb, kb16, vb16, qc, kc, lf, ic, gc) = _proj(
            xs, nm, w16, qna, kna, qnb, knb, lbounds, layer=l, tm=bs * ts)
        oa = _sb_sample(qa, ka16, va16, cak, cav, layer=l, nb=bs, ts=ts)
        ob = _band_sample(qb, kb16, vb16, cbk, cbv, table, layer=l, nb=bs, ts=ts, past=past)
        oc, s_s = _hgrn(qc, kc, lf, ic, state_c[l].astype(F32), nb=bs, t=ts, tr=ts)
        xs = _merge(xs, oa, ob, oc, gc, onorm, wo16, nf, wu16, wd16, tm=bs * ts)
        outs_s[0].append(head(ka, bs))
        outs_s[1].append(head(va, bs))
        outs_s[2].append(head(kb, bs))
        outs_s[3].append(head(vb, bs))
        outs_s[4].append(s_s)

    stack = lambda xs_: jnp.stack(xs_)
    return (xp.reshape(bp, tp, d), xs.reshape(bs, ts, d),
            stack(outs_p[0]), stack(outs_p[1]), stack(outs_p[2]), stack(outs_p[3]), stack(outs_p[4]),
            stack(outs_s[0]), stack(outs_s[1]), stack(outs_s[2]), stack(outs_s[3]), stack(outs_s[4]))
```

```python
import functools
import math

import jax
import jax.numpy as jnp
from jax import lax
from jax.experimental import pallas as pl
from jax.experimental.pallas import tpu as pltpu

F32 = jnp.float32
BF16 = jnp.bfloat16

D_HEAD = 64
N_HEADS = 4
W_ATT = N_HEADS * D_HEAD
DK_C = 128
W_C = N_HEADS * DK_C
CHUNK = 64
N_PREV_CHUNKS = 8
REL_CLIP = 128
N_REL = 2 * REL_CLIP + 1
EPS = 1e-6
NEG_BIG = -1e30
LB_FLOOR = 1e-30
SB_UNDERFLOW = 88.0
SB_TK = 128
SB_TQ = 256
HGRN_SUB = 16
VMEM_LIMIT = 56 * 1024 * 1024


def _cparams(n_axes):
    return pltpu.CompilerParams(dimension_semantics=("arbitrary",) * n_axes,
                                vmem_limit_bytes=VMEM_LIMIT)


def _resident(a):
    zeros = (0,) * a.ndim
    return pl.BlockSpec(a.shape, lambda *_: zeros, pipeline_mode=pl.Buffered(1))


def _split3(x):
    h1 = x.astype(BF16)
    r1 = x - h1.astype(F32)
    h2 = r1.astype(BF16)
    h3 = (r1 - h2.astype(F32)).astype(BF16)
    return h1, h2, h3


def _dot(a, b):
    return jnp.dot(a, b, preferred_element_type=F32)


def _dot_nt(a, b):
    return lax.dot_general(a, b, (((1,), (1,)), ((), ())), preferred_element_type=F32)


def _dot_tn(a, b):
    return lax.dot_general(a, b, (((0,), (0,)), ((), ())), preferred_element_type=F32)


def _iota(shape, dim):
    return lax.broadcasted_iota(jnp.int32, shape, dim)


def _proj_kernel(x_ref, nm_ref, w_ref, qna_ref, kna_ref, qnb_ref, knb_ref, lb_ref,
                 qa_ref, ka_ref, va_ref, ka16_ref, va16_ref,
                 qb_ref, kb_ref, vb_ref, kb16_ref, vb16_ref,
                 qc_ref, kc_ref, lf_ref, ic_ref, gc_ref, *, layer, time_minor):
    x = x_ref[...]
    ms = jnp.mean(x * x, axis=-1, keepdims=True)
    xn = (x * lax.rsqrt(ms + EPS) * nm_ref[...]).astype(BF16)

    def seg(a, b):
        return _dot(xn, w_ref[:, a:b])

    same_head = (_iota((W_ATT, W_ATT), 0) // D_HEAD) == (_iota((W_ATT, W_ATT), 1) // D_HEAD)
    seg_mean = jnp.where(same_head, 1.0 / D_HEAD, 0.0).astype(BF16)

    def head_norm(p, g_ref):
        sq = p * p
        hi = sq.astype(BF16)
        lo = (sq - hi.astype(F32)).astype(BF16)
        m = _dot(hi, seg_mean) + _dot(lo, seg_mean)
        return p * lax.rsqrt(m + EPS) * g_ref[...]

    inv_sqrt_d = 1.0 / math.sqrt(D_HEAD)
    o = 0
    qa_ref[...] = (head_norm(seg(o, o + W_ATT), qna_ref) * inv_sqrt_d).astype(BF16)
    o += W_ATT
    def put_k(k, k_ref, k16_ref):
        k = k.T if time_minor else k
        k_ref[...] = k
        k16_ref[...] = k.astype(BF16)

    def put_v(v, v_ref, v16_ref):
        v_ref[...] = v.T if time_minor else v
        v16_ref[...] = v.astype(BF16)

    put_k(head_norm(seg(o, o + W_ATT), kna_ref), ka_ref, ka16_ref)
    o += W_ATT
    put_v(seg(o, o + W_ATT), va_ref, va16_ref)
    o += W_ATT
    qb_ref[...] = (head_norm(seg(o, o + W_ATT), qnb_ref) * inv_sqrt_d).astype(BF16)
    o += W_ATT
    put_k(head_norm(seg(o, o + W_ATT), knb_ref), kb_ref, kb16_ref)
    o += W_ATT
    put_v(seg(o, o + W_ATT), vb_ref, vb16_ref)
    o += W_ATT

    lbr = lb_ref[...]
    e = jnp.exp(lbr - jnp.max(lbr, axis=0, keepdims=True))
    sm = e / jnp.sum(e, axis=0, keepdims=True)
    lb = jnp.sum(sm[0:layer + 1], axis=0, keepdims=True) - sm[0:1]

    qc_ref[...] = seg(o, o + W_C) * (DK_C ** -0.5)
    o += W_C
    f_raw = seg(o, o + W_C)
    o += W_C
    log_sig = jnp.minimum(f_raw, 0.0) - jnp.log1p(jnp.exp(-jnp.abs(f_raw)))
    a = jnp.log(jnp.maximum(lb, LB_FLOOR))
    b = jnp.log1p(-lb) + log_sig
    lf_ref[...] = jnp.maximum(a, b) + jnp.log1p(jnp.exp(-jnp.abs(a - b)))
    kc_ref[...] = (1.0 - lb) * (1.0 / (1.0 + jnp.exp(f_raw)))
    ic_ref[...] = seg(o, o + W_C)
    o += W_C
    gc_ref[...] = seg(o, o + W_C)


def _proj(x2d, nm, w16, qna, kna, qnb, knb, lower_bounds, *, layer, tm, time_minor):
    m, d = x2d.shape
    assert m % tm == 0
    full = _resident
    row = lambda w, dt: (pl.BlockSpec((tm, w), lambda i: (i, 0)), jax.ShapeDtypeStruct((m, w), dt))
    col = lambda w, dt: (pl.BlockSpec((w, tm), lambda i: (0, i)), jax.ShapeDtypeStruct((w, m), dt))
    kv = col if time_minor else row
    att = [row(W_ATT, BF16), kv(W_ATT, F32), kv(W_ATT, F32), kv(W_ATT, BF16), row(W_ATT, BF16)]
    outs = att + att + [row(W_C, F32)] * 5
    return pl.pallas_call(
        functools.partial(_proj_kernel, layer=layer, time_minor=time_minor),
        grid=(m // tm,),
        in_specs=[row(d, F32)[0], full(nm), full(w16), full(qna), full(kna), full(qnb), full(knb),
                  full(lower_bounds)],
        out_specs=[spec for spec, _ in outs],
        out_shape=[shape for _, shape in outs],
        compiler_params=_cparams(1),
        name="proj",
    )(x2d, nm, w16, qna, kna, qnb, knb, lower_bounds)


def _head_masks(width):
    lane_head = _iota((1, width), 1) // D_HEAD
    return [lane_head == h for h in range(N_HEADS)]


def _stack_heads(x, head_lane):
    return jnp.concatenate([jnp.where(m, x, jnp.zeros_like(x)) for m in head_lane], axis=0)


def _stack_heads_t(x_t, n):
    row_head = _iota((W_ATT, 1), 0) // D_HEAD
    return jnp.concatenate(
        [jnp.where(row_head == h, x_t, jnp.zeros_like(x_t)) for h in range(N_HEADS)], axis=1)


def _sb_block(z, weigh_values, mask, tri, carry_ref, acc_ref):
    tq = acc_ref.shape[0]
    tk = z.shape[1]
    t = jnp.log(1.0 + jnp.exp(-jnp.abs(z)))
    log_1m = jnp.minimum(-z, 0.0) - t
    log_beta = jnp.minimum(z, 0.0) - t
    if mask is not None:
        log_1m = jnp.where(mask, log_1m, 0.0)
    hi = log_1m.astype(BF16)
    lo = (log_1m - hi.astype(F32)).astype(BF16)
    later = _dot(jnp.concatenate([hi, lo], axis=1), tri)
    carry = carry_ref[...]
    w = jnp.exp(log_beta + later + carry[:, :tk])
    if mask is not None:
        w = jnp.where(mask, w, 0.0)
    w = w.astype(BF16)
    w_heads = jnp.concatenate([w[h * tq:(h + 1) * tq] for h in range(N_HEADS)], axis=1)
    acc_ref[...] += weigh_values(w_heads)
    carry_ref[...] = carry + jnp.sum(log_1m, axis=-1, keepdims=True)


def _sb_live(carry_ref):
    return (jnp.max(carry_ref[...]) > -SB_UNDERFLOW).astype(jnp.int32)


def _strict_upper(n):
    return jnp.where(_iota((2 * n, n), 0) % n > _iota((2 * n, n), 1), 1.0, 0.0).astype(BF16)


def _sb_prompt_kernel(q_ref, kt_ref, v_ref, o_ref, carry_ref, acc_ref, *, tq):
    i = pl.program_id(0)
    head_lane = _head_masks(W_ATT)
    q_stack = _stack_heads(q_ref[...], head_lane)
    tri = _strict_upper(SB_TK)
    carry_ref[...] = jnp.zeros_like(carry_ref)
    acc_ref[...] = jnp.zeros_like(acc_ref)

    def add_block(j, mask):
        r0 = pl.multiple_of(j * SB_TK, SB_TK)
        z = _dot(q_stack, kt_ref[:, pl.ds(r0, SB_TK)])
        v_stack = _stack_heads(v_ref[pl.ds(r0, SB_TK), :], head_lane)
        _sb_block(z, lambda w: _dot(w, v_stack), mask, tri, carry_ref, acc_ref)

    shape = (N_HEADS * tq, SB_TK)
    diag_blocks = tq // SB_TK
    for d in reversed(range(diag_blocks)):
        add_block(i * diag_blocks + d, d * SB_TK + _iota(shape, 1) < _iota(shape, 0) % tq)

    def cond(c):
        j, live = c
        return jnp.logical_and(j >= 0, live > 0)

    def body(c):
        j, _ = c
        add_block(j, None)
        return j - 1, _sb_live(carry_ref)

    lax.while_loop(cond, body, (i * diag_blocks - 1, _sb_live(carry_ref)))
    o_ref[...] = acc_ref[...].astype(o_ref.dtype)


def _sb_prompt(q16, kt16, v16, *, tq):
    t = q16.shape[0]
    assert t % tq == 0 and tq % SB_TK == 0
    return pl.pallas_call(
        functools.partial(_sb_prompt_kernel, tq=tq),
        grid=(t // tq,),
        in_specs=[pl.BlockSpec((tq, W_ATT), lambda i: (i, 0)),
                  _resident(kt16), _resident(v16)],
        out_specs=pl.BlockSpec((tq, W_ATT), lambda i: (i, 0)),
        out_shape=jax.ShapeDtypeStruct((t, W_ATT), BF16),
        scratch_shapes=[pltpu.VMEM((N_HEADS * tq, SB_TK), F32), pltpu.VMEM((tq, W_ATT), F32)],
        compiler_params=_cparams(1),
        name="sb_prompt",
    )(q16, kt16, v16)


def _sb_sample_kernel(q_ref, kn_ref, vn_ref, ckt_ref, cvt_ref, o_ref, carry_ref, acc_ref, *, ts, past):
    head_lane = _head_masks(W_ATT)
    q_stack = _stack_heads(q_ref[...], head_lane)
    carry_ref[...] = jnp.zeros_like(carry_ref)
    acc_ref[...] = jnp.zeros_like(acc_ref)
    shape = (N_HEADS * ts, ts)
    mask = _iota(shape, 1) < _iota(shape, 0) % ts
    v_new = _stack_heads(vn_ref[...], head_lane)
    _sb_block(_dot_nt(q_stack, kn_ref[...]), lambda w: _dot(w, v_new), mask, _strict_upper(ts),
              carry_ref, acc_ref)
    tri = _strict_upper(SB_TK)

    def cond(c):
        j, live = c
        return jnp.logical_and(j >= 0, live > 0)

    def body(c):
        j, _ = c
        r0 = pl.multiple_of(j * SB_TK, SB_TK)
        z = _dot(q_stack, ckt_ref[0, 0, :, pl.ds(r0, SB_TK)].astype(BF16))
        vt_stack = _stack_heads_t(cvt_ref[0, 0, :, pl.ds(r0, SB_TK)].astype(BF16), SB_TK)
        _sb_block(z, lambda w: _dot_nt(w, vt_stack), None, tri, carry_ref, acc_ref)
        return j - 1, _sb_live(carry_ref)

    lax.while_loop(cond, body, (past // SB_TK - 1, _sb_live(carry_ref)))
    o_ref[...] = acc_ref[...].astype(o_ref.dtype)


def _sb_sample(q16, kn16, vn16, cache_kt, cache_vt, *, layer, nb, ts):
    past = cache_kt.shape[3]
    assert past % SB_TK == 0
    row = pl.BlockSpec((ts, W_ATT), lambda b: (b, 0))
    cache = pl.BlockSpec((1, 1, W_ATT, past), lambda b: (layer, b, 0, 0))
    return pl.pallas_call(
        functools.partial(_sb_sample_kernel, ts=ts, past=past),
        grid=(nb,),
        in_specs=[row, row, row, cache, cache],
        out_specs=row,
        out_shape=jax.ShapeDtypeStruct((nb * ts, W_ATT), BF16),
        scratch_shapes=[pltpu.VMEM((N_HEADS * ts, SB_TK), F32), pltpu.VMEM((ts, W_ATT), F32)],
        compiler_params=_cparams(1),
        name="sb_sample",
    )(q16, kn16, vn16, cache_kt, cache_vt)


def _rel_bias(table_ref, h, dist):
    idx = jnp.clip(dist, -REL_CLIP, REL_CLIP) + REL_CLIP

    def body(r, b):
        return jnp.where(idx == r, table_ref[h, r], b)

    return lax.fori_loop(0, N_REL, body, jnp.zeros(dist.shape, F32))


def _row_reduce(x, combine, reduce):
    width = x.shape[1]
    acc = x[:, :128]
    for c in range(128, width, 128):
        acc = combine(acc, x[:, c:c + 128])
    return reduce(acc, axis=-1, keepdims=True)


def _toeplitz_bias(table_ref, h, rows, back, width):
    lane = _iota((8, width), 1)
    by_diag = _rel_bias(table_ref, h, back + rows - lane)
    tiled = jnp.concatenate([by_diag] * (rows // 8), axis=0)
    return pltpu.roll(tiled, width - rows, 1, stride=1, stride_axis=0)


BAND_TQ = 128
BAND_BACK = N_PREV_CHUNKS * CHUNK
BAND_WIN = BAND_BACK + BAND_TQ


def _band_prompt_kernel(table_ref, q_ref, k0_ref, k1_ref, k2_ref, v0_ref, v1_ref, v2_ref,
                        o_ref, bias_ref, *, tq):
    i = pl.program_id(0)
    n_kb = 3
    rows = _iota((BAND_TQ, BAND_WIN), 0)
    cols = _iota((BAND_TQ, BAND_WIN), 1)

    @pl.when(i == 0)
    def _():
        q_chunk = rows // CHUNK + N_PREV_CHUNKS
        k_chunk = cols // CHUNK
        in_band = jnp.logical_and(k_chunk >= q_chunk - N_PREV_CHUNKS, k_chunk <= q_chunk)
        for h in range(N_HEADS):
            bias = _toeplitz_bias(table_ref, h, BAND_TQ, BAND_BACK, BAND_WIN + BAND_TQ)
            bias_ref[h] = jnp.where(in_band, bias[:, :BAND_WIN], NEG_BIG)

    q = q_ref[...]
    head_lane = _head_masks(W_ATT)
    k_t = jnp.concatenate([k0_ref[...], k1_ref[...], k2_ref[...]], axis=1)
    vcat = jnp.concatenate([v0_ref[...], v1_ref[...], v2_ref[...]], axis=0)
    parts = [p * BAND_TQ for p in range(tq // BAND_TQ)]
    starts = [r0 + (n_kb - 1) * tq - BAND_BACK for r0 in parts]
    pairs = [(p, h) for p in range(len(parts)) for h in range(N_HEADS)]
    scores = []
    for p, h in pairs:
        qp = q[parts[p]:parts[p] + BAND_TQ]
        qh = jnp.where(head_lane[h], qp, jnp.zeros_like(qp))
        scores.append(_dot(qh, k_t[:, starts[p]:starts[p] + BAND_WIN]))
    weights = []
    inv_den = []
    for (p, h), s in zip(pairs, scores):
        exists = (i - (n_kb - 1)) * tq + starts[p] + cols >= 0
        s = jnp.where(exists, s + bias_ref[h], NEG_BIG)
        e = jnp.exp(s - _row_reduce(s, jnp.maximum, jnp.max))
        weights.append(e.astype(BF16))
        inv_den.append(1.0 / _row_reduce(e, jnp.add, jnp.sum))
    for p in range(len(parts)):
        out = jnp.zeros((BAND_TQ, W_ATT), F32)
        for h in range(N_HEADS):
            n = p * N_HEADS + h
            pv = _dot(weights[n], vcat[starts[p]:starts[p] + BAND_WIN]) * inv_den[n]
            out = jnp.where(head_lane[h], pv, out)
        o_ref[parts[p]:parts[p] + BAND_TQ, :] = out.astype(o_ref.dtype)


def _band_prompt(q16, kt16, v16, table, *, tq):
    t = q16.shape[0]
    assert t % tq == 0 and tq % BAND_TQ == 0 and 2 * tq >= BAND_BACK
    blk = lambda back: pl.BlockSpec((tq, W_ATT), lambda i: (jnp.maximum(i - back, 0), 0))
    blk_t = lambda back: pl.BlockSpec((W_ATT, tq), lambda i: (0, jnp.maximum(i - back, 0)))
    return pl.pallas_call(
        functools.partial(_band_prompt_kernel, tq=tq),
        grid=(t // tq,),
        in_specs=[pl.BlockSpec(memory_space=pltpu.SMEM),
                  blk(0), blk_t(2), blk_t(1), blk_t(0), blk(2), blk(1), blk(0)],
        out_specs=blk(0),
        out_shape=jax.ShapeDtypeStruct((t, W_ATT), BF16),
        scratch_shapes=[pltpu.VMEM((N_HEADS, BAND_TQ, BAND_WIN), F32)],
        compiler_params=_cparams(1),
        name="band_prompt",
    )(table, q16, kt16, kt16, kt16, v16, v16, v16)


def _band_sample_kernel(table_ref, q_ref, kn_ref, vn_ref, ckt_ref, cvt_ref, o_ref, bias_ref, *, ts, past):
    lb = ckt_ref.shape[3]

    @pl.when(pl.program_id(0) == 0)
    def _():
        for h in range(N_HEADS):
            bias_ref[h] = _toeplitz_bias(table_ref, h, ts, lb, bias_ref.shape[2])

    q = q_ref[...]
    head_lane = _head_masks(W_ATT)
    kc_t = ckt_ref[0, 0].astype(BF16)
    vc_t = cvt_ref[0, 0].astype(BF16)
    kn = kn_ref[...]
    vn = vn_ref[...]
    q_pos_c = past + _iota((ts, lb), 0)
    k_pos_c = past - lb + _iota((ts, lb), 1)
    q_pos_n = past + _iota((ts, ts), 0)
    k_pos_n = past + _iota((ts, ts), 1)

    def allowed(q_pos, k_pos):
        qc, kc_ = q_pos // CHUNK, k_pos // CHUNK
        return jnp.logical_and(kc_ >= qc - N_PREV_CHUNKS, kc_ <= qc)

    ok_c = allowed(q_pos_c, k_pos_c)
    ok_n = allowed(q_pos_n, k_pos_n)
    out = jnp.zeros((ts, W_ATT), F32)
    for h in range(N_HEADS):
        qh = jnp.where(head_lane[h], q, jnp.zeros_like(q))
        bias = bias_ref[h]
        s_c = jnp.where(ok_c, _dot(qh, kc_t) + bias[:, :lb], NEG_BIG)
        s_n = jnp.where(ok_n, _dot_nt(qh, kn) + bias[:, lb:lb + ts], NEG_BIG)
        m = jnp.maximum(jnp.max(s_c, axis=-1, keepdims=True), jnp.max(s_n, axis=-1, keepdims=True))
        e_c = jnp.exp(s_c - m)
        e_n = jnp.exp(s_n - m)
        den = jnp.sum(e_c, axis=-1, keepdims=True) + jnp.sum(e_n, axis=-1, keepdims=True)
        pv = (_dot_nt(e_c.astype(BF16), vc_t) + _dot(e_n.astype(BF16), vn)) / den
        out = jnp.where(head_lane[h], pv, out)
    o_ref[...] = out.astype(o_ref.dtype)


def _band_sample(q16, kn16, vn16, cache_kt, cache_vt, table, *, layer, nb, ts, past):
    lb = cache_kt.shape[3]
    row = pl.BlockSpec((ts, W_ATT), lambda b: (b, 0))
    cache = pl.BlockSpec((1, 1, W_ATT, lb), lambda b: (layer, b, 0, 0))
    return pl.pallas_call(
        functools.partial(_band_sample_kernel, ts=ts, past=past),
        grid=(nb,),
        in_specs=[pl.BlockSpec(memory_space=pltpu.SMEM), row, row, row, cache, cache],
        out_specs=row,
        out_shape=jax.ShapeDtypeStruct((nb * ts, W_ATT), BF16),
        scratch_shapes=[pltpu.VMEM((N_HEADS, ts, pl.cdiv(lb + 2 * ts, 128) * 128), F32)],
        compiler_params=_cparams(1),
        name="band_sample",
    )(table, q16, kn16, vn16, cache_kt, cache_vt)


def _hgrn_kernel(q_ref, k_ref, lf_ref, v_ref, s0_ref, o_ref, sout_ref,
                 st_ref, g_ref, dec_ref, qt_ref, kd_ref, *, tr):
    r = pl.program_id(1)

    @pl.when(r == 0)
    def _():
        for h in range(N_HEADS):
            st_ref[h] = s0_ref[0, h].T

    sub = HGRN_SUB
    half = sub // 2
    heads = [slice(h * DK_C, (h + 1) * DK_C) for h in range(N_HEADS)]

    same_sub = (_iota((tr, tr), 0) // sub) == (_iota((tr, tr), 1) // sub)
    lower = _iota((tr, tr), 0) >= _iota((tr, tr), 1)
    tri = jnp.where(jnp.logical_and(same_sub, lower), 1.0, 0.0).astype(BF16)
    f1, f2, f3 = _split3(lf_ref[...])
    g_all = _dot(tri, f1) + _dot(tri, f2) + _dot(tri, f3)
    g_3d = g_all.reshape(tr // sub, sub, W_C)
    g_end = jnp.broadcast_to(g_3d[:, sub - 1:sub, :], g_3d.shape).reshape(tr, W_C)
    g_ref[...] = g_all
    dec_ref[...] = jnp.exp(g_end)
    qt_ref[...] = (q_ref[...] * jnp.exp(g_all)).astype(BF16)
    kd_ref[...] = (k_ref[...] * jnp.exp(g_end - g_all)).astype(BF16)
    row_id = _iota((half, 1), 0)

    def block(n, carry):
        r0 = pl.multiple_of(n * sub, sub)
        rows = pl.ds(r0, sub)
        v = v_ref[rows, :]
        v16 = v.astype(BF16)
        kd = kd_ref[rows, :]
        qt = qt_ref[rows, :]
        upd = [_dot_tn(v16[:, hs], kd[:, hs]) for hs in heads]
        o_top = []
        o_bot = []
        for h, hs in enumerate(heads):
            o_h = _dot_nt(qt[:, hs], st_ref[h].astype(BF16))
            o_top.append(o_h[:half])
            o_bot.append(o_h[half:])
        g = g_ref[rows, :]
        q = q_ref[rows, :]
        k = k_ref[rows, :]
        g_t, g_b = g[:half], g[half:]
        q_t, q_b = q[:half], q[half:]
        for j in range(sub):
            gj, kj, vj = g[j:j + 1, :], k[j:j + 1, :], v[j:j + 1, :]
            p_b = q_b * (kj * jnp.exp(g_b - gj))
            if j < half:
                p_t = q_t * (kj * jnp.exp(g_t - gj))
            for h, hs in enumerate(heads):
                a_b = jnp.sum(p_b[:, hs], axis=-1, keepdims=True)
                if j < half:
                    a_t = jnp.sum(p_t[:, hs], axis=-1, keepdims=True)
                    o_top[h] = o_top[h] + jnp.where(row_id >= j, a_t, 0.0) * vj[:, hs]
                else:
                    a_b = jnp.where(row_id >= j - half, a_b, 0.0)
                o_bot[h] = o_bot[h] + a_b * vj[:, hs]
        o_ref[rows, :] = jnp.concatenate(
            [jnp.concatenate(o_top, axis=-1), jnp.concatenate(o_bot, axis=-1)], axis=0)
        dec = dec_ref[pl.ds(r0, 1), :]
        for h, hs in enumerate(heads):
            st_ref[h] = st_ref[h] * dec[:, hs] + upd[h]
        return carry

    lax.fori_loop(0, tr // sub, block, 0, unroll=2)

    @pl.when(r == pl.num_programs(1) - 1)
    def _():
        for h in range(N_HEADS):
            sout_ref[0, h] = st_ref[h].T


def _hgrn(qc, kc, lf, ic, s0, *, nb, t, tr):
    assert t % tr == 0 and tr % HGRN_SUB == 0
    nr = t // tr
    row = pl.BlockSpec((tr, W_C), lambda b, r: (b * nr + r, 0))
    state = pl.BlockSpec((1, N_HEADS, DK_C, DK_C), lambda b, r: (b, 0, 0, 0))
    return pl.pallas_call(
        functools.partial(_hgrn_kernel, tr=tr),
        grid=(nb, nr),
        in_specs=[row, row, row, row, state],
        out_specs=[row, state],
        out_shape=[jax.ShapeDtypeStruct((nb * t, W_C), F32),
                   jax.ShapeDtypeStruct((nb, N_HEADS, DK_C, DK_C), F32)],
        scratch_shapes=[pltpu.VMEM((N_HEADS, DK_C, DK_C), F32),
                        pltpu.VMEM((tr, W_C), F32), pltpu.VMEM((tr, W_C), F32),
                        pltpu.VMEM((tr, W_C), BF16), pltpu.VMEM((tr, W_C), BF16)],
        compiler_params=_cparams(2),
        name="hgrn",
    )(qc, kc, lf, ic, s0)


def _merge_kernel(x_ref, oa_ref, ob_ref, oc_ref, gc_ref, on_ref, wo_ref, nf_ref, wu_ref, wd_ref,
                  y_ref, *, f_chunk):
    oc = oc_ref[...]
    gc = gc_ref[...]
    parts = []
    for h in range(N_HEADS):
        cs = slice(h * DK_C, (h + 1) * DK_C)
        och = oc[:, cs]
        ms = jnp.mean(och * och, axis=-1, keepdims=True)
        parts.append(och * lax.rsqrt(ms + EPS) * on_ref[...])
    ocn = jnp.concatenate(parts, axis=-1) * (gc * (1.0 / (1.0 + jnp.exp(-gc))))
    mixed = jnp.concatenate([oa_ref[...], ob_ref[...], ocn.astype(BF16)], axis=-1)
    h_res = x_ref[...] + _dot(mixed, wo_ref[...])
    ms = jnp.mean(h_res * h_res, axis=-1, keepdims=True)
    hn = (h_res * lax.rsqrt(ms + EPS) * nf_ref[...]).astype(BF16)
    y = h_res
    d_ff = wu_ref.shape[1]
    for c in range(0, d_ff, f_chunk):
        u = jnp.maximum(_dot(hn, wu_ref[:, c:c + f_chunk]), 0.0)
        y = y + _dot((u * u).astype(BF16), wd_ref[c:c + f_chunk, :])
    y_ref[...] = y


def _merge(x2d, oa, ob, oc, gc, onorm, wo16, nf, wu16, wd16, *, tm):
    m, d = x2d.shape
    assert m % tm == 0
    row = lambda w: pl.BlockSpec((tm, w), lambda i: (i, 0))
    full = _resident
    return pl.pallas_call(
        functools.partial(_merge_kernel, f_chunk=1024),
        grid=(m // tm,),
        in_specs=[row(d), row(W_ATT), row(W_ATT), row(W_C), row(W_C),
                  full(onorm), full(wo16), full(nf), full(wu16), full(wd16)],
        out_specs=row(d),
        out_shape=jax.ShapeDtypeStruct((m, d), F32),
        compiler_params=_cparams(1),
        name="merge_ffn",
    )(x2d, oa, ob, oc, gc, onorm, wo16, nf, wu16, wd16)


def kernel(x_prompt, x_sample, cache_a_k, cache_a_v, cache_b_k, cache_b_v, state_c, norm_mix, w_in, qnorm_a, knorm_a, qnorm_b, knorm_b, rel_bias_b, lower_bounds, onorm_c, w_o, norm_ffn, w_up, w_down):
    depth = w_in.shape[0]
    bp, tp, d = x_prompt.shape
    bs, ts, _ = x_sample.shape
    past = cache_a_k.shape[2]
    band_rows_p = min(N_PREV_CHUNKS * CHUNK, tp)
    assert bp == 1

    xp = x_prompt.reshape(bp * tp, d)
    xs = x_sample.reshape(bs * ts, d)
    time_minor = lambda c: jnp.transpose(c, (0, 1, 3, 4, 2)).reshape(depth, bs, W_ATT, c.shape[2])
    cak, cav, cbk, cbv = (time_minor(c) for c in (cache_a_k, cache_a_v, cache_b_k, cache_b_v))
    heads_of = lambda a_t, n: jnp.transpose(
        a_t.reshape(N_HEADS, D_HEAD, n, a_t.shape[1] // n), (2, 3, 0, 1))
    lbounds = lower_bounds.astype(F32)
    zero_state = jnp.zeros((bp, N_HEADS, DK_C, DK_C), F32)

    tm_p = min(512, tp)
    tq_band = min(256, tp)
    tr_p = min(256, tp)

    outs_p = [[] for _ in range(5)]
    outs_s = [[] for _ in range(5)]
    for l in range(depth):
        w16 = w_in[l].astype(BF16)
        wo16 = w_o[l].astype(BF16)
        wu16 = w_up[l].astype(BF16)
        wd16 = w_down[l].astype(BF16)
        nm = norm_mix[l].reshape(1, d)
        nf = norm_ffn[l].reshape(1, d)
        tile_h = lambda g: jnp.tile(g.reshape(1, D_HEAD), (1, N_HEADS))
        qna, kna, qnb, knb = (tile_h(g[l]) for g in (qnorm_a, knorm_a, qnorm_b, knorm_b))
        onorm = onorm_c[l].reshape(1, DK_C)
        table = rel_bias_b[l].astype(F32)

        (qa, ka_t, va_t, ka16_t, va16, qb, kb_t, vb_t, kb16_t, vb16, qc, kc, lf, ic, gc) = _proj(
            xp, nm, w16, qna, kna, qnb, knb, lbounds, layer=l, tm=tm_p, time_minor=True)
        oa = _sb_prompt(qa, ka16_t, va16, tq=min(SB_TQ, tp))
        ob = _band_prompt(qb, kb16_t, vb16, table, tq=tq_band)
        oc, s_p = _hgrn(qc, kc, lf, ic, zero_state, nb=bp, t=tp, tr=tr_p)
        xp = _merge(xp, oa, ob, oc, gc, onorm, wo16, nf, wu16, wd16, tm=tm_p)
        outs_p[0].append(heads_of(ka_t, bp))
        outs_p[1].append(heads_of(va_t, bp))
        outs_p[2].append(heads_of(kb_t, bp)[:, tp - band_rows_p:])
        outs_p[3].append(heads_of(vb_t, bp)[:, tp - band_rows_p:])
        outs_p[4].append(s_p)

        head = lambda a, n: a.reshape(n, -1, N_HEADS, D_HEAD)
        (qa, ka, va, ka16, va16, qb, kb, vb, kb16, vb16, qc, kc, lf, ic, gc) = _proj(
            xs, nm, w16, qna, kna, qnb, knb, lbounds, layer=l, tm=bs * ts, time_minor=False)
        oa = _sb_sample(qa, ka16, va16, cak, cav, layer=l, nb=bs, ts=ts)
        ob = _band_sample(qb, kb16, vb16, cbk, cbv, table, layer=l, nb=bs, ts=ts, past=past)
        oc, s_s = _hgrn(qc, kc, lf, ic, state_c[l].astype(F32), nb=bs, t=ts, tr=ts)
        xs = _merge(xs, oa, ob, oc, gc, onorm, wo16, nf, wu16, wd16, tm=bs * ts)
        outs_s[0].append(head(ka, bs))
        outs_s[1].append(head(va, bs))
        outs_s[2].append(head(kb, bs))
        outs_s[3].append(head(vb, bs))
        outs_s[4].append(s_s)

    stack = lambda xs_: jnp.stack(xs_)
    return (xp.reshape(bp, tp, d), xs.reshape(bs, ts, d),
            stack(outs_p[0]), stack(outs_p[1]), stack(outs_p[2]), stack(outs_p[3]), stack(outs_p[4]),
            stack(outs_s[0]), stack(outs_s[1]), stack(outs_s[2]), stack(outs_s[3]), stack(outs_s[4]))
```

```python
import functools
import math

import jax
import jax.numpy as jnp
from jax import lax
from jax.experimental import pallas as pl
from jax.experimental.pallas import tpu as pltpu

F32 = jnp.float32
BF16 = jnp.bfloat16

D_HEAD = 64
N_HEADS = 4
W_ATT = N_HEADS * D_HEAD
DK_C = 128
W_C = N_HEADS * DK_C
CHUNK = 64
N_PREV_CHUNKS = 8
REL_CLIP = 128
N_REL = 2 * REL_CLIP + 1
EPS = 1e-6
NEG_BIG = -1e30
LB_FLOOR = 1e-30
SB_UNDERFLOW = 88.0
PROJ_PART = 256
SB_TK = 128
SB_TQ = 256
HGRN_DIAG = 8
VMEM_LIMIT = 56 * 1024 * 1024


def _cparams(n_axes):
    return pltpu.CompilerParams(dimension_semantics=("arbitrary",) * n_axes,
                                vmem_limit_bytes=VMEM_LIMIT)


def _resident(a):
    zeros = (0,) * a.ndim
    return pl.BlockSpec(a.shape, lambda *_: zeros, pipeline_mode=pl.Buffered(1))


def _split3(x):
    h1 = x.astype(BF16)
    r1 = x - h1.astype(F32)
    h2 = r1.astype(BF16)
    h3 = (r1 - h2.astype(F32)).astype(BF16)
    return h1, h2, h3


def _dot(a, b):
    return jnp.dot(a, b, preferred_element_type=F32)


def _dot_nt(a, b):
    return lax.dot_general(a, b, (((1,), (1,)), ((), ())), preferred_element_type=F32)


def _dot_tn(a, b):
    return lax.dot_general(a, b, (((0,), (0,)), ((), ())), preferred_element_type=F32)


def _iota(shape, dim):
    return lax.broadcasted_iota(jnp.int32, shape, dim)


def _proj_kernel(x_ref, nm_ref, w_ref, qna_ref, kna_ref, qnb_ref, knb_ref, lb_ref, *rest,
                 layer, time_minor):
    (qa_ref, ka_ref, va_ref, ka16_ref, va16_ref, qb_ref, kb_ref, vb_ref, kb16_ref, vb16_ref,
     qc_ref, kc_ref, lf_ref, ic_ref, gc_ref) = rest[-15:]
    if time_minor:
        ka_ref, va_ref = ka_ref.at[0], va_ref.at[0]
    x = x_ref[...]
    ms = jnp.mean(x * x, axis=-1, keepdims=True)
    xn = (x * lax.rsqrt(ms + EPS) * nm_ref[...]).astype(BF16)

    tm = x.shape[0]
    part = PROJ_PART if tm % PROJ_PART == 0 else tm
    projected = [_dot(xn[r0:r0 + part], w_ref[...]) for r0 in range(0, tm, part)]

    same_head = (_iota((W_ATT, W_ATT), 0) // D_HEAD) == (_iota((W_ATT, W_ATT), 1) // D_HEAD)
    seg_mean = jnp.where(same_head, 1.0 / D_HEAD, 0.0).astype(BF16)
    inv_sqrt_d = 1.0 / math.sqrt(D_HEAD)

    lbr = lb_ref[...]
    e = jnp.exp(lbr - jnp.max(lbr, axis=0, keepdims=True))
    sm = e / jnp.sum(e, axis=0, keepdims=True)
    lb = jnp.sum(sm[0:layer + 1], axis=0, keepdims=True) - sm[0:1]
    log_lb = jnp.log(jnp.maximum(lb, LB_FLOOR))
    log_1m_lb = jnp.log1p(-lb)

    def head_norm(p, g_ref):
        m = _dot((p * p).astype(BF16), seg_mean)
        return p * lax.rsqrt(m + EPS) * g_ref[...]

    for n, p in enumerate(projected):
        rows = slice(n * part, (n + 1) * part)

        def put_k(k, k_ref, k16_ref):
            if time_minor:
                k_ref[:, rows] = k.T
                k16_ref[:, rows] = k.T.astype(BF16)
            else:
                k_ref[rows, :] = k
                k16_ref[rows, :] = k.astype(BF16)

        def put_v(v, v_ref, v16_ref):
            if time_minor:
                v_ref[:, rows] = v.T
            else:
                v_ref[rows, :] = v
            v16_ref[rows, :] = v.astype(BF16)

        o = 0
        qa_ref[rows, :] = (head_norm(p[:, o:o + W_ATT], qna_ref) * inv_sqrt_d).astype(BF16)
        o += W_ATT
        put_k(head_norm(p[:, o:o + W_ATT], kna_ref), ka_ref, ka16_ref)
        o += W_ATT
        put_v(p[:, o:o + W_ATT], va_ref, va16_ref)
        o += W_ATT
        qb_ref[rows, :] = (head_norm(p[:, o:o + W_ATT], qnb_ref) * inv_sqrt_d).astype(BF16)
        o += W_ATT
        put_k(head_norm(p[:, o:o + W_ATT], knb_ref), kb_ref, kb16_ref)
        o += W_ATT
        put_v(p[:, o:o + W_ATT], vb_ref, vb16_ref)
        o += W_ATT
        qc_ref[rows, :] = p[:, o:o + W_C] * (DK_C ** -0.5)
        o += W_C
        f_raw = p[:, o:o + W_C]
        o += W_C
        log_sig = jnp.minimum(f_raw, 0.0) - jnp.log1p(jnp.exp(-jnp.abs(f_raw)))
        b = log_1m_lb + log_sig
        lf_ref[rows, :] = jnp.maximum(log_lb, b) + jnp.log1p(jnp.exp(-jnp.abs(log_lb - b)))
        kc_ref[rows, :] = (1.0 - lb) * (1.0 / (1.0 + jnp.exp(f_raw)))
        ic_ref[rows, :] = p[:, o:o + W_C]
        o += W_C
        gc_ref[rows, :] = p[:, o:o + W_C]


def _proj(x2d, nm, w16, qna, kna, qnb, knb, lower_bounds, *, layer, tm, time_minor, carried=()):
    m, d = x2d.shape
    depth = lower_bounds.shape[0]
    assert m % tm == 0 and len(carried) == (2 if time_minor and layer > 0 else 0)
    full = _resident
    row = lambda w, dt: (pl.BlockSpec((tm, w), lambda i: (i, 0)), jax.ShapeDtypeStruct((m, w), dt))
    col = lambda w, dt: (pl.BlockSpec((w, tm), lambda i: (0, i)), jax.ShapeDtypeStruct((w, m), dt))
    layered = lambda w, dt: (pl.BlockSpec((1, w, tm), lambda i: (layer, 0, i)),
                             jax.ShapeDtypeStruct((depth, w, m), dt))
    kv = col if time_minor else row
    kv_a = layered if time_minor else row
    att_a = [row(W_ATT, BF16), kv_a(W_ATT, F32), kv_a(W_ATT, F32), kv(W_ATT, BF16), row(W_ATT, BF16)]
    att_b = [row(W_ATT, BF16), kv(W_ATT, F32), kv(W_ATT, F32), kv(W_ATT, BF16), row(W_ATT, BF16)]
    outs = att_a + att_b + [row(W_C, F32)] * 5
    n_in = 8
    return pl.pallas_call(
        functools.partial(_proj_kernel, layer=layer, time_minor=time_minor),
        grid=(m // tm,),
        in_specs=[row(d, F32)[0], full(nm), full(w16), full(qna), full(kna), full(qnb), full(knb),
                  full(lower_bounds)] + [pl.BlockSpec(memory_space=pl.ANY)] * len(carried),
        out_specs=[spec for spec, _ in outs],
        out_shape=[shape for _, shape in outs],
        input_output_aliases={n_in + n: 1 + n for n in range(len(carried))},
        compiler_params=_cparams(1),
        name="proj",
    )(x2d, nm, w16, qna, kna, qnb, knb, lower_bounds, *carried)


def _head_masks(width):
    lane_head = _iota((1, width), 1) // D_HEAD
    return [lane_head == h for h in range(N_HEADS)]


def _stack_heads(x, head_lane):
    return jnp.concatenate([jnp.where(m, x, jnp.zeros_like(x)) for m in head_lane], axis=0)


def _stack_heads_t(x_t, n):
    row_head = _iota((W_ATT, 1), 0) // D_HEAD
    return jnp.concatenate(
        [jnp.where(row_head == h, x_t, jnp.zeros_like(x_t)) for h in range(N_HEADS)], axis=1)


def _sb_block(z, weigh_values, mask, tri, carry_ref, acc_ref):
    tq = acc_ref.shape[0]
    tk = z.shape[1]
    t = jnp.log(1.0 + jnp.exp(-jnp.abs(z)))
    log_1m = jnp.minimum(-z, 0.0) - t
    log_beta = jnp.minimum(z, 0.0) - t
    if mask is not None:
        log_1m = jnp.where(mask, log_1m, 0.0)
    hi = log_1m.astype(BF16)
    lo = (log_1m - hi.astype(F32)).astype(BF16)
    later = _dot(jnp.concatenate([hi, lo], axis=1), tri)
    carry = carry_ref[...]
    w = jnp.exp(log_beta + later + carry[:, :tk])
    if mask is not None:
        w = jnp.where(mask, w, 0.0)
    w = w.astype(BF16)
    w_heads = jnp.concatenate([w[h * tq:(h + 1) * tq] for h in range(N_HEADS)], axis=1)
    acc_ref[...] += weigh_values(w_heads)
    carry_ref[...] = carry + jnp.sum(log_1m, axis=-1, keepdims=True)


def _sb_live(carry_ref):
    return (jnp.max(carry_ref[...]) > -SB_UNDERFLOW).astype(jnp.int32)


def _strict_upper(n):
    return jnp.where(_iota((2 * n, n), 0) % n > _iota((2 * n, n), 1), 1.0, 0.0).astype(BF16)


def _sb_prompt_kernel(q_ref, kt_ref, v_ref, o_ref, carry_ref, acc_ref, *, tq):
    i = pl.program_id(0)
    head_lane = _head_masks(W_ATT)
    q_stack = _stack_heads(q_ref[...], head_lane)
    tri = _strict_upper(SB_TK)
    carry_ref[...] = jnp.zeros_like(carry_ref)
    acc_ref[...] = jnp.zeros_like(acc_ref)

    def add_block(j, mask):
        r0 = pl.multiple_of(j * SB_TK, SB_TK)
        z = _dot(q_stack, kt_ref[:, pl.ds(r0, SB_TK)])
        v_stack = _stack_heads(v_ref[pl.ds(r0, SB_TK), :], head_lane)
        _sb_block(z, lambda w: _dot(w, v_stack), mask, tri, carry_ref, acc_ref)

    shape = (N_HEADS * tq, SB_TK)
    diag_blocks = tq // SB_TK
    for d in reversed(range(diag_blocks)):
        add_block(i * diag_blocks + d, d * SB_TK + _iota(shape, 1) < _iota(shape, 0) % tq)

    def cond(c):
        j, live = c
        return jnp.logical_and(j >= 0, live > 0)

    def body(c):
        j, _ = c
        add_block(j, None)
        return j - 1, _sb_live(carry_ref)

    lax.while_loop(cond, body, (i * diag_blocks - 1, _sb_live(carry_ref)))
    o_ref[...] = acc_ref[...].astype(o_ref.dtype)


def _sb_prompt(q16, kt16, v16, *, tq):
    t = q16.shape[0]
    assert t % tq == 0 and tq % SB_TK == 0
    return pl.pallas_call(
        functools.partial(_sb_prompt_kernel, tq=tq),
        grid=(t // tq,),
        in_specs=[pl.BlockSpec((tq, W_ATT), lambda i: (i, 0)),
                  _resident(kt16), _resident(v16)],
        out_specs=pl.BlockSpec((tq, W_ATT), lambda i: (i, 0)),
        out_shape=jax.ShapeDtypeStruct((t, W_ATT), BF16),
        scratch_shapes=[pltpu.VMEM((N_HEADS * tq, SB_TK), F32), pltpu.VMEM((tq, W_ATT), F32)],
        compiler_params=_cparams(1),
        name="sb_prompt",
    )(q16, kt16, v16)


def _sb_sample_kernel(q_ref, kn_ref, vn_ref, ckt_ref, cvt_ref, o_ref, carry_ref, acc_ref, *, ts, past):
    head_lane = _head_masks(W_ATT)
    q_stack = _stack_heads(q_ref[...], head_lane)
    carry_ref[...] = jnp.zeros_like(carry_ref)
    acc_ref[...] = jnp.zeros_like(acc_ref)
    shape = (N_HEADS * ts, ts)
    mask = _iota(shape, 1) < _iota(shape, 0) % ts
    v_new = _stack_heads(vn_ref[...], head_lane)
    _sb_block(_dot_nt(q_stack, kn_ref[...]), lambda w: _dot(w, v_new), mask, _strict_upper(ts),
              carry_ref, acc_ref)
    tri = _strict_upper(SB_TK)

    def cond(c):
        j, live = c
        return jnp.logical_and(j >= 0, live > 0)

    def body(c):
        j, _ = c
        r0 = pl.multiple_of(j * SB_TK, SB_TK)
        z = _dot(q_stack, ckt_ref[0, 0, :, pl.ds(r0, SB_TK)].astype(BF16))
        vt_stack = _stack_heads_t(cvt_ref[0, 0, :, pl.ds(r0, SB_TK)].astype(BF16), SB_TK)
        _sb_block(z, lambda w: _dot_nt(w, vt_stack), None, tri, carry_ref, acc_ref)
        return j - 1, _sb_live(carry_ref)

    lax.while_loop(cond, body, (past // SB_TK - 1, _sb_live(carry_ref)))
    o_ref[...] = acc_ref[...].astype(o_ref.dtype)


def _sb_sample(q16, kn16, vn16, cache_kt, cache_vt, *, layer, nb, ts):
    past = cache_kt.shape[3]
    assert past % SB_TK == 0
    row = pl.BlockSpec((ts, W_ATT), lambda b: (b, 0))
    cache = pl.BlockSpec((1, 1, W_ATT, past), lambda b: (layer, b, 0, 0))
    return pl.pallas_call(
        functools.partial(_sb_sample_kernel, ts=ts, past=past),
        grid=(nb,),
        in_specs=[row, row, row, cache, cache],
        out_specs=row,
        out_shape=jax.ShapeDtypeStruct((nb * ts, W_ATT), BF16),
        scratch_shapes=[pltpu.VMEM((N_HEADS * ts, SB_TK), F32), pltpu.VMEM((ts, W_ATT), F32)],
        compiler_params=_cparams(1),
        name="sb_sample",
    )(q16, kn16, vn16, cache_kt, cache_vt)


def _rel_bias(table_ref, h, dist):
    idx = jnp.clip(dist, -REL_CLIP, REL_CLIP) + REL_CLIP

    def body(r, b):
        return jnp.where(idx == r, table_ref[h, r], b)

    return lax.fori_loop(0, N_REL, body, jnp.zeros(dist.shape, F32))


def _row_reduce(x, combine, reduce):
    width = x.shape[1]
    acc = x[:, :128]
    for c in range(128, width, 128):
        acc = combine(acc, x[:, c:c + 128])
    return reduce(acc, axis=-1, keepdims=True)


def _toeplitz_bias(table_ref, h, rows, back, width):
    lane = _iota((8, width), 1)
    by_diag = _rel_bias(table_ref, h, back + rows - lane)
    tiled = jnp.concatenate([by_diag] * (rows // 8), axis=0)
    return pltpu.roll(tiled, width - rows, 1, stride=1, stride_axis=0)


BAND_TQ = 128
BAND_BACK = N_PREV_CHUNKS * CHUNK
BAND_WIN = BAND_BACK + BAND_TQ


def _band_prompt_kernel(table_ref, q_ref, k0_ref, k1_ref, k2_ref, v0_ref, v1_ref, v2_ref,
                        o_ref, bias_ref, *, tq):
    i = pl.program_id(0)
    n_kb = 3
    rows = _iota((BAND_TQ, BAND_WIN), 0)
    cols = _iota((BAND_TQ, BAND_WIN), 1)

    @pl.when(i == 0)
    def _():
        q_chunk = rows // CHUNK + N_PREV_CHUNKS
        k_chunk = cols // CHUNK
        in_band = jnp.logical_and(k_chunk >= q_chunk - N_PREV_CHUNKS, k_chunk <= q_chunk)
        for h in range(N_HEADS):
            bias = _toeplitz_bias(table_ref, h, BAND_TQ, BAND_BACK, BAND_WIN + BAND_TQ)
            bias_ref[h] = jnp.where(in_band, bias[:, :BAND_WIN], NEG_BIG)

    q = q_ref[...]
    head_lane = _head_masks(W_ATT)
    k_t = jnp.concatenate([k0_ref[...], k1_ref[...], k2_ref[...]], axis=1)
    vcat = jnp.concatenate([v0_ref[...], v1_ref[...], v2_ref[...]], axis=0)
    parts = [p * BAND_TQ for p in range(tq // BAND_TQ)]
    starts = [r0 + (n_kb - 1) * tq - BAND_BACK for r0 in parts]
    pairs = [(p, h) for p in range(len(parts)) for h in range(N_HEADS)]
    scores = []
    for p, h in pairs:
        qp = q[parts[p]:parts[p] + BAND_TQ]
        qh = jnp.where(head_lane[h], qp, jnp.zeros_like(qp))
        scores.append(_dot(qh, k_t[:, starts[p]:starts[p] + BAND_WIN]))
    weights = []
    inv_den = []
    for (p, h), s in zip(pairs, scores):
        exists = (i - (n_kb - 1)) * tq + starts[p] + cols >= 0
        s = jnp.where(exists, s + bias_ref[h], NEG_BIG)
        e = jnp.exp(s - _row_reduce(s, jnp.maximum, jnp.max))
        weights.append(e.astype(BF16))
        inv_den.append(1.0 / _row_reduce(e, jnp.add, jnp.sum))
    for p in range(len(parts)):
        out = jnp.zeros((BAND_TQ, W_ATT), F32)
        for h in range(N_HEADS):
            n = p * N_HEADS + h
            pv = _dot(weights[n], vcat[starts[p]:starts[p] + BAND_WIN]) * inv_den[n]
            out = jnp.where(head_lane[h], pv, out)
        o_ref[parts[p]:parts[p] + BAND_TQ, :] = out.astype(o_ref.dtype)


def _band_prompt(q16, kt16, v16, table, *, tq):
    t = q16.shape[0]
    assert t % tq == 0 and tq % BAND_TQ == 0 and 2 * tq >= BAND_BACK
    blk = lambda back: pl.BlockSpec((tq, W_ATT), lambda i: (jnp.maximum(i - back, 0), 0))
    blk_t = lambda back: pl.BlockSpec((W_ATT, tq), lambda i: (0, jnp.maximum(i - back, 0)))
    return pl.pallas_call(
        functools.partial(_band_prompt_kernel, tq=tq),
        grid=(t // tq,),
        in_specs=[pl.BlockSpec(memory_space=pltpu.SMEM),
                  blk(0), blk_t(2), blk_t(1), blk_t(0), blk(2), blk(1), blk(0)],
        out_specs=blk(0),
        out_shape=jax.ShapeDtypeStruct((t, W_ATT), BF16),
        scratch_shapes=[pltpu.VMEM((N_HEADS, BAND_TQ, BAND_WIN), F32)],
        compiler_params=_cparams(1),
        name="band_prompt",
    )(table, q16, kt16, kt16, kt16, v16, v16, v16)


def _band_sample_kernel(table_ref, q_ref, kn_ref, vn_ref, ckt_ref, cvt_ref, o_ref, bias_ref, *, ts, past):
    lb = ckt_ref.shape[3]

    @pl.when(pl.program_id(0) == 0)
    def _():
        for h in range(N_HEADS):
            bias_ref[h] = _toeplitz_bias(table_ref, h, ts, lb, bias_ref.shape[2])

    q = q_ref[...]
    head_lane = _head_masks(W_ATT)
    kc_t = ckt_ref[0, 0].astype(BF16)
    vc_t = cvt_ref[0, 0].astype(BF16)
    kn = kn_ref[...]
    vn = vn_ref[...]
    q_pos_c = past + _iota((ts, lb), 0)
    k_pos_c = past - lb + _iota((ts, lb), 1)
    q_pos_n = past + _iota((ts, ts), 0)
    k_pos_n = past + _iota((ts, ts), 1)

    def allowed(q_pos, k_pos):
        qc, kc_ = q_pos // CHUNK, k_pos // CHUNK
        return jnp.logical_and(kc_ >= qc - N_PREV_CHUNKS, kc_ <= qc)

    ok_c = allowed(q_pos_c, k_pos_c)
    ok_n = allowed(q_pos_n, k_pos_n)
    out = jnp.zeros((ts, W_ATT), F32)
    for h in range(N_HEADS):
        qh = jnp.where(head_lane[h], q, jnp.zeros_like(q))
        bias = bias_ref[h]
        s_c = jnp.where(ok_c, _dot(qh, kc_t) + bias[:, :lb], NEG_BIG)
        s_n = jnp.where(ok_n, _dot_nt(qh, kn) + bias[:, lb:lb + ts], NEG_BIG)
        m = jnp.maximum(jnp.max(s_c, axis=-1, keepdims=True), jnp.max(s_n, axis=-1, keepdims=True))
        e_c = jnp.exp(s_c - m)
        e_n = jnp.exp(s_n - m)
        den = jnp.sum(e_c, axis=-1, keepdims=True) + jnp.sum(e_n, axis=-1, keepdims=True)
        pv = (_dot_nt(e_c.astype(BF16), vc_t) + _dot(e_n.astype(BF16), vn)) / den
        out = jnp.where(head_lane[h], pv, out)
    o_ref[...] = out.astype(o_ref.dtype)


def _band_sample(q16, kn16, vn16, cache_kt, cache_vt, table, *, layer, nb, ts, past):
    lb = cache_kt.shape[3]
    row = pl.BlockSpec((ts, W_ATT), lambda b: (b, 0))
    cache = pl.BlockSpec((1, 1, W_ATT, lb), lambda b: (layer, b, 0, 0))
    return pl.pallas_call(
        functools.partial(_band_sample_kernel, ts=ts, past=past),
        grid=(nb,),
        in_specs=[pl.BlockSpec(memory_space=pltpu.SMEM), row, row, row, cache, cache],
        out_specs=row,
        out_shape=jax.ShapeDtypeStruct((nb * ts, W_ATT), BF16),
        scratch_shapes=[pltpu.VMEM((N_HEADS, ts, pl.cdiv(lb + 2 * ts, 128) * 128), F32)],
        compiler_params=_cparams(1),
        name="band_sample",
    )(table, q16, kn16, vn16, cache_kt, cache_vt)


def _hgrn_kernel(q_ref, k_ref, lf_ref, v_ref, s0_ref, o_ref, sout_ref, st_ref, g_ref, *, tr):
    r = pl.program_id(1)

    @pl.when(r == 0)
    def _():
        for h in range(N_HEADS):
            st_ref[h] = s0_ref[0, h].T

    heads = [slice(h * DK_C, (h + 1) * DK_C) for h in range(N_HEADS)]
    tri = jnp.where(_iota((tr, tr), 0) >= _iota((tr, tr), 1), 1.0, 0.0).astype(BF16)
    f1, f2, f3 = _split3(lf_ref[...])
    g = (_dot(tri, f1) + _dot(tri, f2) + _dot(tri, f3)) * math.log2(math.e)
    g_ref[...] = g
    q = q_ref[...]
    k = k_ref[...]
    v16 = v_ref[...].astype(BF16)

    g_end = g[tr - 1:tr, :]
    qt = (q * jnp.exp2(g)).astype(BF16)
    kd = (k * jnp.exp2(g_end - g)).astype(BF16)
    dec = jnp.exp2(g_end)
    states = [st_ref[h] for h in range(N_HEADS)]
    o_heads = [_dot_nt(qt[:, hs], st.astype(BF16)) for hs, st in zip(heads, states)]
    grown = [_dot_tn(v16[:, hs], kd[:, hs]) for hs in heads]
    for h, hs in enumerate(heads):
        st_ref[h] = states[h] * dec[:, hs] + grown[h]
    o = jnp.concatenate(o_heads, axis=1)

    sizes = []
    h = HGRN_DIAG
    while h < tr:
        sizes.append(h)
        h *= 2
    split = lambda x, h: x.reshape(tr // (2 * h), 2 * h, W_C)
    operands = []
    for h in sizes:
        g3, q3, k3, v3 = split(g, h), split(q, h), split(k, h), split(v16, h)
        c = g3[:, h - 1:h, :]
        qe = (q3[:, h:] * jnp.exp2(g3[:, h:] - c)).reshape(tr // 2, W_C).astype(BF16)
        ke = (k3[:, :h] * jnp.exp2(c - g3[:, :h])).reshape(tr // 2, W_C).astype(BF16)
        operands.append((qe, ke, v3[:, :h].reshape(tr // 2, W_C)))
    pair = [[_dot_nt(qe[:, hs], ke[:, hs]) for hs in heads] for qe, ke, _ in operands]
    span_of_row = _iota((tr // 2, tr // 2), 0)
    span_of_col = _iota((tr // 2, tr // 2), 1)
    for n, h in enumerate(sizes):
        if 2 * h < tr:
            same_span = (span_of_row // h) == (span_of_col // h)
            pair[n] = [jnp.where(same_span, p, 0.0) for p in pair[n]]
    added = [jnp.concatenate([_dot(p.astype(BF16), vl[:, hs]) for p, hs in zip(pair[n], heads)],
                             axis=1) for n, (_, _, vl) in enumerate(operands)]
    for n, h in enumerate(sizes):
        o3 = split(o, h)
        upper = o3[:, h:] + added[n].reshape(tr // (2 * h), h, W_C)
        o = jnp.concatenate([o3[:, :h], upper], axis=1).reshape(tr, W_C)
    o_ref[...] = o

    row_id = _iota((HGRN_DIAG, 1), 0)

    def diag(n, carry):
        rows = pl.ds(pl.multiple_of(n * HGRN_DIAG, HGRN_DIAG), HGRN_DIAG)
        g8, q8, k8, v8 = g_ref[rows, :], q_ref[rows, :], k_ref[rows, :], v_ref[rows, :]
        acc = [jnp.zeros((HGRN_DIAG, DK_C), F32) for _ in heads]
        for j in range(HGRN_DIAG):
            gj, kj, vj = g8[j:j + 1, :], k8[j:j + 1, :], v8[j:j + 1, :]
            p = q8 * (kj * jnp.exp2(g8 - gj))
            for h_i, hs in enumerate(heads):
                a = jnp.sum(p[:, hs], axis=-1, keepdims=True)
                acc[h_i] = acc[h_i] + jnp.where(row_id >= j, a, 0.0) * vj[:, hs]
        o_ref[rows, :] += jnp.concatenate(acc, axis=1)
        return carry

    lax.fori_loop(0, tr // HGRN_DIAG, diag, 0, unroll=4)

    @pl.when(r == pl.num_programs(1) - 1)
    def _():
        for h in range(N_HEADS):
            sout_ref[0, h] = st_ref[h].T


def _hgrn(qc, kc, lf, ic, s0, *, nb, t, tr):
    assert t % tr == 0 and tr % (2 * HGRN_DIAG) == 0 and tr & (tr - 1) == 0
    nr = t // tr
    row = pl.BlockSpec((tr, W_C), lambda b, r: (b * nr + r, 0))
    state = pl.BlockSpec((1, N_HEADS, DK_C, DK_C), lambda b, r: (b, 0, 0, 0))
    return pl.pallas_call(
        functools.partial(_hgrn_kernel, tr=tr),
        grid=(nb, nr),
        in_specs=[row, row, row, row, state],
        out_specs=[row, state],
        out_shape=[jax.ShapeDtypeStruct((nb * t, W_C), F32),
                   jax.ShapeDtypeStruct((nb, N_HEADS, DK_C, DK_C), F32)],
        scratch_shapes=[pltpu.VMEM((N_HEADS, DK_C, DK_C), F32), pltpu.VMEM((tr, W_C), F32)],
        compiler_params=_cparams(2),
        name="hgrn",
    )(qc, kc, lf, ic, s0)


def _merge_kernel(x_ref, oa_ref, ob_ref, oc_ref, gc_ref, on_ref, wo_ref, nf_ref, wu_ref, wd_ref,
                  y_ref, *, f_chunk):
    oc = oc_ref[...]
    gc = gc_ref[...]
    parts = []
    for h in range(N_HEADS):
        cs = slice(h * DK_C, (h + 1) * DK_C)
        och = oc[:, cs]
        ms = jnp.mean(och * och, axis=-1, keepdims=True)
        parts.append(och * lax.rsqrt(ms + EPS) * on_ref[...])
    ocn = jnp.concatenate(parts, axis=-1) * (gc * (1.0 / (1.0 + jnp.exp(-gc))))
    mixed = jnp.concatenate([oa_ref[...], ob_ref[...], ocn.astype(BF16)], axis=-1)
    h_res = x_ref[...] + _dot(mixed, wo_ref[...])
    ms = jnp.mean(h_res * h_res, axis=-1, keepdims=True)
    hn = (h_res * lax.rsqrt(ms + EPS) * nf_ref[...]).astype(BF16)
    y = h_res
    d_ff = wu_ref.shape[1]
    for c in range(0, d_ff, f_chunk):
        u = jnp.maximum(_dot(hn, wu_ref[:, c:c + f_chunk]), 0.0)
        y = y + _dot((u * u).astype(BF16), wd_ref[c:c + f_chunk, :])
    y_ref[...] = y


def _merge(x2d, oa, ob, oc, gc, onorm, wo16, nf, wu16, wd16, *, tm):
    m, d = x2d.shape
    assert m % tm == 0
    row = lambda w: pl.BlockSpec((tm, w), lambda i: (i, 0))
    full = _resident
    return pl.pallas_call(
        functools.partial(_merge_kernel, f_chunk=1024),
        grid=(m // tm,),
        in_specs=[row(d), row(W_ATT), row(W_ATT), row(W_C), row(W_C),
                  full(onorm), full(wo16), full(nf), full(wu16), full(wd16)],
        out_specs=row(d),
        out_shape=jax.ShapeDtypeStruct((m, d), F32),
        compiler_params=_cparams(1),
        name="merge_ffn",
    )(x2d, oa, ob, oc, gc, onorm, wo16, nf, wu16, wd16)


def kernel(x_prompt, x_sample, cache_a_k, cache_a_v, cache_b_k, cache_b_v, state_c, norm_mix, w_in, qnorm_a, knorm_a, qnorm_b, knorm_b, rel_bias_b, lower_bounds, onorm_c, w_o, norm_ffn, w_up, w_down):
    depth = w_in.shape[0]
    bp, tp, d = x_prompt.shape
    bs, ts, _ = x_sample.shape
    past = cache_a_k.shape[2]
    band_rows_p = min(N_PREV_CHUNKS * CHUNK, tp)
    assert bp == 1

    xp = x_prompt.reshape(bp * tp, d)
    xs = x_sample.reshape(bs * ts, d)
    time_minor = lambda c: jnp.transpose(c, (0, 1, 3, 4, 2)).reshape(depth, bs, W_ATT, c.shape[2])
    cak, cav, cbk, cbv = (time_minor(c) for c in (cache_a_k, cache_a_v, cache_b_k, cache_b_v))
    heads_of = lambda a_t, n: jnp.transpose(
        a_t.reshape(N_HEADS, D_HEAD, n, a_t.shape[1] // n), (2, 3, 0, 1))
    lbounds = lower_bounds.astype(F32)
    zero_state = jnp.zeros((bp, N_HEADS, DK_C, DK_C), F32)

    tm_p = min(512, tp)
    tq_band = min(256, tp)
    tr_p = min(256, tp)

    outs_p = [[] for _ in range(5)]
    outs_s = [[] for _ in range(5)]
    carried = ()
    for l in range(depth):
        w16 = w_in[l].astype(BF16)
        wo16 = w_o[l].astype(BF16)
        wu16 = w_up[l].astype(BF16)
        wd16 = w_down[l].astype(BF16)
        nm = norm_mix[l].reshape(1, d)
        nf = norm_ffn[l].reshape(1, d)
        tile_h = lambda g: jnp.tile(g.reshape(1, D_HEAD), (1, N_HEADS))
        qna, kna, qnb, knb = (tile_h(g[l]) for g in (qnorm_a, knorm_a, qnorm_b, knorm_b))
        onorm = onorm_c[l].reshape(1, DK_C)
        table = rel_bias_b[l].astype(F32)

        (qa, ka_all, va_all, ka16_t, va16, qb, kb_t, vb_t, kb16_t, vb16, qc, kc, lf, ic, gc) = _proj(
            xp, nm, w16, qna, kna, qnb, knb, lbounds, layer=l, tm=tm_p, time_minor=True,
            carried=carried)
        carried = (ka_all, va_all)
        oa = _sb_prompt(qa, ka16_t, va16, tq=min(SB_TQ, tp))
        ob = _band_prompt(qb, kb16_t, vb16, table, tq=tq_band)
        oc, s_p = _hgrn(qc, kc, lf, ic, zero_state, nb=bp, t=tp, tr=tr_p)
        xp = _merge(xp, oa, ob, oc, gc, onorm, wo16, nf, wu16, wd16, tm=tm_p)
        outs_p[2].append(heads_of(kb_t, bp)[:, tp - band_rows_p:])
        outs_p[3].append(heads_of(vb_t, bp)[:, tp - band_rows_p:])
        outs_p[4].append(s_p)

        head = lambda a, n: a.reshape(n, -1, N_HEADS, D_HEAD)
        (qa, ka, va, ka16, va16, qb, kb, vb, kb16, vb16, qc, kc, lf, ic, gc) = _proj(
            xs, nm, w16, qna, kna, qnb, knb, lbounds, layer=l, tm=bs * ts, time_minor=False)
        oa = _sb_sample(qa, ka16, va16, cak, cav, layer=l, nb=bs, ts=ts)
        ob = _band_sample(qb, kb16, vb16, cbk, cbv, table, layer=l, nb=bs, ts=ts, past=past)
        oc, s_s = _hgrn(qc, kc, lf, ic, state_c[l].astype(F32), nb=bs, t=ts, tr=ts)
        xs = _merge(xs, oa, ob, oc, gc, onorm, wo16, nf, wu16, wd16, tm=bs * ts)
        outs_s[0].append(head(ka, bs))
        outs_s[1].append(head(va, bs))
        outs_s[2].append(head(kb, bs))
        outs_s[3].append(head(vb, bs))
        outs_s[4].append(s_s)

    stack = lambda xs_: jnp.stack(xs_)
    all_heads_of = lambda a: jnp.transpose(
        a.reshape(depth, N_HEADS, D_HEAD, bp, tp), (0, 3, 4, 1, 2))
    ka_all, va_all = carried
    return (xp.reshape(bp, tp, d), xs.reshape(bs, ts, d),
            all_heads_of(ka_all), all_heads_of(va_all),
            stack(outs_p[2]), stack(outs_p[3]), stack(outs_p[4]),
            stack(outs_s[0]), stack(outs_s[1]), stack(outs_s[2]), stack(outs_s[3]), stack(outs_s[4]))
```

```python
import functools
import math

import jax
import jax.numpy as jnp
from jax import lax
from jax.experimental import pallas as pl
from jax.experimental.pallas import tpu as pltpu

F32 = jnp.float32
BF16 = jnp.bfloat16

D_HEAD = 64
N_HEADS = 4
W_ATT = N_HEADS * D_HEAD
DK_C = 128
W_C = N_HEADS * DK_C
CHUNK = 64
N_PREV_CHUNKS = 8
REL_CLIP = 128
N_REL = 2 * REL_CLIP + 1
EPS = 1e-6
NEG_BIG = -1e30
LB_FLOOR = 1e-30
SB_UNDERFLOW = 88.0
PROJ_PART = 256
FFN_CHUNK = 1024
SB_TK = 128
SB_TQ = 256
HGRN_DIAG = 8
VMEM_LIMIT = 56 * 1024 * 1024


def _cparams(n_axes):
    return pltpu.CompilerParams(dimension_semantics=("arbitrary",) * n_axes,
                                vmem_limit_bytes=VMEM_LIMIT)


def _resident(a):
    zeros = (0,) * a.ndim
    return pl.BlockSpec(a.shape, lambda *_: zeros, pipeline_mode=pl.Buffered(1))


def _split3(x):
    h1 = x.astype(BF16)
    r1 = x - h1.astype(F32)
    h2 = r1.astype(BF16)
    h3 = (r1 - h2.astype(F32)).astype(BF16)
    return h1, h2, h3


def _dot(a, b):
    return jnp.dot(a, b, preferred_element_type=F32)


def _dot_nt(a, b):
    return lax.dot_general(a, b, (((1,), (1,)), ((), ())), preferred_element_type=F32)


def _dot_tn(a, b):
    return lax.dot_general(a, b, (((0,), (0,)), ((), ())), preferred_element_type=F32)


def _iota(shape, dim):
    return lax.broadcasted_iota(jnp.int32, shape, dim)


def _proj_kernel(x_ref, nm_ref, w_ref, qna_ref, kna_ref, qnb_ref, knb_ref, lb_ref, *rest,
                 layer, time_minor):
    (qa_ref, ka_ref, va_ref, ka16_ref, va16_ref, qb_ref, kb_ref, vb_ref, kb16_ref, vb16_ref,
     qc_ref, kc_ref, lf_ref, ic_ref, gc_ref) = rest[-15:]
    if time_minor:
        ka_ref, va_ref = ka_ref.at[0], va_ref.at[0]
    x = x_ref[...]
    ms = jnp.mean(x * x, axis=-1, keepdims=True)
    xn = (x * lax.rsqrt(ms + EPS) * nm_ref[...]).astype(BF16)

    tm = x.shape[0]
    part = PROJ_PART if tm % PROJ_PART == 0 else tm
    projected = [_dot(xn[r0:r0 + part], w_ref[...]) for r0 in range(0, tm, part)]

    same_head = (_iota((W_ATT, W_ATT), 0) // D_HEAD) == (_iota((W_ATT, W_ATT), 1) // D_HEAD)
    seg_mean = jnp.where(same_head, 1.0 / D_HEAD, 0.0).astype(BF16)
    inv_sqrt_d = 1.0 / math.sqrt(D_HEAD)

    lbr = lb_ref[...]
    e = jnp.exp(lbr - jnp.max(lbr, axis=0, keepdims=True))
    sm = e / jnp.sum(e, axis=0, keepdims=True)
    lb = jnp.sum(sm[0:layer + 1], axis=0, keepdims=True) - sm[0:1]
    log_lb = jnp.log(jnp.maximum(lb, LB_FLOOR))
    log_1m_lb = jnp.log1p(-lb)

    def head_norm(p, g_ref):
        m = _dot((p * p).astype(BF16), seg_mean)
        return p * lax.rsqrt(m + EPS) * g_ref[...]

    for n, p in enumerate(projected):
        rows = slice(n * part, (n + 1) * part)

        def put_k(k, k_ref, k16_ref):
            if time_minor:
                k_ref[:, rows] = k.T
                k16_ref[:, rows] = k.T.astype(BF16)
            else:
                k_ref[rows, :] = k
                k16_ref[rows, :] = k.astype(BF16)

        def put_v(v, v_ref, v16_ref):
            if time_minor:
                v_ref[:, rows] = v.T
            else:
                v_ref[rows, :] = v
            v16_ref[rows, :] = v.astype(BF16)

        o = 0
        qa_ref[rows, :] = (head_norm(p[:, o:o + W_ATT], qna_ref) * inv_sqrt_d).astype(BF16)
        o += W_ATT
        put_k(head_norm(p[:, o:o + W_ATT], kna_ref), ka_ref, ka16_ref)
        o += W_ATT
        put_v(p[:, o:o + W_ATT], va_ref, va16_ref)
        o += W_ATT
        qb_ref[rows, :] = (head_norm(p[:, o:o + W_ATT], qnb_ref) * inv_sqrt_d).astype(BF16)
        o += W_ATT
        put_k(head_norm(p[:, o:o + W_ATT], knb_ref), kb_ref, kb16_ref)
        o += W_ATT
        put_v(p[:, o:o + W_ATT], vb_ref, vb16_ref)
        o += W_ATT
        qc_ref[rows, :] = p[:, o:o + W_C] * (DK_C ** -0.5)
        o += W_C
        f_raw = p[:, o:o + W_C]
        o += W_C
        log_sig = jnp.minimum(f_raw, 0.0) - jnp.log1p(jnp.exp(-jnp.abs(f_raw)))
        b = log_1m_lb + log_sig
        lf_ref[rows, :] = jnp.maximum(log_lb, b) + jnp.log1p(jnp.exp(-jnp.abs(log_lb - b)))
        kc_ref[rows, :] = (1.0 - lb) * (1.0 / (1.0 + jnp.exp(f_raw)))
        ic_ref[rows, :] = p[:, o:o + W_C]
        o += W_C
        gc_ref[rows, :] = p[:, o:o + W_C]


def _proj(x2d, nm, w16, qna, kna, qnb, knb, lower_bounds, *, layer, tm, time_minor, carried=()):
    m, d = x2d.shape
    depth = lower_bounds.shape[0]
    assert m % tm == 0 and len(carried) == (2 if time_minor and layer > 0 else 0)
    full = _resident
    row = lambda w, dt: (pl.BlockSpec((tm, w), lambda i: (i, 0)), jax.ShapeDtypeStruct((m, w), dt))
    col = lambda w, dt: (pl.BlockSpec((w, tm), lambda i: (0, i)), jax.ShapeDtypeStruct((w, m), dt))
    layered = lambda w, dt: (pl.BlockSpec((1, w, tm), lambda i: (layer, 0, i)),
                             jax.ShapeDtypeStruct((depth, w, m), dt))
    kv = col if time_minor else row
    kv_a = layered if time_minor else row
    att_a = [row(W_ATT, BF16), kv_a(W_ATT, F32), kv_a(W_ATT, F32), kv(W_ATT, BF16), row(W_ATT, BF16)]
    att_b = [row(W_ATT, BF16), kv(W_ATT, F32), kv(W_ATT, F32), kv(W_ATT, BF16), row(W_ATT, BF16)]
    outs = att_a + att_b + [row(W_C, F32)] * 5
    n_in = 8
    return pl.pallas_call(
        functools.partial(_proj_kernel, layer=layer, time_minor=time_minor),
        grid=(m // tm,),
        in_specs=[row(d, F32)[0], full(nm), full(w16), full(qna), full(kna), full(qnb), full(knb),
                  full(lower_bounds)] + [pl.BlockSpec(memory_space=pl.ANY)] * len(carried),
        out_specs=[spec for spec, _ in outs],
        out_shape=[shape for _, shape in outs],
        input_output_aliases={n_in + n: 1 + n for n in range(len(carried))},
        compiler_params=_cparams(1),
        name="proj",
    )(x2d, nm, w16, qna, kna, qnb, knb, lower_bounds, *carried)


def _head_masks(width):
    lane_head = _iota((1, width), 1) // D_HEAD
    return [lane_head == h for h in range(N_HEADS)]


def _stack_heads(x, head_lane):
    return jnp.concatenate([jnp.where(m, x, jnp.zeros_like(x)) for m in head_lane], axis=0)


def _stack_heads_t(x_t, n):
    row_head = _iota((W_ATT, 1), 0) // D_HEAD
    return jnp.concatenate(
        [jnp.where(row_head == h, x_t, jnp.zeros_like(x_t)) for h in range(N_HEADS)], axis=1)


def _stack_groups(x, head_lane, gq):
    return jnp.concatenate(
        [_stack_heads(x[g:g + gq], head_lane) for g in range(0, x.shape[0], gq)], axis=0)


def _sb_block(z, weigh_values, mask, tri, carry_ref, acc_ref, gq):
    tk = z.shape[1]
    t = jnp.log(1.0 + jnp.exp2(jnp.abs(z) * -math.log2(math.e)))
    log_1m = jnp.minimum(-z, 0.0) - t
    log_beta = log_1m + z
    if mask is not None:
        log_1m = jnp.where(mask, log_1m, 0.0)
    hi = log_1m.astype(BF16)
    lo = (log_1m - hi.astype(F32)).astype(BF16)
    later = _dot(jnp.concatenate([hi, lo], axis=1), tri)
    carry = carry_ref[...]
    w = jnp.exp(log_beta + later + carry[:, :tk])
    if mask is not None:
        w = jnp.where(mask, w, 0.0)
    w = w.astype(BF16)
    blocks = [w[n * gq:(n + 1) * gq] for n in range(z.shape[0] // gq)]
    w_heads = jnp.concatenate(
        [jnp.concatenate(blocks[g:g + N_HEADS], axis=1) for g in range(0, len(blocks), N_HEADS)],
        axis=0)
    acc_ref[...] += weigh_values(w_heads)
    carry_ref[...] = carry + jnp.sum(log_1m, axis=-1, keepdims=True)


def _sb_live(carry_ref):
    return (jnp.max(carry_ref[...]) > -SB_UNDERFLOW).astype(jnp.int32)


def _strict_upper(n):
    return jnp.where(_iota((2 * n, n), 0) % n > _iota((2 * n, n), 1), 1.0, 0.0).astype(BF16)


def _sb_prompt_kernel(q_ref, kt_ref, v_ref, o_ref, carry_ref, acc_ref, *, tq):
    i = pl.program_id(0)
    head_lane = _head_masks(W_ATT)
    gq = SB_TK
    n_groups = tq // gq
    q_stack = _stack_groups(q_ref[...], head_lane, gq)
    tri = _strict_upper(SB_TK)
    carry_ref[...] = jnp.zeros_like(carry_ref)
    acc_ref[...] = jnp.zeros_like(acc_ref)

    def add_block(j, first_group, mask):
        r0 = pl.multiple_of(j * SB_TK, SB_TK)
        s0 = first_group * N_HEADS * gq
        z = _dot(q_stack[s0:], kt_ref[:, pl.ds(r0, SB_TK)])
        v_stack = _stack_heads(v_ref[pl.ds(r0, SB_TK), :], head_lane)
        _sb_block(z, lambda w: _dot(w, v_stack), mask, tri,
                  carry_ref.at[pl.ds(s0, z.shape[0])],
                  acc_ref.at[pl.ds(first_group * gq, (n_groups - first_group) * gq)], gq)

    for g in reversed(range(n_groups)):
        shape = ((n_groups - g) * N_HEADS * gq, SB_TK)
        row = _iota(shape, 0)
        add_block(i * n_groups + g, g, jnp.logical_or(_iota(shape, 1) < row % gq, row >= N_HEADS * gq))
    add_block(jnp.maximum(i * n_groups - 1, 0), 0,
              jnp.broadcast_to(i > 0, (n_groups * N_HEADS * gq, SB_TK)))

    def cond(c):
        j, live = c
        return jnp.logical_and(j >= 0, live > 0)

    def body(c):
        j, _ = c
        add_block(j, 0, None)
        return j - 1, _sb_live(carry_ref)

    lax.while_loop(cond, body, (i * n_groups - 2, _sb_live(carry_ref)))
    o_ref[...] = acc_ref[...].astype(o_ref.dtype)


def _sb_prompt(q16, kt16, v16, *, tq):
    t = q16.shape[0]
    assert t % tq == 0 and tq % SB_TK == 0
    return pl.pallas_call(
        functools.partial(_sb_prompt_kernel, tq=tq),
        grid=(t // tq,),
        in_specs=[pl.BlockSpec((tq, W_ATT), lambda i: (i, 0)),
                  _resident(kt16), _resident(v16)],
        out_specs=pl.BlockSpec((tq, W_ATT), lambda i: (i, 0)),
        out_shape=jax.ShapeDtypeStruct((t, W_ATT), BF16),
        scratch_shapes=[pltpu.VMEM((N_HEADS * tq, SB_TK), F32), pltpu.VMEM((tq, W_ATT), F32)],
        compiler_params=_cparams(1),
        name="sb_prompt",
    )(q16, kt16, v16)


def _sb_sample_kernel(q_ref, kn_ref, vn_ref, ckt_ref, cvt_ref, o_ref, carry_ref, acc_ref, *, ts, past):
    head_lane = _head_masks(W_ATT)
    q_stack = _stack_heads(q_ref[...], head_lane)
    carry_ref[...] = jnp.zeros_like(carry_ref)
    acc_ref[...] = jnp.zeros_like(acc_ref)
    shape = (N_HEADS * ts, ts)
    mask = _iota(shape, 1) < _iota(shape, 0) % ts
    v_new = _stack_heads(vn_ref[...], head_lane)
    _sb_block(_dot_nt(q_stack, kn_ref[...]), lambda w: _dot(w, v_new), mask, _strict_upper(ts),
              carry_ref, acc_ref, ts)
    tri = _strict_upper(SB_TK)

    def cond(c):
        j, live = c
        return jnp.logical_and(j >= 0, live > 0)

    def body(c):
        j, _ = c
        r0 = pl.multiple_of(j * SB_TK, SB_TK)
        z = _dot(q_stack, ckt_ref[0, 0, :, pl.ds(r0, SB_TK)].astype(BF16))
        vt_stack = _stack_heads_t(cvt_ref[0, 0, :, pl.ds(r0, SB_TK)].astype(BF16), SB_TK)
        _sb_block(z, lambda w: _dot_nt(w, vt_stack), None, tri, carry_ref, acc_ref, ts)
        return j - 1, _sb_live(carry_ref)

    lax.while_loop(cond, body, (past // SB_TK - 1, _sb_live(carry_ref)))
    o_ref[...] = acc_ref[...].astype(o_ref.dtype)


def _sb_sample(q16, kn16, vn16, cache_kt, cache_vt, *, layer, nb, ts):
    past = cache_kt.shape[3]
    assert past % SB_TK == 0
    row = pl.BlockSpec((ts, W_ATT), lambda b: (b, 0))
    cache = pl.BlockSpec((1, 1, W_ATT, past), lambda b: (layer, b, 0, 0))
    return pl.pallas_call(
        functools.partial(_sb_sample_kernel, ts=ts, past=past),
        grid=(nb,),
        in_specs=[row, row, row, cache, cache],
        out_specs=row,
        out_shape=jax.ShapeDtypeStruct((nb * ts, W_ATT), BF16),
        scratch_shapes=[pltpu.VMEM((N_HEADS * ts, SB_TK), F32), pltpu.VMEM((ts, W_ATT), F32)],
        compiler_params=_cparams(1),
        name="sb_sample",
    )(q16, kn16, vn16, cache_kt, cache_vt)


def _rel_bias(table_ref, h, dist):
    idx = jnp.clip(dist, -REL_CLIP, REL_CLIP) + REL_CLIP

    def body(r, b):
        return jnp.where(idx == r, table_ref[h, r], b)

    return lax.fori_loop(0, N_REL, body, jnp.zeros(dist.shape, F32))


def _row_reduce(x, combine, reduce):
    width = x.shape[1]
    acc = x[:, :128]
    for c in range(128, width, 128):
        acc = combine(acc, x[:, c:c + 128])
    return reduce(acc, axis=-1, keepdims=True)


def _toeplitz_bias(table_ref, h, rows, back, width):
    lane = _iota((8, width), 1)
    by_diag = _rel_bias(table_ref, h, back + rows - lane)
    tiled = jnp.concatenate([by_diag] * (rows // 8), axis=0)
    return pltpu.roll(tiled, width - rows, 1, stride=1, stride_axis=0)


BAND_TQ = 128
BAND_BACK = N_PREV_CHUNKS * CHUNK
BAND_WIN = BAND_BACK + BAND_TQ


def _band_prompt_kernel(table_ref, q_ref, k0_ref, k1_ref, k2_ref, v0_ref, v1_ref, v2_ref,
                        o_ref, bias_ref, *, tq):
    i = pl.program_id(0)
    n_kb = 3
    rows = _iota((BAND_TQ, BAND_WIN), 0)
    cols = _iota((BAND_TQ, BAND_WIN), 1)

    @pl.when(i == 0)
    def _():
        q_chunk = rows // CHUNK + N_PREV_CHUNKS
        k_chunk = cols // CHUNK
        in_band = jnp.logical_and(k_chunk >= q_chunk - N_PREV_CHUNKS, k_chunk <= q_chunk)
        for h in range(N_HEADS):
            bias = _toeplitz_bias(table_ref, h, BAND_TQ, BAND_BACK, BAND_WIN + BAND_TQ)
            bias_ref[h] = jnp.where(in_band, bias[:, :BAND_WIN], NEG_BIG)

    q = q_ref[...]
    head_lane = _head_masks(W_ATT)
    k_t = jnp.concatenate([k0_ref[...], k1_ref[...], k2_ref[...]], axis=1)
    vcat = jnp.concatenate([v0_ref[...], v1_ref[...], v2_ref[...]], axis=0)
    parts = [p * BAND_TQ for p in range(tq // BAND_TQ)]
    starts = [r0 + (n_kb - 1) * tq - BAND_BACK for r0 in parts]
    pairs = [(p, h) for p in range(len(parts)) for h in range(N_HEADS)]
    scores = []
    for p, h in pairs:
        qp = q[parts[p]:parts[p] + BAND_TQ]
        qh = jnp.where(head_lane[h], qp, jnp.zeros_like(qp))
        scores.append(_dot(qh, k_t[:, starts[p]:starts[p] + BAND_WIN]))
    weights = []
    inv_den = []
    for (p, h), s in zip(pairs, scores):
        exists = (i - (n_kb - 1)) * tq + starts[p] + cols >= 0
        s = jnp.where(exists, s + bias_ref[h], NEG_BIG)
        e = jnp.exp(s - _row_reduce(s, jnp.maximum, jnp.max))
        weights.append(e.astype(BF16))
        inv_den.append(1.0 / _row_reduce(e, jnp.add, jnp.sum))
    for p in range(len(parts)):
        out = jnp.zeros((BAND_TQ, W_ATT), F32)
        for h in range(N_HEADS):
            n = p * N_HEADS + h
            pv = _dot(weights[n], vcat[starts[p]:starts[p] + BAND_WIN]) * inv_den[n]
            out = jnp.where(head_lane[h], pv, out)
        o_ref[parts[p]:parts[p] + BAND_TQ, :] = out.astype(o_ref.dtype)


def _band_prompt(q16, kt16, v16, table, *, tq):
    t = q16.shape[0]
    assert t % tq == 0 and tq % BAND_TQ == 0 and 2 * tq >= BAND_BACK
    blk = lambda back: pl.BlockSpec((tq, W_ATT), lambda i: (jnp.maximum(i - back, 0), 0))
    blk_t = lambda back: pl.BlockSpec((W_ATT, tq), lambda i: (0, jnp.maximum(i - back, 0)))
    return pl.pallas_call(
        functools.partial(_band_prompt_kernel, tq=tq),
        grid=(t // tq,),
        in_specs=[pl.BlockSpec(memory_space=pltpu.SMEM),
                  blk(0), blk_t(2), blk_t(1), blk_t(0), blk(2), blk(1), blk(0)],
        out_specs=blk(0),
        out_shape=jax.ShapeDtypeStruct((t, W_ATT), BF16),
        scratch_shapes=[pltpu.VMEM((N_HEADS, BAND_TQ, BAND_WIN), F32)],
        compiler_params=_cparams(1),
        name="band_prompt",
    )(table, q16, kt16, kt16, kt16, v16, v16, v16)


def _band_sample_kernel(table_ref, q_ref, kn_ref, vn_ref, ckt_ref, cvt_ref, o_ref, bias_ref, *, ts, past):
    lb = ckt_ref.shape[3]

    @pl.when(pl.program_id(0) == 0)
    def _():
        for h in range(N_HEADS):
            bias_ref[h] = _toeplitz_bias(table_ref, h, ts, lb, bias_ref.shape[2])

    q = q_ref[...]
    head_lane = _head_masks(W_ATT)
    kc_t = ckt_ref[0, 0].astype(BF16)
    vc_t = cvt_ref[0, 0].astype(BF16)
    kn = kn_ref[...]
    vn = vn_ref[...]
    q_pos_c = past + _iota((ts, lb), 0)
    k_pos_c = past - lb + _iota((ts, lb), 1)
    q_pos_n = past + _iota((ts, ts), 0)
    k_pos_n = past + _iota((ts, ts), 1)

    def allowed(q_pos, k_pos):
        qc, kc_ = q_pos // CHUNK, k_pos // CHUNK
        return jnp.logical_and(kc_ >= qc - N_PREV_CHUNKS, kc_ <= qc)

    ok_c = allowed(q_pos_c, k_pos_c)
    ok_n = allowed(q_pos_n, k_pos_n)
    out = jnp.zeros((ts, W_ATT), F32)
    for h in range(N_HEADS):
        qh = jnp.where(head_lane[h], q, jnp.zeros_like(q))
        bias = bias_ref[h]
        s_c = jnp.where(ok_c, _dot(qh, kc_t) + bias[:, :lb], NEG_BIG)
        s_n = jnp.where(ok_n, _dot_nt(qh, kn) + bias[:, lb:lb + ts], NEG_BIG)
        m = jnp.maximum(jnp.max(s_c, axis=-1, keepdims=True), jnp.max(s_n, axis=-1, keepdims=True))
        e_c = jnp.exp(s_c - m)
        e_n = jnp.exp(s_n - m)
        den = jnp.sum(e_c, axis=-1, keepdims=True) + jnp.sum(e_n, axis=-1, keepdims=True)
        pv = (_dot_nt(e_c.astype(BF16), vc_t) + _dot(e_n.astype(BF16), vn)) / den
        out = jnp.where(head_lane[h], pv, out)
    o_ref[...] = out.astype(o_ref.dtype)


def _band_sample(q16, kn16, vn16, cache_kt, cache_vt, table, *, layer, nb, ts, past):
    lb = cache_kt.shape[3]
    row = pl.BlockSpec((ts, W_ATT), lambda b: (b, 0))
    cache = pl.BlockSpec((1, 1, W_ATT, lb), lambda b: (layer, b, 0, 0))
    return pl.pallas_call(
        functools.partial(_band_sample_kernel, ts=ts, past=past),
        grid=(nb,),
        in_specs=[pl.BlockSpec(memory_space=pltpu.SMEM), row, row, row, cache, cache],
        out_specs=row,
        out_shape=jax.ShapeDtypeStruct((nb * ts, W_ATT), BF16),
        scratch_shapes=[pltpu.VMEM((N_HEADS, ts, pl.cdiv(lb + 2 * ts, 128) * 128), F32)],
        compiler_params=_cparams(1),
        name="band_sample",
    )(table, q16, kn16, vn16, cache_kt, cache_vt)


def _hgrn_block(q, k, lf, v, st_ref):
    tr = q.shape[0]
    heads = [slice(h * DK_C, (h + 1) * DK_C) for h in range(N_HEADS)]
    tri = jnp.where(_iota((tr, tr), 0) >= _iota((tr, tr), 1), 1.0, 0.0).astype(BF16)
    f1, f2, f3 = _split3(lf)
    g = (_dot(tri, f1) + _dot(tri, f2) + _dot(tri, f3)) * math.log2(math.e)
    v16 = v.astype(BF16)

    g_end = g[tr - 1:tr, :]
    qt = (q * jnp.exp2(g)).astype(BF16)
    kd = (k * jnp.exp2(g_end - g)).astype(BF16)
    dec = jnp.exp2(g_end)
    states = [st_ref[h] for h in range(N_HEADS)]
    o_heads = [_dot_nt(qt[:, hs], st.astype(BF16)) for hs, st in zip(heads, states)]
    grown = [_dot_tn(v16[:, hs], kd[:, hs]) for hs in heads]
    for h, hs in enumerate(heads):
        st_ref[h] = states[h] * dec[:, hs] + grown[h]
    o = jnp.concatenate(o_heads, axis=1)

    sizes = []
    h = HGRN_DIAG
    while h < tr:
        sizes.append(h)
        h *= 2
    split = lambda x, h: x.reshape(tr // (2 * h), 2 * h, W_C)
    operands = []
    for h in sizes:
        g3, q3, k3, v3 = split(g, h), split(q, h), split(k, h), split(v16, h)
        c = g3[:, h - 1:h, :]
        qe = (q3[:, h:] * jnp.exp2(g3[:, h:] - c)).reshape(tr // 2, W_C).astype(BF16)
        ke = (k3[:, :h] * jnp.exp2(c - g3[:, :h])).reshape(tr // 2, W_C).astype(BF16)
        operands.append((qe, ke, v3[:, :h].reshape(tr // 2, W_C)))
    pair = [[_dot_nt(qe[:, hs], ke[:, hs]) for hs in heads] for qe, ke, _ in operands]
    span_of_row = _iota((tr // 2, tr // 2), 0)
    span_of_col = _iota((tr // 2, tr // 2), 1)
    for n, h in enumerate(sizes):
        if 2 * h < tr:
            same_span = (span_of_row // h) == (span_of_col // h)
            pair[n] = [jnp.where(same_span, p, 0.0) for p in pair[n]]
    added = [jnp.concatenate([_dot(p.astype(BF16), vl[:, hs]) for p, hs in zip(pair[n], heads)],
                             axis=1) for n, (_, _, vl) in enumerate(operands)]
    for n, h in enumerate(sizes):
        o3 = split(o, h)
        upper = o3[:, h:] + added[n].reshape(tr // (2 * h), h, W_C)
        o = jnp.concatenate([o3[:, :h], upper], axis=1).reshape(tr, W_C)

    row_id = _iota((HGRN_DIAG, 1), 0)
    diagonal = []
    for r0 in range(0, tr, HGRN_DIAG):
        rows = slice(r0, r0 + HGRN_DIAG)
        g8, q8, k8, v8 = g[rows], q[rows], k[rows], v[rows]
        acc = [jnp.zeros((HGRN_DIAG, DK_C), F32) for _ in heads]
        for j in range(HGRN_DIAG):
            gj, kj, vj = g8[j:j + 1, :], k8[j:j + 1, :], v8[j:j + 1, :]
            p = q8 * (kj * jnp.exp2(g8 - gj))
            for h_i, hs in enumerate(heads):
                a = jnp.sum(p[:, hs], axis=-1, keepdims=True)
                acc[h_i] = acc[h_i] + jnp.where(row_id >= j, a, 0.0) * vj[:, hs]
        diagonal.append(jnp.concatenate(acc, axis=1))
    return o + jnp.concatenate(diagonal, axis=0)


def _hgrn_kernel(q_ref, k_ref, lf_ref, v_ref, s0_ref, o_ref, sout_ref, st_ref):
    r = pl.program_id(1)

    @pl.when(r == 0)
    def _():
        for h in range(N_HEADS):
            st_ref[h] = s0_ref[0, h].T

    o_ref[...] = _hgrn_block(q_ref[...], k_ref[...], lf_ref[...], v_ref[...], st_ref)

    @pl.when(r == pl.num_programs(1) - 1)
    def _():
        for h in range(N_HEADS):
            sout_ref[0, h] = st_ref[h].T


def _hgrn(qc, kc, lf, ic, s0, *, nb, t, tr):
    assert t % tr == 0 and tr % (2 * HGRN_DIAG) == 0 and tr & (tr - 1) == 0
    nr = t // tr
    row = pl.BlockSpec((tr, W_C), lambda b, r: (b * nr + r, 0))
    state = pl.BlockSpec((1, N_HEADS, DK_C, DK_C), lambda b, r: (b, 0, 0, 0))
    return pl.pallas_call(
        _hgrn_kernel,
        grid=(nb, nr),
        in_specs=[row, row, row, row, state],
        out_specs=[row, state],
        out_shape=[jax.ShapeDtypeStruct((nb * t, W_C), F32),
                   jax.ShapeDtypeStruct((nb, N_HEADS, DK_C, DK_C), F32)],
        scratch_shapes=[pltpu.VMEM((N_HEADS, DK_C, DK_C), F32)],
        compiler_params=_cparams(2),
        name="hgrn",
    )(qc, kc, lf, ic, s0)


def _merge_block(x, oa, ob, oc, gc, on, wo_ref, nf, wu_ref, wd_ref):
    parts = []
    for h in range(N_HEADS):
        och = oc[:, h * DK_C:(h + 1) * DK_C]
        ms = jnp.mean(och * och, axis=-1, keepdims=True)
        parts.append(och * lax.rsqrt(ms + EPS) * on)
    ocn = jnp.concatenate(parts, axis=-1) * (gc * (1.0 / (1.0 + jnp.exp(-gc))))
    mixed = jnp.concatenate([oa, ob, ocn.astype(BF16)], axis=-1)
    h_res = x + _dot(mixed, wo_ref[...])
    ms = jnp.mean(h_res * h_res, axis=-1, keepdims=True)
    hn = (h_res * lax.rsqrt(ms + EPS) * nf).astype(BF16)
    y = h_res
    for c in range(0, wu_ref.shape[1], FFN_CHUNK):
        u = jnp.maximum(_dot(hn, wu_ref[:, c:c + FFN_CHUNK]), 0.0)
        y = y + _dot((u * u).astype(BF16), wd_ref[c:c + FFN_CHUNK, :])
    return y


def _merge_kernel(x_ref, oa_ref, ob_ref, oc_ref, gc_ref, on_ref, wo_ref, nf_ref, wu_ref, wd_ref,
                  y_ref):
    y_ref[...] = _merge_block(x_ref[...], oa_ref[...], ob_ref[...], oc_ref[...], gc_ref[...],
                              on_ref[...], wo_ref, nf_ref[...], wu_ref, wd_ref)


def _merge(x2d, oa, ob, oc, gc, onorm, wo16, nf, wu16, wd16, *, tm):
    m, d = x2d.shape
    assert m % tm == 0
    row = lambda w: pl.BlockSpec((tm, w), lambda i: (i, 0))
    full = _resident
    return pl.pallas_call(
        _merge_kernel,
        grid=(m // tm,),
        in_specs=[row(d), row(W_ATT), row(W_ATT), row(W_C), row(W_C),
                  full(onorm), full(wo16), full(nf), full(wu16), full(wd16)],
        out_specs=row(d),
        out_shape=jax.ShapeDtypeStruct((m, d), F32),
        compiler_params=_cparams(1),
        name="merge_ffn",
    )(x2d, oa, ob, oc, gc, onorm, wo16, nf, wu16, wd16)


def kernel(x_prompt, x_sample, cache_a_k, cache_a_v, cache_b_k, cache_b_v, state_c, norm_mix, w_in, qnorm_a, knorm_a, qnorm_b, knorm_b, rel_bias_b, lower_bounds, onorm_c, w_o, norm_ffn, w_up, w_down):
    depth = w_in.shape[0]
    bp, tp, d = x_prompt.shape
    bs, ts, _ = x_sample.shape
    past = cache_a_k.shape[2]
    band_rows_p = min(N_PREV_CHUNKS * CHUNK, tp)
    assert bp == 1

    xp = x_prompt.reshape(bp * tp, d)
    xs = x_sample.reshape(bs * ts, d)
    time_minor = lambda c: jnp.transpose(c, (0, 1, 3, 4, 2)).reshape(depth, bs, W_ATT, c.shape[2])
    cak, cav, cbk, cbv = (time_minor(c) for c in (cache_a_k, cache_a_v, cache_b_k, cache_b_v))
    heads_of = lambda a_t, n: jnp.transpose(
        a_t.reshape(N_HEADS, D_HEAD, n, a_t.shape[1] // n), (2, 3, 0, 1))
    lbounds = lower_bounds.astype(F32)
    zero_state = jnp.zeros((bp, N_HEADS, DK_C, DK_C), F32)

    tm_p = min(512, tp)
    tq_band = min(256, tp)
    tr_p = min(256, tp)

    outs_p = [[] for _ in range(5)]
    outs_s = [[] for _ in range(5)]
    carried = ()
    for l in range(depth):
        w16 = w_in[l].astype(BF16)
        wo16 = w_o[l].astype(BF16)
        wu16 = w_up[l].astype(BF16)
        wd16 = w_down[l].astype(BF16)
        nm = norm_mix[l].reshape(1, d)
        nf = norm_ffn[l].reshape(1, d)
        tile_h = lambda g: jnp.tile(g.reshape(1, D_HEAD), (1, N_HEADS))
        qna, kna, qnb, knb = (tile_h(g[l]) for g in (qnorm_a, knorm_a, qnorm_b, knorm_b))
        onorm = onorm_c[l].reshape(1, DK_C)
        table = rel_bias_b[l].astype(F32)

        (qa, ka_all, va_all, ka16_t, va16, qb, kb_t, vb_t, kb16_t, vb16, qc, kc, lf, ic, gc) = _proj(
            xp, nm, w16, qna, kna, qnb, knb, lbounds, layer=l, tm=tm_p, time_minor=True,
            carried=carried)
        carried = (ka_all, va_all)
        oa = _sb_prompt(qa, ka16_t, va16, tq=min(SB_TQ, tp))
        ob = _band_prompt(qb, kb16_t, vb16, table, tq=tq_band)
        oc, s_p = _hgrn(qc, kc, lf, ic, zero_state, nb=bp, t=tp, tr=tr_p)
        xp = _merge(xp, oa, ob, oc, gc, onorm, wo16, nf, wu16, wd16, tm=tm_p)
        outs_p[2].append(heads_of(kb_t, bp)[:, tp - band_rows_p:])
        outs_p[3].append(heads_of(vb_t, bp)[:, tp - band_rows_p:])
        outs_p[4].append(s_p)

        head = lambda a, n: a.reshape(n, -1, N_HEADS, D_HEAD)
        (qa, ka, va, ka16, va16, qb, kb, vb, kb16, vb16, qc, kc, lf, ic, gc) = _proj(
            xs, nm, w16, qna, kna, qnb, knb, lbounds, layer=l, tm=bs * ts, time_minor=False)
        oa = _sb_sample(qa, ka16, va16, cak, cav, layer=l, nb=bs, ts=ts)
        ob = _band_sample(qb, kb16, vb16, cbk, cbv, table, layer=l, nb=bs, ts=ts, past=past)
        oc, s_s = _hgrn(qc, kc, lf, ic, state_c[l].astype(F32), nb=bs, t=ts, tr=ts)
        xs = _merge(xs, oa, ob, oc, gc, onorm, wo16, nf, wu16, wd16, tm=bs * ts)
        outs_s[0].append(head(ka, bs))
        outs_s[1].append(head(va, bs))
        outs_s[2].append(head(kb, bs))
        outs_s[3].append(head(vb, bs))
        outs_s[4].append(s_s)

    stack = lambda xs_: jnp.stack(xs_)
    all_heads_of = lambda a: jnp.transpose(
        a.reshape(depth, N_HEADS, D_HEAD, bp, tp), (0, 3, 4, 1, 2))
    ka_all, va_all = carried
    return (xp.reshape(bp, tp, d), xs.reshape(bs, ts, d),
            all_heads_of(ka_all), all_heads_of(va_all),
            stack(outs_p[2]), stack(outs_p[3]), stack(outs_p[4]),
            stack(outs_s[0]), stack(outs_s[1]), stack(outs_s[2]), stack(outs_s[3]), stack(outs_s[4]))
```

```python
import functools
import math

import jax
import jax.numpy as jnp
from jax import lax
from jax.experimental import pallas as pl
from jax.experimental.pallas import tpu as pltpu

F32 = jnp.float32
BF16 = jnp.bfloat16

D_HEAD = 64
N_HEADS = 4
W_ATT = N_HEADS * D_HEAD
DK_C = 128
W_C = N_HEADS * DK_C
CHUNK = 64
N_PREV_CHUNKS = 8
REL_CLIP = 128
N_REL = 2 * REL_CLIP + 1
EPS = 1e-6
NEG_BIG = -1e30
LB_FLOOR = 1e-30
SB_UNDERFLOW = 88.0
PROJ_PART = 256
FFN_CHUNK = 1024
SB_TK = 128
SB_TQ = 256
HGRN_DIAG = 8
VMEM_LIMIT = 56 * 1024 * 1024


def _cparams(n_axes):
    return pltpu.CompilerParams(dimension_semantics=("arbitrary",) * n_axes,
                                vmem_limit_bytes=VMEM_LIMIT)


def _resident(a):
    zeros = (0,) * a.ndim
    return pl.BlockSpec(a.shape, lambda *_: zeros, pipeline_mode=pl.Buffered(1))


def _resident_layer(a, layer):
    index = (layer,) + (0,) * (a.ndim - 1)
    return pl.BlockSpec((1,) + a.shape[1:], lambda *_: index, pipeline_mode=pl.Buffered(1))


def _split3(x):
    h1 = x.astype(BF16)
    r1 = x - h1.astype(F32)
    h2 = r1.astype(BF16)
    h3 = (r1 - h2.astype(F32)).astype(BF16)
    return h1, h2, h3


def _dot(a, b):
    return jnp.dot(a, b, preferred_element_type=F32)


def _dot_nt(a, b):
    return lax.dot_general(a, b, (((1,), (1,)), ((), ())), preferred_element_type=F32)


def _dot_tn(a, b):
    return lax.dot_general(a, b, (((0,), (0,)), ((), ())), preferred_element_type=F32)


def _iota(shape, dim):
    return lax.broadcasted_iota(jnp.int32, shape, dim)


def _proj_kernel(x_ref, nm_ref, w_ref, qna_ref, kna_ref, qnb_ref, knb_ref, lb_ref, *rest,
                 layer, time_minor):
    (qa_ref, ka_ref, va_ref, ka16_ref, va16_ref, qb_ref, kb_ref, vb_ref, kb16_ref, vb16_ref,
     qc_ref, kc_ref, lf_ref, ic_ref, gc_ref) = rest[-15:]
    if time_minor:
        for earlier_ref, all_ref in zip(rest[:-15], (ka_ref, va_ref)):
            all_ref[0:layer] = earlier_ref[...]
        ka_ref, va_ref = ka_ref.at[layer], va_ref.at[layer]
    x = x_ref[...]
    ms = jnp.mean(x * x, axis=-1, keepdims=True)
    xn = (x * lax.rsqrt(ms + EPS) * nm_ref[...]).astype(BF16)

    tm = x.shape[0]
    part = PROJ_PART if tm % PROJ_PART == 0 else tm
    projected = [_dot(xn[r0:r0 + part], w_ref[0]) for r0 in range(0, tm, part)]

    same_head = (_iota((W_ATT, W_ATT), 0) // D_HEAD) == (_iota((W_ATT, W_ATT), 1) // D_HEAD)
    seg_mean = jnp.where(same_head, 1.0 / D_HEAD, 0.0).astype(BF16)
    inv_sqrt_d = 1.0 / math.sqrt(D_HEAD)

    lbr = lb_ref[...]
    e = jnp.exp(lbr - jnp.max(lbr, axis=0, keepdims=True))
    sm = e / jnp.sum(e, axis=0, keepdims=True)
    lb = jnp.sum(sm[0:layer + 1], axis=0, keepdims=True) - sm[0:1]
    log_lb = jnp.log(jnp.maximum(lb, LB_FLOOR))
    log_1m_lb = jnp.log1p(-lb)

    def head_norm(p, g_ref):
        m = _dot((p * p).astype(BF16), seg_mean)
        return p * lax.rsqrt(m + EPS) * g_ref[...]

    for n, p in enumerate(projected):
        rows = slice(n * part, (n + 1) * part)

        def put_k(k, k_ref, k16_ref):
            if time_minor:
                k_ref[:, rows] = k.T
                k16_ref[:, rows] = k.T.astype(BF16)
            else:
                k_ref[rows, :] = k
                k16_ref[rows, :] = k.astype(BF16)

        def put_v(v, v_ref, v16_ref):
            if time_minor:
                v_ref[:, rows] = v.T
            else:
                v_ref[rows, :] = v
            v16_ref[rows, :] = v.astype(BF16)

        o = 0
        qa_ref[rows, :] = (head_norm(p[:, o:o + W_ATT], qna_ref) * inv_sqrt_d).astype(BF16)
        o += W_ATT
        put_k(head_norm(p[:, o:o + W_ATT], kna_ref), ka_ref, ka16_ref)
        o += W_ATT
        put_v(p[:, o:o + W_ATT], va_ref, va16_ref)
        o += W_ATT
        qb_ref[rows, :] = (head_norm(p[:, o:o + W_ATT], qnb_ref) * inv_sqrt_d).astype(BF16)
        o += W_ATT
        put_k(head_norm(p[:, o:o + W_ATT], knb_ref), kb_ref, kb16_ref)
        o += W_ATT
        put_v(p[:, o:o + W_ATT], vb_ref, vb16_ref)
        o += W_ATT
        qc_ref[rows, :] = p[:, o:o + W_C] * (DK_C ** -0.5)
        o += W_C
        f_raw = p[:, o:o + W_C]
        o += W_C
        log_sig = jnp.minimum(f_raw, 0.0) - jnp.log1p(jnp.exp(-jnp.abs(f_raw)))
        b = log_1m_lb + log_sig
        lf_ref[rows, :] = jnp.maximum(log_lb, b) + jnp.log1p(jnp.exp(-jnp.abs(log_lb - b)))
        kc_ref[rows, :] = (1.0 - lb) * (1.0 / (1.0 + jnp.exp(f_raw)))
        ic_ref[rows, :] = p[:, o:o + W_C]
        o += W_C
        gc_ref[rows, :] = p[:, o:o + W_C]


def _proj(x2d, nm, w16, qna, kna, qnb, knb, lower_bounds, *, layer, tm, time_minor, carried=()):
    m, d = x2d.shape
    assert m % tm == 0 and len(carried) == (2 if time_minor and layer > 0 else 0)
    full = _resident
    row = lambda w, dt: (pl.BlockSpec((tm, w), lambda i: (i, 0)), jax.ShapeDtypeStruct((m, w), dt))
    col = lambda w, dt: (pl.BlockSpec((w, tm), lambda i: (0, i)), jax.ShapeDtypeStruct((w, m), dt))
    layers = lambda n, w: pl.BlockSpec((n, w, tm), lambda i: (0, 0, i))
    layered = lambda w, dt: (layers(layer + 1, w), jax.ShapeDtypeStruct((layer + 1, w, m), dt))
    kv = col if time_minor else row
    kv_a = layered if time_minor else row
    att_a = [row(W_ATT, BF16), kv_a(W_ATT, F32), kv_a(W_ATT, F32), kv(W_ATT, BF16), row(W_ATT, BF16)]
    att_b = [row(W_ATT, BF16), kv(W_ATT, F32), kv(W_ATT, F32), kv(W_ATT, BF16), row(W_ATT, BF16)]
    outs = att_a + att_b + [row(W_C, F32)] * 5
    return pl.pallas_call(
        functools.partial(_proj_kernel, layer=layer, time_minor=time_minor),
        grid=(m // tm,),
        in_specs=[row(d, F32)[0], full(nm), _resident_layer(w16, layer), full(qna), full(kna),
                  full(qnb), full(knb),
                  full(lower_bounds)] + [layers(layer, W_ATT)] * len(carried),
        out_specs=[spec for spec, _ in outs],
        out_shape=[shape for _, shape in outs],
        compiler_params=_cparams(1),
        name="proj",
    )(x2d, nm, w16, qna, kna, qnb, knb, lower_bounds, *carried)


def _head_masks(width):
    lane_head = _iota((1, width), 1) // D_HEAD
    return [lane_head == h for h in range(N_HEADS)]


def _stack_heads(x, head_lane):
    return jnp.concatenate([jnp.where(m, x, jnp.zeros_like(x)) for m in head_lane], axis=0)


def _stack_heads_t(x_t, n):
    row_head = _iota((W_ATT, 1), 0) // D_HEAD
    return jnp.concatenate(
        [jnp.where(row_head == h, x_t, jnp.zeros_like(x_t)) for h in range(N_HEADS)], axis=1)


def _stack_groups(x, head_lane, gq):
    return jnp.concatenate(
        [_stack_heads(x[g:g + gq], head_lane) for g in range(0, x.shape[0], gq)], axis=0)


def _sb_block(z, weigh_values, mask, tri, carry_ref, acc_ref, gq):
    tk = z.shape[1]
    t = jnp.log(1.0 + jnp.exp2(jnp.abs(z) * -math.log2(math.e)))
    log_1m = jnp.minimum(-z, 0.0) - t
    log_beta = log_1m + z
    if mask is not None:
        log_1m = jnp.where(mask, log_1m, 0.0)
    hi = log_1m.astype(BF16)
    lo = (log_1m - hi.astype(F32)).astype(BF16)
    later = _dot(jnp.concatenate([hi, lo], axis=1), tri)
    carry = carry_ref[...]
    w = jnp.exp(log_beta + later + carry[:, :tk])
    if mask is not None:
        w = jnp.where(mask, w, 0.0)
    w = w.astype(BF16)
    blocks = [w[n * gq:(n + 1) * gq] for n in range(z.shape[0] // gq)]
    w_heads = jnp.concatenate(
        [jnp.concatenate(blocks[g:g + N_HEADS], axis=1) for g in range(0, len(blocks), N_HEADS)],
        axis=0)
    acc_ref[...] += weigh_values(w_heads)
    carry_ref[...] = carry + jnp.sum(log_1m, axis=-1, keepdims=True)


def _sb_live(carry_ref):
    return (jnp.max(carry_ref[...]) > -SB_UNDERFLOW).astype(jnp.int32)


def _strict_upper(n):
    return jnp.where(_iota((2 * n, n), 0) % n > _iota((2 * n, n), 1), 1.0, 0.0).astype(BF16)


def _sb_prompt_kernel(q_ref, kt_ref, v_ref, o_ref, carry_ref, acc_ref, *, tq):
    i = pl.program_id(0)
    head_lane = _head_masks(W_ATT)
    gq = SB_TK
    n_groups = tq // gq
    q_stack = _stack_groups(q_ref[...], head_lane, gq)
    tri = _strict_upper(SB_TK)
    carry_ref[...] = jnp.zeros_like(carry_ref)
    acc_ref[...] = jnp.zeros_like(acc_ref)

    def add_block(j, first_group, mask):
        r0 = pl.multiple_of(j * SB_TK, SB_TK)
        s0 = first_group * N_HEADS * gq
        z = _dot(q_stack[s0:], kt_ref[:, pl.ds(r0, SB_TK)])
        v_stack = _stack_heads(v_ref[pl.ds(r0, SB_TK), :], head_lane)
        _sb_block(z, lambda w: _dot(w, v_stack), mask, tri,
                  carry_ref.at[pl.ds(s0, z.shape[0])],
                  acc_ref.at[pl.ds(first_group * gq, (n_groups - first_group) * gq)], gq)

    for g in reversed(range(n_groups)):
        shape = ((n_groups - g) * N_HEADS * gq, SB_TK)
        row = _iota(shape, 0)
        add_block(i * n_groups + g, g, jnp.logical_or(_iota(shape, 1) < row % gq, row >= N_HEADS * gq))
    add_block(jnp.maximum(i * n_groups - 1, 0), 0,
              jnp.broadcast_to(i > 0, (n_groups * N_HEADS * gq, SB_TK)))

    def cond(c):
        j, live = c
        return jnp.logical_and(j >= 0, live > 0)

    def body(c):
        j, _ = c
        add_block(j, 0, None)
        return j - 1, _sb_live(carry_ref)

    lax.while_loop(cond, body, (i * n_groups - 2, _sb_live(carry_ref)))
    o_ref[...] = acc_ref[...].astype(o_ref.dtype)


def _sb_prompt(q16, kt16, v16, *, tq):
    t = q16.shape[0]
    assert t % tq == 0 and tq % SB_TK == 0
    return pl.pallas_call(
        functools.partial(_sb_prompt_kernel, tq=tq),
        grid=(t // tq,),
        in_specs=[pl.BlockSpec((tq, W_ATT), lambda i: (i, 0)),
                  _resident(kt16), _resident(v16)],
        out_specs=pl.BlockSpec((tq, W_ATT), lambda i: (i, 0)),
        out_shape=jax.ShapeDtypeStruct((t, W_ATT), BF16),
        scratch_shapes=[pltpu.VMEM((N_HEADS * tq, SB_TK), F32), pltpu.VMEM((tq, W_ATT), F32)],
        compiler_params=_cparams(1),
        name="sb_prompt",
    )(q16, kt16, v16)


def _sb_sample_kernel(q_ref, kn_ref, vn_ref, ckt_ref, cvt_ref, o_ref, carry_ref, acc_ref, *, ts, past):
    head_lane = _head_masks(W_ATT)
    q_stack = _stack_heads(q_ref[...], head_lane)
    carry_ref[...] = jnp.zeros_like(carry_ref)
    acc_ref[...] = jnp.zeros_like(acc_ref)
    shape = (N_HEADS * ts, ts)
    mask = _iota(shape, 1) < _iota(shape, 0) % ts
    v_new = _stack_heads(vn_ref[...], head_lane)
    _sb_block(_dot_nt(q_stack, kn_ref[...]), lambda w: _dot(w, v_new), mask, _strict_upper(ts),
              carry_ref, acc_ref, ts)
    tri = _strict_upper(SB_TK)

    def cond(c):
        j, live = c
        return jnp.logical_and(j >= 0, live > 0)

    def body(c):
        j, _ = c
        r0 = pl.multiple_of(j * SB_TK, SB_TK)
        z = _dot(q_stack, ckt_ref[0, 0, :, pl.ds(r0, SB_TK)].astype(BF16))
        vt_stack = _stack_heads_t(cvt_ref[0, 0, :, pl.ds(r0, SB_TK)].astype(BF16), SB_TK)
        _sb_block(z, lambda w: _dot_nt(w, vt_stack), None, tri, carry_ref, acc_ref, ts)
        return j - 1, _sb_live(carry_ref)

    lax.while_loop(cond, body, (past // SB_TK - 1, _sb_live(carry_ref)))
    o_ref[...] = acc_ref[...].astype(o_ref.dtype)


def _sb_sample(q16, kn16, vn16, cache_kt, cache_vt, *, layer, nb, ts):
    past = cache_kt.shape[3]
    assert past % SB_TK == 0
    row = pl.BlockSpec((ts, W_ATT), lambda b: (b, 0))
    cache = pl.BlockSpec((1, 1, W_ATT, past), lambda b: (layer, b, 0, 0))
    return pl.pallas_call(
        functools.partial(_sb_sample_kernel, ts=ts, past=past),
        grid=(nb,),
        in_specs=[row, row, row, cache, cache],
        out_specs=row,
        out_shape=jax.ShapeDtypeStruct((nb * ts, W_ATT), BF16),
        scratch_shapes=[pltpu.VMEM((N_HEADS * ts, SB_TK), F32), pltpu.VMEM((ts, W_ATT), F32)],
        compiler_params=_cparams(1),
        name="sb_sample",
    )(q16, kn16, vn16, cache_kt, cache_vt)


def _rel_bias(table_ref, h, dist):
    idx = jnp.clip(dist, -REL_CLIP, REL_CLIP) + REL_CLIP

    def body(r, b):
        return jnp.where(idx == r, table_ref[h, r], b)

    return lax.fori_loop(0, N_REL, body, jnp.zeros(dist.shape, F32))


def _row_reduce(x, combine, reduce):
    width = x.shape[1]
    acc = x[:, :128]
    for c in range(128, width, 128):
        acc = combine(acc, x[:, c:c + 128])
    return reduce(acc, axis=-1, keepdims=True)


def _toeplitz_bias(table_ref, h, rows, back, width):
    lane = _iota((8, width), 1)
    by_diag = _rel_bias(table_ref, h, back + rows - lane)
    tiled = jnp.concatenate([by_diag] * (rows // 8), axis=0)
    return pltpu.roll(tiled, width - rows, 1, stride=1, stride_axis=0)


BAND_TQ = 128
BAND_BACK = N_PREV_CHUNKS * CHUNK
BAND_WIN = BAND_BACK + BAND_TQ


def _band_prompt_kernel(table_ref, q_ref, k0_ref, k1_ref, k2_ref, v0_ref, v1_ref, v2_ref,
                        o_ref, bias_ref, *, tq):
    i = pl.program_id(0)
    n_kb = 3
    rows = _iota((BAND_TQ, BAND_WIN), 0)
    cols = _iota((BAND_TQ, BAND_WIN), 1)

    @pl.when(i == 0)
    def _():
        q_chunk = rows // CHUNK + N_PREV_CHUNKS
        k_chunk = cols // CHUNK
        in_band = jnp.logical_and(k_chunk >= q_chunk - N_PREV_CHUNKS, k_chunk <= q_chunk)
        for h in range(N_HEADS):
            bias = _toeplitz_bias(table_ref, h, BAND_TQ, BAND_BACK, BAND_WIN + BAND_TQ)
            bias_ref[h] = jnp.where(in_band, bias[:, :BAND_WIN], NEG_BIG)

    q = q_ref[...]
    head_lane = _head_masks(W_ATT)
    k_t = jnp.concatenate([k0_ref[...], k1_ref[...], k2_ref[...]], axis=1)
    vcat = jnp.concatenate([v0_ref[...], v1_ref[...], v2_ref[...]], axis=0)
    parts = [p * BAND_TQ for p in range(tq // BAND_TQ)]
    starts = [r0 + (n_kb - 1) * tq - BAND_BACK for r0 in parts]
    scores = []
    for p in range(len(parts)):
        qp = q[parts[p]:parts[p] + BAND_TQ]
        k_win = k_t[:, starts[p]:starts[p] + BAND_WIN]
        scores.append([_dot(jnp.where(head_lane[h], qp, jnp.zeros_like(qp)), k_win)
                       for h in range(N_HEADS)])
    for p in range(len(parts)):
        exists = (i - (n_kb - 1)) * tq + starts[p] + cols >= 0
        weights = []
        inv_den = []
        for h, s in enumerate(scores[p]):
            s = jnp.where(exists, s + bias_ref[h], NEG_BIG)
            e = jnp.exp(s - _row_reduce(s, jnp.maximum, jnp.max))
            weights.append(e.astype(BF16))
            inv_den.append(1.0 / _row_reduce(e, jnp.add, jnp.sum))
        v_win = vcat[starts[p]:starts[p] + BAND_WIN]
        out = jnp.zeros((BAND_TQ, W_ATT), F32)
        for h in range(N_HEADS):
            out = jnp.where(head_lane[h], _dot(weights[h], v_win) * inv_den[h], out)
        o_ref[parts[p]:parts[p] + BAND_TQ, :] = out.astype(o_ref.dtype)


def _band_prompt(q16, kt16, v16, table, *, tq):
    t = q16.shape[0]
    assert t % tq == 0 and tq % BAND_TQ == 0 and 2 * tq >= BAND_BACK
    blk = lambda back: pl.BlockSpec((tq, W_ATT), lambda i: (jnp.maximum(i - back, 0), 0))
    blk_t = lambda back: pl.BlockSpec((W_ATT, tq), lambda i: (0, jnp.maximum(i - back, 0)))
    return pl.pallas_call(
        functools.partial(_band_prompt_kernel, tq=tq),
        grid=(t // tq,),
        in_specs=[pl.BlockSpec(memory_space=pltpu.SMEM),
                  blk(0), blk_t(2), blk_t(1), blk_t(0), blk(2), blk(1), blk(0)],
        out_specs=blk(0),
        out_shape=jax.ShapeDtypeStruct((t, W_ATT), BF16),
        scratch_shapes=[pltpu.VMEM((N_HEADS, BAND_TQ, BAND_WIN), F32)],
        compiler_params=_cparams(1),
        name="band_prompt",
    )(table, q16, kt16, kt16, kt16, v16, v16, v16)


def _band_sample_kernel(table_ref, q_ref, kn_ref, vn_ref, ckt_ref, cvt_ref, o_ref, bias_ref, *, ts, past):
    lb = ckt_ref.shape[3]

    @pl.when(pl.program_id(0) == 0)
    def _():
        for h in range(N_HEADS):
            bias_ref[h] = _toeplitz_bias(table_ref, h, ts, lb, bias_ref.shape[2])

    q = q_ref[...]
    head_lane = _head_masks(W_ATT)
    kc_t = ckt_ref[0, 0].astype(BF16)
    vc_t = cvt_ref[0, 0].astype(BF16)
    kn = kn_ref[...]
    vn = vn_ref[...]
    q_pos_c = past + _iota((ts, lb), 0)
    k_pos_c = past - lb + _iota((ts, lb), 1)
    q_pos_n = past + _iota((ts, ts), 0)
    k_pos_n = past + _iota((ts, ts), 1)

    def allowed(q_pos, k_pos):
        qc, kc_ = q_pos // CHUNK, k_pos // CHUNK
        return jnp.logical_and(kc_ >= qc - N_PREV_CHUNKS, kc_ <= qc)

    ok_c = allowed(q_pos_c, k_pos_c)
    ok_n = allowed(q_pos_n, k_pos_n)
    out = jnp.zeros((ts, W_ATT), F32)
    for h in range(N_HEADS):
        qh = jnp.where(head_lane[h], q, jnp.zeros_like(q))
        bias = bias_ref[h]
        s_c = jnp.where(ok_c, _dot(qh, kc_t) + bias[:, :lb], NEG_BIG)
        s_n = jnp.where(ok_n, _dot_nt(qh, kn) + bias[:, lb:lb + ts], NEG_BIG)
        m = jnp.maximum(jnp.max(s_c, axis=-1, keepdims=True), jnp.max(s_n, axis=-1, keepdims=True))
        e_c = jnp.exp(s_c - m)
        e_n = jnp.exp(s_n - m)
        den = jnp.sum(e_c, axis=-1, keepdims=True) + jnp.sum(e_n, axis=-1, keepdims=True)
        pv = (_dot_nt(e_c.astype(BF16), vc_t) + _dot(e_n.astype(BF16), vn)) / den
        out = jnp.where(head_lane[h], pv, out)
    o_ref[...] = out.astype(o_ref.dtype)


def _band_sample(q16, kn16, vn16, cache_kt, cache_vt, table, *, layer, nb, ts, past):
    lb = cache_kt.shape[3]
    row = pl.BlockSpec((ts, W_ATT), lambda b: (b, 0))
    cache = pl.BlockSpec((1, 1, W_ATT, lb), lambda b: (layer, b, 0, 0))
    return pl.pallas_call(
        functools.partial(_band_sample_kernel, ts=ts, past=past),
        grid=(nb,),
        in_specs=[pl.BlockSpec(memory_space=pltpu.SMEM), row, row, row, cache, cache],
        out_specs=row,
        out_shape=jax.ShapeDtypeStruct((nb * ts, W_ATT), BF16),
        scratch_shapes=[pltpu.VMEM((N_HEADS, ts, pl.cdiv(lb + 2 * ts, 128) * 128), F32)],
        compiler_params=_cparams(1),
        name="band_sample",
    )(table, q16, kn16, vn16, cache_kt, cache_vt)


def _hgrn_block(q, k, lf, v, st_ref):
    tr = q.shape[0]
    heads = [slice(h * DK_C, (h + 1) * DK_C) for h in range(N_HEADS)]
    tri = jnp.where(_iota((tr, tr), 0) >= _iota((tr, tr), 1), 1.0, 0.0).astype(BF16)
    f1, f2, f3 = _split3(lf)
    g = (_dot(tri, f1) + _dot(tri, f2) + _dot(tri, f3)) * math.log2(math.e)
    v16 = v.astype(BF16)

    g_end = g[tr - 1:tr, :]
    qt = (q * jnp.exp2(g)).astype(BF16)
    kd = (k * jnp.exp2(g_end - g)).astype(BF16)
    dec = jnp.exp2(g_end)
    states = [st_ref[h] for h in range(N_HEADS)]
    o_heads = [_dot_nt(qt[:, hs], st.astype(BF16)) for hs, st in zip(heads, states)]
    grown = [_dot_tn(v16[:, hs], kd[:, hs]) for hs in heads]
    for h, hs in enumerate(heads):
        st_ref[h] = states[h] * dec[:, hs] + grown[h]
    o = jnp.concatenate(o_heads, axis=1)

    sizes = []
    h = HGRN_DIAG
    while h < tr:
        sizes.append(h)
        h *= 2
    split = lambda x, h: x.reshape(tr // (2 * h), 2 * h, W_C)
    operands = []
    for h in sizes:
        g3, q3, k3, v3 = split(g, h), split(q, h), split(k, h), split(v16, h)
        c = g3[:, h - 1:h, :]
        qe = (q3[:, h:] * jnp.exp2(g3[:, h:] - c)).reshape(tr // 2, W_C).astype(BF16)
        ke = (k3[:, :h] * jnp.exp2(c - g3[:, :h])).reshape(tr // 2, W_C).astype(BF16)
        operands.append((qe, ke, v3[:, :h].reshape(tr // 2, W_C)))
    pair = [[_dot_nt(qe[:, hs], ke[:, hs]) for hs in heads] for qe, ke, _ in operands]
    span_of_row = _iota((tr // 2, tr // 2), 0)
    span_of_col = _iota((tr // 2, tr // 2), 1)
    for n, h in enumerate(sizes):
        if 2 * h < tr:
            same_span = (span_of_row // h) == (span_of_col // h)
            pair[n] = [jnp.where(same_span, p, 0.0) for p in pair[n]]
    added = [jnp.concatenate([_dot(p.astype(BF16), vl[:, hs]) for p, hs in zip(pair[n], heads)],
                             axis=1) for n, (_, _, vl) in enumerate(operands)]
    for n, h in enumerate(sizes):
        o3 = split(o, h)
        upper = o3[:, h:] + added[n].reshape(tr // (2 * h), h, W_C)
        o = jnp.concatenate([o3[:, :h], upper], axis=1).reshape(tr, W_C)

    row_id = _iota((HGRN_DIAG, 1), 0)
    diagonal = []
    for r0 in range(0, tr, HGRN_DIAG):
        rows = slice(r0, r0 + HGRN_DIAG)
        g8, q8, k8, v8 = g[rows], q[rows], k[rows], v[rows]
        acc = [jnp.zeros((HGRN_DIAG, DK_C), F32) for _ in heads]
        for j in range(HGRN_DIAG):
            gj, kj, vj = g8[j:j + 1, :], k8[j:j + 1, :], v8[j:j + 1, :]
            p = q8 * (kj * jnp.exp2(g8 - gj))
            for h_i, hs in enumerate(heads):
                a = jnp.sum(p[:, hs], axis=-1, keepdims=True)
                acc[h_i] = acc[h_i] + jnp.where(row_id >= j, a, 0.0) * vj[:, hs]
        diagonal.append(jnp.concatenate(acc, axis=1))
    return o + jnp.concatenate(diagonal, axis=0)


def _hgrn_kernel(q_ref, k_ref, lf_ref, v_ref, s0_ref, o_ref, sout_ref, st_ref):
    r = pl.program_id(1)

    @pl.when(r == 0)
    def _():
        for h in range(N_HEADS):
            st_ref[h] = s0_ref[0, h].T

    o_ref[...] = _hgrn_block(q_ref[...], k_ref[...], lf_ref[...], v_ref[...], st_ref)

    @pl.when(r == pl.num_programs(1) - 1)
    def _():
        for h in range(N_HEADS):
            sout_ref[0, h] = st_ref[h].T


def _hgrn(qc, kc, lf, ic, s0, *, nb, t, tr):
    assert t % tr == 0 and tr % (2 * HGRN_DIAG) == 0 and tr & (tr - 1) == 0
    nr = t // tr
    row = pl.BlockSpec((tr, W_C), lambda b, r: (b * nr + r, 0))
    state = pl.BlockSpec((1, N_HEADS, DK_C, DK_C), lambda b, r: (b, 0, 0, 0))
    return pl.pallas_call(
        _hgrn_kernel,
        grid=(nb, nr),
        in_specs=[row, row, row, row, state],
        out_specs=[row, state],
        out_shape=[jax.ShapeDtypeStruct((nb * t, W_C), F32),
                   jax.ShapeDtypeStruct((nb, N_HEADS, DK_C, DK_C), F32)],
        scratch_shapes=[pltpu.VMEM((N_HEADS, DK_C, DK_C), F32)],
        compiler_params=_cparams(2),
        name="hgrn",
    )(qc, kc, lf, ic, s0)


def _merge_block(x, oa, ob, oc, gc, on, wo_ref, nf, wu_ref, wd_ref):
    parts = []
    for h in range(N_HEADS):
        och = oc[:, h * DK_C:(h + 1) * DK_C]
        ms = jnp.mean(och * och, axis=-1, keepdims=True)
        parts.append(och * lax.rsqrt(ms + EPS) * on)
    ocn = jnp.concatenate(parts, axis=-1) * (gc * (1.0 / (1.0 + jnp.exp(-gc))))
    mixed = jnp.concatenate([oa, ob, ocn.astype(BF16)], axis=-1)
    h_res = x + _dot(mixed, wo_ref[0])
    ms = jnp.mean(h_res * h_res, axis=-1, keepdims=True)
    hn = (h_res * lax.rsqrt(ms + EPS) * nf).astype(BF16)
    y = h_res
    for c in range(0, wu_ref.shape[2], FFN_CHUNK):
        u = jnp.maximum(_dot(hn, wu_ref[0, :, c:c + FFN_CHUNK]), 0.0)
        y = y + _dot((u * u).astype(BF16), wd_ref[0, c:c + FFN_CHUNK, :])
    return y


def _merge_kernel(x_ref, oa_ref, ob_ref, oc_ref, gc_ref, on_ref, wo_ref, nf_ref, wu_ref, wd_ref,
                  y_ref):
    y_ref[...] = _merge_block(x_ref[...], oa_ref[...], ob_ref[...], oc_ref[...], gc_ref[...],
                              on_ref[...], wo_ref, nf_ref[...], wu_ref, wd_ref)


def _merge(x2d, oa, ob, oc, gc, onorm, wo16, nf, wu16, wd16, *, layer, tm):
    m, d = x2d.shape
    assert m % tm == 0
    row = lambda w: pl.BlockSpec((tm, w), lambda i: (i, 0))
    full = _resident
    return pl.pallas_call(
        _merge_kernel,
        grid=(m // tm,),
        in_specs=[row(d), row(W_ATT), row(W_ATT), row(W_C), row(W_C),
                  full(onorm), _resident_layer(wo16, layer), full(nf),
                  _resident_layer(wu16, layer), _resident_layer(wd16, layer)],
        out_specs=row(d),
        out_shape=jax.ShapeDtypeStruct((m, d), F32),
        compiler_params=_cparams(1),
        name="merge_ffn",
    )(x2d, oa, ob, oc, gc, onorm, wo16, nf, wu16, wd16)


def kernel(x_prompt, x_sample, cache_a_k, cache_a_v, cache_b_k, cache_b_v, state_c, norm_mix, w_in, qnorm_a, knorm_a, qnorm_b, knorm_b, rel_bias_b, lower_bounds, onorm_c, w_o, norm_ffn, w_up, w_down):
    depth = w_in.shape[0]
    bp, tp, d = x_prompt.shape
    bs, ts, _ = x_sample.shape
    past = cache_a_k.shape[2]
    band_rows_p = min(N_PREV_CHUNKS * CHUNK, tp)
    assert bp == 1

    xp = x_prompt.reshape(bp * tp, d)
    xs = x_sample.reshape(bs * ts, d)
    time_minor = lambda c: jnp.transpose(c, (0, 1, 3, 4, 2)).reshape(depth, bs, W_ATT, c.shape[2])
    cak, cav, cbk, cbv = (time_minor(c) for c in (cache_a_k, cache_a_v, cache_b_k, cache_b_v))
    heads_of = lambda a_t, n: jnp.transpose(
        a_t.reshape(N_HEADS, D_HEAD, n, a_t.shape[1] // n), (2, 3, 0, 1))
    lbounds = lower_bounds.astype(F32)
    w16, wo16, wu16, wd16 = (w.astype(BF16) for w in (w_in, w_o, w_up, w_down))
    zero_state = jnp.zeros((bp, N_HEADS, DK_C, DK_C), F32)

    tm_p = min(512, tp)
    tq_band = min(256, tp)
    tr_p = min(256, tp)

    outs_p = [[] for _ in range(5)]
    outs_s = [[] for _ in range(5)]
    carried = ()
    for l in range(depth):
        nm = norm_mix[l].reshape(1, d)
        nf = norm_ffn[l].reshape(1, d)
        tile_h = lambda g: jnp.tile(g.reshape(1, D_HEAD), (1, N_HEADS))
        qna, kna, qnb, knb = (tile_h(g[l]) for g in (qnorm_a, knorm_a, qnorm_b, knorm_b))
        onorm = onorm_c[l].reshape(1, DK_C)
        table = rel_bias_b[l].astype(F32)

        (qa, ka_all, va_all, ka16_t, va16, qb, kb_t, vb_t, kb16_t, vb16, qc, kc, lf, ic, gc) = _proj(
            xp, nm, w16, qna, kna, qnb, knb, lbounds, layer=l, tm=tm_p, time_minor=True,
            carried=carried)
        carried = (ka_all, va_all)
        oa = _sb_prompt(qa, ka16_t, va16, tq=min(SB_TQ, tp))
        ob = _band_prompt(qb, kb16_t, vb16, table, tq=tq_band)
        oc, s_p = _hgrn(qc, kc, lf, ic, zero_state, nb=bp, t=tp, tr=tr_p)
        xp = _merge(xp, oa, ob, oc, gc, onorm, wo16, nf, wu16, wd16, layer=l, tm=tm_p)
        outs_p[2].append(heads_of(kb_t, bp)[:, tp - band_rows_p:])
        outs_p[3].append(heads_of(vb_t, bp)[:, tp - band_rows_p:])
        outs_p[4].append(s_p)

        head = lambda a, n: a.reshape(n, -1, N_HEADS, D_HEAD)
        (qa, ka, va, ka16, va16, qb, kb, vb, kb16, vb16, qc, kc, lf, ic, gc) = _proj(
            xs, nm, w16, qna, kna, qnb, knb, lbounds, layer=l, tm=bs * ts, time_minor=False)
        oa = _sb_sample(qa, ka16, va16, cak, cav, layer=l, nb=bs, ts=ts)
        ob = _band_sample(qb, kb16, vb16, cbk, cbv, table, layer=l, nb=bs, ts=ts, past=past)
        oc, s_s = _hgrn(qc, kc, lf, ic, state_c[l].astype(F32), nb=bs, t=ts, tr=ts)
        xs = _merge(xs, oa, ob, oc, gc, onorm, wo16, nf, wu16, wd16, layer=l, tm=bs * ts)
        outs_s[0].append(head(ka, bs))
        outs_s[1].append(head(va, bs))
        outs_s[2].append(head(kb, bs))
        outs_s[3].append(head(vb, bs))
        outs_s[4].append(s_s)

    stack = lambda xs_: jnp.stack(xs_)
    all_heads_of = lambda a: jnp.transpose(
        a.reshape(depth, N_HEADS, D_HEAD, bp, tp), (0, 3, 4, 1, 2))
    ka_all, va_all = carried
    return (xp.reshape(bp, tp, d), xs.reshape(bs, ts, d),
            all_heads_of(ka_all), all_heads_of(va_all),
            stack(outs_p[2]), stack(outs_p[3]), stack(outs_p[4]),
            stack(outs_s[0]), stack(outs_s[1]), stack(outs_s[2]), stack(outs_s[3]), stack(outs_s[4]))
```

```python
import functools
import math

import jax
import jax.numpy as jnp
from jax import lax
from jax.experimental import pallas as pl
from jax.experimental.pallas import tpu as pltpu

F32 = jnp.float32
BF16 = jnp.bfloat16

D_HEAD = 64
N_HEADS = 4
W_ATT = N_HEADS * D_HEAD
DK_C = 128
W_C = N_HEADS * DK_C
CHUNK = 64
N_PREV_CHUNKS = 8
REL_CLIP = 128
N_REL = 2 * REL_CLIP + 1
EPS = 1e-6
NEG_BIG = -1e30
LB_FLOOR = 1e-30
SB_UNDERFLOW = 88.0
PROJ_PART = 256
FFN_CHUNK = 1024
SB_TK = 128
SB_TQ = 256
HGRN_DIAG = 8
VMEM_LIMIT = 56 * 1024 * 1024


def _cparams(n_axes):
    return pltpu.CompilerParams(dimension_semantics=("arbitrary",) * n_axes,
                                vmem_limit_bytes=VMEM_LIMIT)


def _resident(a):
    zeros = (0,) * a.ndim
    return pl.BlockSpec(a.shape, lambda *_: zeros, pipeline_mode=pl.Buffered(1))


def _resident_layer(a, layer):
    index = (layer,) + (0,) * (a.ndim - 1)
    return pl.BlockSpec((1,) + a.shape[1:], lambda *_: index, pipeline_mode=pl.Buffered(1))


def _split3(x):
    h1 = x.astype(BF16)
    r1 = x - h1.astype(F32)
    h2 = r1.astype(BF16)
    h3 = (r1 - h2.astype(F32)).astype(BF16)
    return h1, h2, h3


def _dot(a, b):
    return jnp.dot(a, b, preferred_element_type=F32)


def _dot_nt(a, b):
    return lax.dot_general(a, b, (((1,), (1,)), ((), ())), preferred_element_type=F32)


def _dot_tn(a, b):
    return lax.dot_general(a, b, (((0,), (0,)), ((), ())), preferred_element_type=F32)


def _iota(shape, dim):
    return lax.broadcasted_iota(jnp.int32, shape, dim)


def _proj_kernel(x_ref, nm_ref, w_ref, qna_ref, kna_ref, qnb_ref, knb_ref, lb_ref, *rest,
                 layer, time_minor):
    (qa_ref, ka_ref, va_ref, ka16_ref, va16_ref, qb_ref, kb_ref, vb_ref, kb16_ref, vb16_ref,
     qc_ref, kc_ref, lf_ref, ic_ref, gc_ref) = rest[-15:]
    if time_minor:
        for earlier_ref, all_ref in zip(rest[:-15], (ka_ref, va_ref)):
            all_ref[0:layer] = earlier_ref[...]
        ka_ref, va_ref = ka_ref.at[layer], va_ref.at[layer]
    x = x_ref[...]
    ms = jnp.mean(x * x, axis=-1, keepdims=True)
    xn = (x * lax.rsqrt(ms + EPS) * nm_ref[...]).astype(BF16)

    tm = x.shape[0]
    part = PROJ_PART if tm % PROJ_PART == 0 else tm
    projected = [_dot(xn[r0:r0 + part], w_ref[0]) for r0 in range(0, tm, part)]

    same_head = (_iota((W_ATT, W_ATT), 0) // D_HEAD) == (_iota((W_ATT, W_ATT), 1) // D_HEAD)
    seg_mean = jnp.where(same_head, 1.0 / D_HEAD, 0.0).astype(BF16)
    inv_sqrt_d = 1.0 / math.sqrt(D_HEAD)

    lbr = lb_ref[...]
    e = jnp.exp(lbr - jnp.max(lbr, axis=0, keepdims=True))
    sm = e / jnp.sum(e, axis=0, keepdims=True)
    lb = jnp.sum(sm[0:layer + 1], axis=0, keepdims=True) - sm[0:1]
    log_lb = jnp.log(jnp.maximum(lb, LB_FLOOR))
    log_1m_lb = jnp.log1p(-lb)

    def head_norm(p, g_ref):
        m = _dot((p * p).astype(BF16), seg_mean)
        return p * lax.rsqrt(m + EPS) * g_ref[...]

    for n, p in enumerate(projected):
        rows = slice(n * part, (n + 1) * part)

        def put_k(k, k_ref, k16_ref):
            if time_minor:
                k_ref[:, rows] = k.T
                k16_ref[:, rows] = k.T.astype(BF16)
            else:
                k_ref[rows, :] = k
                k16_ref[rows, :] = k.astype(BF16)

        def put_v(v, v_ref, v16_ref):
            if time_minor:
                v_ref[:, rows] = v.T
            else:
                v_ref[rows, :] = v
            v16_ref[rows, :] = v.astype(BF16)

        o = 0
        qa_ref[rows, :] = (head_norm(p[:, o:o + W_ATT], qna_ref) * inv_sqrt_d).astype(BF16)
        o += W_ATT
        put_k(head_norm(p[:, o:o + W_ATT], kna_ref), ka_ref, ka16_ref)
        o += W_ATT
        put_v(p[:, o:o + W_ATT], va_ref, va16_ref)
        o += W_ATT
        qb_ref[rows, :] = (head_norm(p[:, o:o + W_ATT], qnb_ref) * inv_sqrt_d).astype(BF16)
        o += W_ATT
        put_k(head_norm(p[:, o:o + W_ATT], knb_ref), kb_ref, kb16_ref)
        o += W_ATT
        put_v(p[:, o:o + W_ATT], vb_ref, vb16_ref)
        o += W_ATT
        qc_ref[rows, :] = p[:, o:o + W_C] * (DK_C ** -0.5)
        o += W_C
        f_raw = p[:, o:o + W_C]
        o += W_C
        log_sig = jnp.minimum(f_raw, 0.0) - jnp.log1p(jnp.exp(-jnp.abs(f_raw)))
        b = log_1m_lb + log_sig
        lf_ref[rows, :] = jnp.maximum(log_lb, b) + jnp.log1p(jnp.exp(-jnp.abs(log_lb - b)))
        kc_ref[rows, :] = (1.0 - lb) * (1.0 / (1.0 + jnp.exp(f_raw)))
        ic_ref[rows, :] = p[:, o:o + W_C]
        o += W_C
        gc_ref[rows, :] = p[:, o:o + W_C]


def _proj(x2d, nm, w16, qna, kna, qnb, knb, lower_bounds, *, layer, tm, time_minor, carried=()):
    m, d = x2d.shape
    assert m % tm == 0 and len(carried) == (2 if time_minor and layer > 0 else 0)
    full = _resident
    row = lambda w, dt: (pl.BlockSpec((tm, w), lambda i: (i, 0)), jax.ShapeDtypeStruct((m, w), dt))
    col = lambda w, dt: (pl.BlockSpec((w, tm), lambda i: (0, i)), jax.ShapeDtypeStruct((w, m), dt))
    layers = lambda n, w: pl.BlockSpec((n, w, tm), lambda i: (0, 0, i))
    layered = lambda w, dt: (layers(layer + 1, w), jax.ShapeDtypeStruct((layer + 1, w, m), dt))
    kv = col if time_minor else row
    kv_a = layered if time_minor else row
    att_a = [row(W_ATT, BF16), kv_a(W_ATT, F32), kv_a(W_ATT, F32), kv(W_ATT, BF16), row(W_ATT, BF16)]
    att_b = [row(W_ATT, BF16), kv(W_ATT, F32), kv(W_ATT, F32), kv(W_ATT, BF16), row(W_ATT, BF16)]
    outs = att_a + att_b + [row(W_C, F32)] * 5
    return pl.pallas_call(
        functools.partial(_proj_kernel, layer=layer, time_minor=time_minor),
        grid=(m // tm,),
        in_specs=[row(d, F32)[0], full(nm), _resident_layer(w16, layer), full(qna), full(kna),
                  full(qnb), full(knb),
                  full(lower_bounds)] + [layers(layer, W_ATT)] * len(carried),
        out_specs=[spec for spec, _ in outs],
        out_shape=[shape for _, shape in outs],
        compiler_params=_cparams(1),
        name="proj",
    )(x2d, nm, w16, qna, kna, qnb, knb, lower_bounds, *carried)


def _head_masks(width):
    lane_head = _iota((1, width), 1) // D_HEAD
    return [lane_head == h for h in range(N_HEADS)]


def _stack_heads(x, head_lane):
    return jnp.concatenate([jnp.where(m, x, jnp.zeros_like(x)) for m in head_lane], axis=0)


def _stack_heads_t(x_t, n):
    row_head = _iota((W_ATT, 1), 0) // D_HEAD
    return jnp.concatenate(
        [jnp.where(row_head == h, x_t, jnp.zeros_like(x_t)) for h in range(N_HEADS)], axis=1)


def _stack_groups(x, head_lane, gq):
    return jnp.concatenate(
        [_stack_heads(x[g:g + gq], head_lane) for g in range(0, x.shape[0], gq)], axis=0)


def _sb_block(z, weigh_values, mask, tri, carry_ref, acc_ref, gq):
    tk = z.shape[1]
    t = jnp.log(1.0 + jnp.exp2(jnp.abs(z) * -math.log2(math.e)))
    log_1m = jnp.minimum(-z, 0.0) - t
    log_beta = log_1m + z
    if mask is not None:
        log_1m = jnp.where(mask, log_1m, 0.0)
    hi = log_1m.astype(BF16)
    lo = (log_1m - hi.astype(F32)).astype(BF16)
    later = _dot(jnp.concatenate([hi, lo], axis=1), tri)
    carry = carry_ref[...]
    w = jnp.exp(log_beta + later + carry[:, :tk])
    if mask is not None:
        w = jnp.where(mask, w, 0.0)
    w = w.astype(BF16)
    blocks = [w[n * gq:(n + 1) * gq] for n in range(z.shape[0] // gq)]
    w_heads = jnp.concatenate(
        [jnp.concatenate(blocks[g:g + N_HEADS], axis=1) for g in range(0, len(blocks), N_HEADS)],
        axis=0)
    acc_ref[...] += weigh_values(w_heads)
    carry_ref[...] = carry + jnp.sum(log_1m, axis=-1, keepdims=True)


def _sb_live(carry_ref):
    return (jnp.max(carry_ref[...]) > -SB_UNDERFLOW).astype(jnp.int32)


def _strict_upper(n):
    return jnp.where(_iota((2 * n, n), 0) % n > _iota((2 * n, n), 1), 1.0, 0.0).astype(BF16)


def _sb_prompt_kernel(q_ref, kt_ref, v_ref, o_ref, carry_ref, acc_ref, *, tq):
    i = pl.program_id(0)
    head_lane = _head_masks(W_ATT)
    gq = SB_TK
    n_groups = tq // gq
    q_stack = _stack_groups(q_ref[...], head_lane, gq)
    tri = _strict_upper(SB_TK)
    carry_ref[...] = jnp.zeros_like(carry_ref)
    acc_ref[...] = jnp.zeros_like(acc_ref)

    def add_block(j, first_group, mask):
        r0 = pl.multiple_of(j * SB_TK, SB_TK)
        s0 = first_group * N_HEADS * gq
        z = _dot(q_stack[s0:], kt_ref[:, pl.ds(r0, SB_TK)])
        v_stack = _stack_heads(v_ref[pl.ds(r0, SB_TK), :], head_lane)
        _sb_block(z, lambda w: _dot(w, v_stack), mask, tri,
                  carry_ref.at[pl.ds(s0, z.shape[0])],
                  acc_ref.at[pl.ds(first_group * gq, (n_groups - first_group) * gq)], gq)

    for g in reversed(range(n_groups)):
        shape = ((n_groups - g) * N_HEADS * gq, SB_TK)
        row = _iota(shape, 0)
        add_block(i * n_groups + g, g, jnp.logical_or(_iota(shape, 1) < row % gq, row >= N_HEADS * gq))
    add_block(jnp.maximum(i * n_groups - 1, 0), 0,
              jnp.broadcast_to(i > 0, (n_groups * N_HEADS * gq, SB_TK)))

    def cond(c):
        j, live = c
        return jnp.logical_and(j >= 0, live > 0)

    def body(c):
        j, _ = c
        add_block(j, 0, None)
        return j - 1, _sb_live(carry_ref)

    lax.while_loop(cond, body, (i * n_groups - 2, _sb_live(carry_ref)))
    o_ref[...] = acc_ref[...].astype(o_ref.dtype)


def _sb_prompt(q16, kt16, v16, *, tq):
    t = q16.shape[0]
    assert t % tq == 0 and tq % SB_TK == 0
    return pl.pallas_call(
        functools.partial(_sb_prompt_kernel, tq=tq),
        grid=(t // tq,),
        in_specs=[pl.BlockSpec((tq, W_ATT), lambda i: (i, 0)),
                  _resident(kt16), _resident(v16)],
        out_specs=pl.BlockSpec((tq, W_ATT), lambda i: (i, 0)),
        out_shape=jax.ShapeDtypeStruct((t, W_ATT), BF16),
        scratch_shapes=[pltpu.VMEM((N_HEADS * tq, SB_TK), F32), pltpu.VMEM((tq, W_ATT), F32)],
        compiler_params=_cparams(1),
        name="sb_prompt",
    )(q16, kt16, v16)


def _sb_sample_kernel(q_ref, kn_ref, vn_ref, ckt_ref, cvt_ref, o_ref, carry_ref, acc_ref, *, ts, past):
    head_lane = _head_masks(W_ATT)
    q_stack = _stack_heads(q_ref[...], head_lane)
    carry_ref[...] = jnp.zeros_like(carry_ref)
    acc_ref[...] = jnp.zeros_like(acc_ref)
    shape = (N_HEADS * ts, ts)
    mask = _iota(shape, 1) < _iota(shape, 0) % ts
    v_new = _stack_heads(vn_ref[...], head_lane)
    _sb_block(_dot_nt(q_stack, kn_ref[...]), lambda w: _dot(w, v_new), mask, _strict_upper(ts),
              carry_ref, acc_ref, ts)
    tri = _strict_upper(SB_TK)

    def cond(c):
        j, live = c
        return jnp.logical_and(j >= 0, live > 0)

    def body(c):
        j, _ = c
        r0 = pl.multiple_of(j * SB_TK, SB_TK)
        z = _dot(q_stack, ckt_ref[0, 0, :, pl.ds(r0, SB_TK)].astype(BF16))
        vt_stack = _stack_heads_t(cvt_ref[0, 0, :, pl.ds(r0, SB_TK)].astype(BF16), SB_TK)
        _sb_block(z, lambda w: _dot_nt(w, vt_stack), None, tri, carry_ref, acc_ref, ts)
        return j - 1, _sb_live(carry_ref)

    lax.while_loop(cond, body, (past // SB_TK - 1, _sb_live(carry_ref)))
    o_ref[...] = acc_ref[...].astype(o_ref.dtype)


def _sb_sample(q16, kn16, vn16, cache_kt, cache_vt, *, layer, nb, ts):
    past = cache_kt.shape[3]
    assert past % SB_TK == 0
    row = pl.BlockSpec((ts, W_ATT), lambda b: (b, 0))
    cache = pl.BlockSpec((1, 1, W_ATT, past), lambda b: (layer, b, 0, 0))
    return pl.pallas_call(
        functools.partial(_sb_sample_kernel, ts=ts, past=past),
        grid=(nb,),
        in_specs=[row, row, row, cache, cache],
        out_specs=row,
        out_shape=jax.ShapeDtypeStruct((nb * ts, W_ATT), BF16),
        scratch_shapes=[pltpu.VMEM((N_HEADS * ts, SB_TK), F32), pltpu.VMEM((ts, W_ATT), F32)],
        compiler_params=_cparams(1),
        name="sb_sample",
    )(q16, kn16, vn16, cache_kt, cache_vt)


def _rel_bias(table_ref, h, dist):
    idx = jnp.clip(dist, -REL_CLIP, REL_CLIP) + REL_CLIP

    def body(r, b):
        return jnp.where(idx == r, table_ref[h, r], b)

    return lax.fori_loop(0, N_REL, body, jnp.zeros(dist.shape, F32))


def _row_reduce(x, combine, reduce):
    width = x.shape[1]
    acc = x[:, :128]
    for c in range(128, width, 128):
        acc = combine(acc, x[:, c:c + 128])
    return reduce(acc, axis=-1, keepdims=True)


def _toeplitz_bias(table_ref, h, rows, back, width):
    lane = _iota((8, width), 1)
    by_diag = _rel_bias(table_ref, h, back + rows - lane)
    tiled = jnp.concatenate([by_diag] * (rows // 8), axis=0)
    return pltpu.roll(tiled, width - rows, 1, stride=1, stride_axis=0)


BAND_TQ = 128
BAND_BACK = N_PREV_CHUNKS * CHUNK
BAND_WIN = BAND_BACK + BAND_TQ


def _band_prompt_kernel(table_ref, q_ref, k0_ref, k1_ref, k2_ref, v0_ref, v1_ref, v2_ref,
                        o_ref, bias_ref, *, tq):
    i = pl.program_id(0)
    n_kb = 3
    rows = _iota((BAND_TQ, BAND_WIN), 0)
    cols = _iota((BAND_TQ, BAND_WIN), 1)

    @pl.when(i == 0)
    def _():
        q_chunk = rows // CHUNK + N_PREV_CHUNKS
        k_chunk = cols // CHUNK
        in_band = jnp.logical_and(k_chunk >= q_chunk - N_PREV_CHUNKS, k_chunk <= q_chunk)
        for h in range(N_HEADS):
            bias = _toeplitz_bias(table_ref, h, BAND_TQ, BAND_BACK, BAND_WIN + BAND_TQ)
            bias_ref[h] = jnp.where(in_band, bias[:, :BAND_WIN], NEG_BIG)

    q = q_ref[...]
    head_lane = _head_masks(W_ATT)
    k_t = jnp.concatenate([k0_ref[...], k1_ref[...], k2_ref[...]], axis=1)
    vcat = jnp.concatenate([v0_ref[...], v1_ref[...], v2_ref[...]], axis=0)
    parts = [p * BAND_TQ for p in range(tq // BAND_TQ)]
    starts = [r0 + (n_kb - 1) * tq - BAND_BACK for r0 in parts]
    scores = []
    for p in range(len(parts)):
        qp = q[parts[p]:parts[p] + BAND_TQ]
        k_win = k_t[:, starts[p]:starts[p] + BAND_WIN]
        scores.append([_dot(jnp.where(head_lane[h], qp, jnp.zeros_like(qp)), k_win)
                       for h in range(N_HEADS)])
    for p in range(len(parts)):
        exists = (i - (n_kb - 1)) * tq + starts[p] + cols >= 0
        weights = []
        inv_den = []
        for h, s in enumerate(scores[p]):
            s = jnp.where(exists, s + bias_ref[h], NEG_BIG)
            e = jnp.exp(s - _row_reduce(s, jnp.maximum, jnp.max))
            weights.append(e.astype(BF16))
            inv_den.append(1.0 / _row_reduce(e, jnp.add, jnp.sum))
        v_win = vcat[starts[p]:starts[p] + BAND_WIN]
        out = jnp.zeros((BAND_TQ, W_ATT), F32)
        for h in range(N_HEADS):
            out = jnp.where(head_lane[h], _dot(weights[h], v_win) * inv_den[h], out)
        o_ref[parts[p]:parts[p] + BAND_TQ, :] = out.astype(o_ref.dtype)


def _band_prompt(q16, kt16, v16, table, *, tq):
    t = q16.shape[0]
    assert t % tq == 0 and tq % BAND_TQ == 0 and 2 * tq >= BAND_BACK
    blk = lambda back: pl.BlockSpec((tq, W_ATT), lambda i: (jnp.maximum(i - back, 0), 0))
    blk_t = lambda back: pl.BlockSpec((W_ATT, tq), lambda i: (0, jnp.maximum(i - back, 0)))
    return pl.pallas_call(
        functools.partial(_band_prompt_kernel, tq=tq),
        grid=(t // tq,),
        in_specs=[pl.BlockSpec(memory_space=pltpu.SMEM),
                  blk(0), blk_t(2), blk_t(1), blk_t(0), blk(2), blk(1), blk(0)],
        out_specs=blk(0),
        out_shape=jax.ShapeDtypeStruct((t, W_ATT), BF16),
        scratch_shapes=[pltpu.VMEM((N_HEADS, BAND_TQ, BAND_WIN), F32)],
        compiler_params=_cparams(1),
        name="band_prompt",
    )(table, q16, kt16, kt16, kt16, v16, v16, v16)


def _band_sample_kernel(table_ref, q_ref, kn_ref, vn_ref, ckt_ref, cvt_ref, o_ref, bias_ref, *, ts, past):
    lb = ckt_ref.shape[3]

    @pl.when(pl.program_id(0) == 0)
    def _():
        for h in range(N_HEADS):
            bias_ref[h] = _toeplitz_bias(table_ref, h, ts, lb, bias_ref.shape[2])

    q = q_ref[...]
    head_lane = _head_masks(W_ATT)
    kc_t = ckt_ref[0, 0].astype(BF16)
    vc_t = cvt_ref[0, 0].astype(BF16)
    kn = kn_ref[...]
    vn = vn_ref[...]
    q_pos_c = past + _iota((ts, lb), 0)
    k_pos_c = past - lb + _iota((ts, lb), 1)
    q_pos_n = past + _iota((ts, ts), 0)
    k_pos_n = past + _iota((ts, ts), 1)

    def allowed(q_pos, k_pos):
        qc, kc_ = q_pos // CHUNK, k_pos // CHUNK
        return jnp.logical_and(kc_ >= qc - N_PREV_CHUNKS, kc_ <= qc)

    ok_c = allowed(q_pos_c, k_pos_c)
    ok_n = allowed(q_pos_n, k_pos_n)
    out = jnp.zeros((ts, W_ATT), F32)
    for h in range(N_HEADS):
        qh = jnp.where(head_lane[h], q, jnp.zeros_like(q))
        bias = bias_ref[h]
        s_c = jnp.where(ok_c, _dot(qh, kc_t) + bias[:, :lb], NEG_BIG)
        s_n = jnp.where(ok_n, _dot_nt(qh, kn) + bias[:, lb:lb + ts], NEG_BIG)
        m = jnp.maximum(jnp.max(s_c, axis=-1, keepdims=True), jnp.max(s_n, axis=-1, keepdims=True))
        e_c = jnp.exp(s_c - m)
        e_n = jnp.exp(s_n - m)
        den = jnp.sum(e_c, axis=-1, keepdims=True) + jnp.sum(e_n, axis=-1, keepdims=True)
        pv = (_dot_nt(e_c.astype(BF16), vc_t) + _dot(e_n.astype(BF16), vn)) / den
        out = jnp.where(head_lane[h], pv, out)
    o_ref[...] = out.astype(o_ref.dtype)


def _band_sample(q16, kn16, vn16, cache_kt, cache_vt, table, *, layer, nb, ts, past):
    lb = cache_kt.shape[3]
    row = pl.BlockSpec((ts, W_ATT), lambda b: (b, 0))
    cache = pl.BlockSpec((1, 1, W_ATT, lb), lambda b: (layer, b, 0, 0))
    return pl.pallas_call(
        functools.partial(_band_sample_kernel, ts=ts, past=past),
        grid=(nb,),
        in_specs=[pl.BlockSpec(memory_space=pltpu.SMEM), row, row, row, cache, cache],
        out_specs=row,
        out_shape=jax.ShapeDtypeStruct((nb * ts, W_ATT), BF16),
        scratch_shapes=[pltpu.VMEM((N_HEADS, ts, pl.cdiv(lb + 2 * ts, 128) * 128), F32)],
        compiler_params=_cparams(1),
        name="band_sample",
    )(table, q16, kn16, vn16, cache_kt, cache_vt)


def _hgrn_spans(q, k, lf, v, st_ref):
    tr = q.shape[0]
    heads = [slice(h * DK_C, (h + 1) * DK_C) for h in range(N_HEADS)]
    tri = jnp.where(_iota((tr, tr), 0) >= _iota((tr, tr), 1), 1.0, 0.0).astype(BF16)
    f1, f2, f3 = _split3(lf)
    g = (_dot(tri, f1) + _dot(tri, f2) + _dot(tri, f3)) * math.log2(math.e)
    v16 = v.astype(BF16)

    g_end = g[tr - 1:tr, :]
    qt = (q * jnp.exp2(g)).astype(BF16)
    kd = (k * jnp.exp2(g_end - g)).astype(BF16)
    dec = jnp.exp2(g_end)
    states = [st_ref[h] for h in range(N_HEADS)]
    o_heads = [_dot_nt(qt[:, hs], st.astype(BF16)) for hs, st in zip(heads, states)]
    grown = [_dot_tn(v16[:, hs], kd[:, hs]) for hs in heads]
    for h, hs in enumerate(heads):
        st_ref[h] = states[h] * dec[:, hs] + grown[h]
    o = jnp.concatenate(o_heads, axis=1)

    sizes = []
    h = HGRN_DIAG
    while h < tr:
        sizes.append(h)
        h *= 2
    split = lambda x, h: x.reshape(tr // (2 * h), 2 * h, W_C)
    operands = []
    for h in sizes:
        g3, q3, k3, v3 = split(g, h), split(q, h), split(k, h), split(v16, h)
        c = g3[:, h - 1:h, :]
        qe = (q3[:, h:] * jnp.exp2(g3[:, h:] - c)).reshape(tr // 2, W_C).astype(BF16)
        ke = (k3[:, :h] * jnp.exp2(c - g3[:, :h])).reshape(tr // 2, W_C).astype(BF16)
        operands.append((qe, ke, v3[:, :h].reshape(tr // 2, W_C)))
    pair = [[_dot_nt(qe[:, hs], ke[:, hs]) for hs in heads] for qe, ke, _ in operands]
    span_of_row = _iota((tr // 2, tr // 2), 0)
    span_of_col = _iota((tr // 2, tr // 2), 1)
    for n, h in enumerate(sizes):
        if 2 * h < tr:
            same_span = (span_of_row // h) == (span_of_col // h)
            pair[n] = [jnp.where(same_span, p, 0.0) for p in pair[n]]
    added = [jnp.concatenate([_dot(p.astype(BF16), vl[:, hs]) for p, hs in zip(pair[n], heads)],
                             axis=1) for n, (_, _, vl) in enumerate(operands)]
    for n, h in enumerate(sizes):
        o3 = split(o, h)
        upper = o3[:, h:] + added[n].reshape(tr // (2 * h), h, W_C)
        o = jnp.concatenate([o3[:, :h], upper], axis=1).reshape(tr, W_C)
    return o, g


def _hgrn_diagonal(g8, q8, k8, v8):
    row_id = _iota((HGRN_DIAG, 1), 0)
    acc = [jnp.zeros((HGRN_DIAG, DK_C), F32) for _ in range(N_HEADS)]
    for j in range(HGRN_DIAG):
        gj, kj, vj = g8[j:j + 1, :], k8[j:j + 1, :], v8[j:j + 1, :]
        p = q8 * (kj * jnp.exp2(g8 - gj))
        for h in range(N_HEADS):
            hs = slice(h * DK_C, (h + 1) * DK_C)
            a = jnp.sum(p[:, hs], axis=-1, keepdims=True)
            acc[h] = acc[h] + jnp.where(row_id >= j, a, 0.0) * vj[:, hs]
    return jnp.concatenate(acc, axis=1)


def _hgrn_block(q, k, lf, v, st_ref):
    o, g = _hgrn_spans(q, k, lf, v, st_ref)
    spans = [slice(r0, r0 + HGRN_DIAG) for r0 in range(0, q.shape[0], HGRN_DIAG)]
    return o + jnp.concatenate([_hgrn_diagonal(g[s], q[s], k[s], v[s]) for s in spans], axis=0)


def _hgrn_kernel(q_ref, k_ref, lf_ref, v_ref, s0_ref, o_ref, sout_ref, st_ref):
    r = pl.program_id(1)

    @pl.when(r == 0)
    def _():
        for h in range(N_HEADS):
            st_ref[h] = s0_ref[0, h].T

    o_ref[...] = _hgrn_block(q_ref[...], k_ref[...], lf_ref[...], v_ref[...], st_ref)

    @pl.when(r == pl.num_programs(1) - 1)
    def _():
        for h in range(N_HEADS):
            sout_ref[0, h] = st_ref[h].T


def _hgrn(qc, kc, lf, ic, s0, *, nb, t, tr):
    assert t % tr == 0 and tr % (2 * HGRN_DIAG) == 0 and tr & (tr - 1) == 0
    nr = t // tr
    row = pl.BlockSpec((tr, W_C), lambda b, r: (b * nr + r, 0))
    state = pl.BlockSpec((1, N_HEADS, DK_C, DK_C), lambda b, r: (b, 0, 0, 0))
    return pl.pallas_call(
        _hgrn_kernel,
        grid=(nb, nr),
        in_specs=[row, row, row, row, state],
        out_specs=[row, state],
        out_shape=[jax.ShapeDtypeStruct((nb * t, W_C), F32),
                   jax.ShapeDtypeStruct((nb, N_HEADS, DK_C, DK_C), F32)],
        scratch_shapes=[pltpu.VMEM((N_HEADS, DK_C, DK_C), F32)],
        compiler_params=_cparams(2),
        name="hgrn",
    )(qc, kc, lf, ic, s0)


def _mix_block(x, oa, ob, oc, gc, on, wo_ref, nf):
    parts = []
    for h in range(N_HEADS):
        och = oc[:, h * DK_C:(h + 1) * DK_C]
        ms = jnp.mean(och * och, axis=-1, keepdims=True)
        parts.append(och * lax.rsqrt(ms + EPS) * on)
    ocn = jnp.concatenate(parts, axis=-1) * (gc * (1.0 / (1.0 + jnp.exp(-gc))))
    mixed = jnp.concatenate([oa, ob, ocn.astype(BF16)], axis=-1)
    h_res = x + _dot(mixed, wo_ref[0])
    ms = jnp.mean(h_res * h_res, axis=-1, keepdims=True)
    return h_res, (h_res * lax.rsqrt(ms + EPS) * nf).astype(BF16)


def _ffn_block(h_res, hn, wu_ref, wd_ref, chunk=FFN_CHUNK, after_chunk=None):
    y = h_res
    for n, c in enumerate(range(0, wu_ref.shape[2], chunk)):
        u = jnp.maximum(_dot(hn, wu_ref[0, :, c:c + chunk]), 0.0)
        y = y + _dot((u * u).astype(BF16), wd_ref[0, c:c + chunk, :])
        if after_chunk is not None:
            after_chunk(n, y)
    return y


def _merge_block(x, oa, ob, oc, gc, on, wo_ref, nf, wu_ref, wd_ref):
    h_res, hn = _mix_block(x, oa, ob, oc, gc, on, wo_ref, nf)
    return _ffn_block(h_res, hn, wu_ref, wd_ref)


def _merge_kernel(x_ref, oa_ref, ob_ref, oc_ref, gc_ref, on_ref, wo_ref, nf_ref, wu_ref, wd_ref,
                  y_ref):
    y_ref[...] = _merge_block(x_ref[...], oa_ref[...], ob_ref[...], oc_ref[...], gc_ref[...],
                              on_ref[...], wo_ref, nf_ref[...], wu_ref, wd_ref)


def _merge(x2d, oa, ob, oc, gc, onorm, wo16, nf, wu16, wd16, *, layer, tm):
    m, d = x2d.shape
    assert m % tm == 0
    row = lambda w: pl.BlockSpec((tm, w), lambda i: (i, 0))
    full = _resident
    return pl.pallas_call(
        _merge_kernel,
        grid=(m // tm,),
        in_specs=[row(d), row(W_ATT), row(W_ATT), row(W_C), row(W_C),
                  full(onorm), _resident_layer(wo16, layer), full(nf),
                  _resident_layer(wu16, layer), _resident_layer(wd16, layer)],
        out_specs=row(d),
        out_shape=jax.ShapeDtypeStruct((m, d), F32),
        compiler_params=_cparams(1),
        name="merge_ffn",
    )(x2d, oa, ob, oc, gc, onorm, wo16, nf, wu16, wd16)


HGRN_MERGE_FFN_CHUNK = 512


def _hgrn_merge_kernel(q_ref, k_ref, lf_ref, v_ref, s0_ref,
                       x_ref, oa_ref, ob_ref, gc_ref, on_ref, wo_ref, nf_ref, wu_ref, wd_ref,
                       y_ref, sout_ref, st_ref, res_ref, hn_ref):
    r = pl.program_id(0)
    n_blocks = pl.num_programs(0) - 1

    @pl.when(r == 0)
    def _():
        for h in range(N_HEADS):
            st_ref[h] = s0_ref[0, h].T
        res_ref[...] = jnp.zeros_like(res_ref)
        hn_ref[...] = jnp.zeros_like(hn_ref)

    q, k, v = q_ref[...], k_ref[...], v_ref[...]
    o, g = _hgrn_spans(q, k, lf_ref[...], v, st_ref)
    tr = q.shape[0]
    n_chunks = wu_ref.shape[2] // HGRN_MERGE_FFN_CHUNK
    spans_per_chunk = tr // HGRN_DIAG // n_chunks
    never = r < 0
    diagonal = []

    def spans_after(n, y_so_far):
        for s in range(n * spans_per_chunk, (n + 1) * spans_per_chunk):
            rows = slice(s * HGRN_DIAG, (s + 1) * HGRN_DIAG)
            q8 = jnp.where(never, y_so_far[:HGRN_DIAG, :W_C], q[rows])
            diagonal.append(_hgrn_diagonal(g[rows], q8, k[rows], v[rows]))

    y_ref[...] = _ffn_block(res_ref[...], hn_ref[...], wu_ref, wd_ref, HGRN_MERGE_FFN_CHUNK,
                            spans_after)
    oc = o + jnp.concatenate(diagonal, axis=0)
    res_ref[...], hn_ref[...] = _mix_block(x_ref[...], oa_ref[...], ob_ref[...], oc, gc_ref[...],
                                           on_ref[...], wo_ref, nf_ref[...])

    @pl.when(r == n_blocks - 1)
    def _():
        for h in range(N_HEADS):
            sout_ref[0, h] = st_ref[h].T


def _hgrn_merge(qc, kc, lf, ic, s0, x2d, oa, ob, gc, onorm, wo16, nf, wu16, wd16, *, layer, tr):
    m, d = x2d.shape
    assert m % tr == 0 and tr % (2 * HGRN_DIAG) == 0 and tr & (tr - 1) == 0
    nr = m // tr
    ahead = lambda w: pl.BlockSpec((tr, w), lambda r: (jnp.minimum(r, nr - 1), 0))
    behind = lambda w: pl.BlockSpec((tr, w), lambda r: (jnp.maximum(r - 1, 0), 0))
    state = pl.BlockSpec((1, N_HEADS, DK_C, DK_C), lambda r: (0, 0, 0, 0))
    full = _resident
    return pl.pallas_call(
        _hgrn_merge_kernel,
        grid=(nr + 1,),
        in_specs=[ahead(W_C), ahead(W_C), ahead(W_C), ahead(W_C), state,
                  ahead(d), ahead(W_ATT), ahead(W_ATT), ahead(W_C),
                  full(onorm), _resident_layer(wo16, layer), full(nf),
                  _resident_layer(wu16, layer), _resident_layer(wd16, layer)],
        out_specs=[behind(d), state],
        out_shape=[jax.ShapeDtypeStruct((m, d), F32),
                   jax.ShapeDtypeStruct((1, N_HEADS, DK_C, DK_C), F32)],
        scratch_shapes=[pltpu.VMEM((N_HEADS, DK_C, DK_C), F32),
                        pltpu.VMEM((tr, d), F32), pltpu.VMEM((tr, d), BF16)],
        compiler_params=_cparams(1),
        name="hgrn_merge",
    )(qc, kc, lf, ic, s0, x2d, oa, ob, gc, onorm, wo16, nf, wu16, wd16)


def kernel(x_prompt, x_sample, cache_a_k, cache_a_v, cache_b_k, cache_b_v, state_c, norm_mix, w_in, qnorm_a, knorm_a, qnorm_b, knorm_b, rel_bias_b, lower_bounds, onorm_c, w_o, norm_ffn, w_up, w_down):
    depth = w_in.shape[0]
    bp, tp, d = x_prompt.shape
    bs, ts, _ = x_sample.shape
    past = cache_a_k.shape[2]
    band_rows_p = min(N_PREV_CHUNKS * CHUNK, tp)
    assert bp == 1

    xp = x_prompt.reshape(bp * tp, d)
    xs = x_sample.reshape(bs * ts, d)
    time_minor = lambda c: jnp.transpose(c, (0, 1, 3, 4, 2)).reshape(depth, bs, W_ATT, c.shape[2])
    cak, cav, cbk, cbv = (time_minor(c) for c in (cache_a_k, cache_a_v, cache_b_k, cache_b_v))
    heads_of = lambda a_t, n: jnp.transpose(
        a_t.reshape(N_HEADS, D_HEAD, n, a_t.shape[1] // n), (2, 3, 0, 1))
    lbounds = lower_bounds.astype(F32)
    w16, wo16, wu16, wd16 = (w.astype(BF16) for w in (w_in, w_o, w_up, w_down))
    zero_state = jnp.zeros((bp, N_HEADS, DK_C, DK_C), F32)

    tm_p = min(512, tp)
    tq_band = min(256, tp)
    tr_p = min(256, tp)

    outs_p = [[] for _ in range(5)]
    outs_s = [[] for _ in range(5)]
    carried = ()
    for l in range(depth):
        nm = norm_mix[l].reshape(1, d)
        nf = norm_ffn[l].reshape(1, d)
        tile_h = lambda g: jnp.tile(g.reshape(1, D_HEAD), (1, N_HEADS))
        qna, kna, qnb, knb = (tile_h(g[l]) for g in (qnorm_a, knorm_a, qnorm_b, knorm_b))
        onorm = onorm_c[l].reshape(1, DK_C)
        table = rel_bias_b[l].astype(F32)

        (qa, ka_all, va_all, ka16_t, va16, qb, kb_t, vb_t, kb16_t, vb16, qc, kc, lf, ic, gc) = _proj(
            xp, nm, w16, qna, kna, qnb, knb, lbounds, layer=l, tm=tm_p, time_minor=True,
            carried=carried)
        carried = (ka_all, va_all)
        oa = _sb_prompt(qa, ka16_t, va16, tq=min(SB_TQ, tp))
        ob = _band_prompt(qb, kb16_t, vb16, table, tq=tq_band)
        xp, s_p = _hgrn_merge(qc, kc, lf, ic, zero_state, xp, oa, ob, gc, onorm, wo16, nf, wu16,
                              wd16, layer=l, tr=tr_p)
        outs_p[2].append(heads_of(kb_t, bp)[:, tp - band_rows_p:])
        outs_p[3].append(heads_of(vb_t, bp)[:, tp - band_rows_p:])
        outs_p[4].append(s_p)

        head = lambda a, n: a.reshape(n, -1, N_HEADS, D_HEAD)
        (qa, ka, va, ka16, va16, qb, kb, vb, kb16, vb16, qc, kc, lf, ic, gc) = _proj(
            xs, nm, w16, qna, kna, qnb, knb, lbounds, layer=l, tm=bs * ts, time_minor=False)
        oa = _sb_sample(qa, ka16, va16, cak, cav, layer=l, nb=bs, ts=ts)
        ob = _band_sample(qb, kb16, vb16, cbk, cbv, table, layer=l, nb=bs, ts=ts, past=past)
        oc, s_s = _hgrn(qc, kc, lf, ic, state_c[l].astype(F32), nb=bs, t=ts, tr=ts)
        xs = _merge(xs, oa, ob, oc, gc, onorm, wo16, nf, wu16, wd16, layer=l, tm=bs * ts)
        outs_s[0].append(head(ka, bs))
        outs_s[1].append(head(va, bs))
        outs_s[2].append(head(kb, bs))
        outs_s[3].append(head(vb, bs))
        outs_s[4].append(s_s)

    stack = lambda xs_: jnp.stack(xs_)
    all_heads_of = lambda a: jnp.transpose(
        a.reshape(depth, N_HEADS, D_HEAD, bp, tp), (0, 3, 4, 1, 2))
    ka_all, va_all = carried
    return (xp.reshape(bp, tp, d), xs.reshape(bs, ts, d),
            all_heads_of(ka_all), all_heads_of(va_all),
            stack(outs_p[2]), stack(outs_p[3]), stack(outs_p[4]),
            stack(outs_s[0]), stack(outs_s[1]), stack(outs_s[2]), stack(outs_s[3]), stack(outs_s[4]))
```

```python
import functools
import math

import jax
import jax.numpy as jnp
from jax import lax
from jax.experimental import pallas as pl
from jax.experimental.pallas import tpu as pltpu

F32 = jnp.float32
BF16 = jnp.bfloat16

D_HEAD = 64
N_HEADS = 4
W_ATT = N_HEADS * D_HEAD
DK_C = 128
W_C = N_HEADS * DK_C
CHUNK = 64
N_PREV_CHUNKS = 8
REL_CLIP = 128
N_REL = 2 * REL_CLIP + 1
EPS = 1e-6
NEG_BIG = -1e30
LB_FLOOR = 1e-30
SB_UNDERFLOW = 88.0
PROJ_PART = 256
FFN_CHUNK = 1024
SB_TK = 128
SB_TQ = 256
HGRN_DIAG = 8
VMEM_LIMIT = 56 * 1024 * 1024


def _cparams(n_axes):
    return pltpu.CompilerParams(dimension_semantics=("arbitrary",) * n_axes,
                                vmem_limit_bytes=VMEM_LIMIT)


def _resident(a):
    zeros = (0,) * a.ndim
    return pl.BlockSpec(a.shape, lambda *_: zeros, pipeline_mode=pl.Buffered(1))


def _resident_layer(a, layer):
    index = (layer,) + (0,) * (a.ndim - 1)
    return pl.BlockSpec((1,) + a.shape[1:], lambda *_: index, pipeline_mode=pl.Buffered(1))


def _split3(x):
    h1 = x.astype(BF16)
    r1 = x - h1.astype(F32)
    h2 = r1.astype(BF16)
    h3 = (r1 - h2.astype(F32)).astype(BF16)
    return h1, h2, h3


def _dot(a, b):
    return jnp.dot(a, b, preferred_element_type=F32)


def _dot_nt(a, b):
    return lax.dot_general(a, b, (((1,), (1,)), ((), ())), preferred_element_type=F32)


def _dot_tn(a, b):
    return lax.dot_general(a, b, (((0,), (0,)), ((), ())), preferred_element_type=F32)


def _iota(shape, dim):
    return lax.broadcasted_iota(jnp.int32, shape, dim)


def _proj_kernel(x_ref, nm_ref, w_ref, qna_ref, kna_ref, qnb_ref, knb_ref, lb_ref, *rest,
                 layer, time_minor):
    (qa_ref, ka_ref, va_ref, ka16_ref, va16_ref, qb_ref, kb_ref, vb_ref, kb16_ref, vb16_ref,
     qc_ref, kc_ref, lf_ref, ic_ref, gc_ref) = rest[-15:]
    if time_minor:
        for earlier_ref, all_ref in zip(rest[:-15], (ka_ref, va_ref)):
            all_ref[0:layer] = earlier_ref[...]
        ka_ref, va_ref = ka_ref.at[layer], va_ref.at[layer]
    x = x_ref[...]
    ms = jnp.mean(x * x, axis=-1, keepdims=True)
    xn = (x * lax.rsqrt(ms + EPS) * nm_ref[...]).astype(BF16)

    tm = x.shape[0]
    part = PROJ_PART if tm % PROJ_PART == 0 else tm
    projected = [_dot(xn[r0:r0 + part], w_ref[0]) for r0 in range(0, tm, part)]

    same_head = (_iota((W_ATT, W_ATT), 0) // D_HEAD) == (_iota((W_ATT, W_ATT), 1) // D_HEAD)
    seg_mean = jnp.where(same_head, 1.0 / D_HEAD, 0.0).astype(BF16)
    inv_sqrt_d = 1.0 / math.sqrt(D_HEAD)

    lbr = lb_ref[...]
    e = jnp.exp(lbr - jnp.max(lbr, axis=0, keepdims=True))
    sm = e / jnp.sum(e, axis=0, keepdims=True)
    lb = jnp.sum(sm[0:layer + 1], axis=0, keepdims=True) - sm[0:1]
    log_lb = jnp.log(jnp.maximum(lb, LB_FLOOR))
    log_1m_lb = jnp.log1p(-lb)

    def head_norm(p, g_ref):
        m = _dot((p * p).astype(BF16), seg_mean)
        return p * lax.rsqrt(m + EPS) * g_ref[...]

    for n, p in enumerate(projected):
        rows = slice(n * part, (n + 1) * part)

        def put_k(k, k_ref, k16_ref):
            if time_minor:
                k_ref[:, rows] = k.T
                k16_ref[:, rows] = k.T.astype(BF16)
            else:
                k_ref[rows, :] = k
                k16_ref[rows, :] = k.astype(BF16)

        def put_v(v, v_ref, v16_ref):
            if time_minor:
                v_ref[:, rows] = v.T
            else:
                v_ref[rows, :] = v
            v16_ref[rows, :] = v.astype(BF16)

        o = 0
        qa_ref[rows, :] = (head_norm(p[:, o:o + W_ATT], qna_ref) * inv_sqrt_d).astype(BF16)
        o += W_ATT
        put_k(head_norm(p[:, o:o + W_ATT], kna_ref), ka_ref, ka16_ref)
        o += W_ATT
        put_v(p[:, o:o + W_ATT], va_ref, va16_ref)
        o += W_ATT
        qb_ref[rows, :] = (head_norm(p[:, o:o + W_ATT], qnb_ref) * inv_sqrt_d).astype(BF16)
        o += W_ATT
        put_k(head_norm(p[:, o:o + W_ATT], knb_ref), kb_ref, kb16_ref)
        o += W_ATT
        put_v(p[:, o:o + W_ATT], vb_ref, vb16_ref)
        o += W_ATT
        qc_ref[rows, :] = p[:, o:o + W_C] * (DK_C ** -0.5)
        o += W_C
        f_raw = p[:, o:o + W_C]
        o += W_C
        log_sig = jnp.minimum(f_raw, 0.0) - jnp.log1p(jnp.exp(-jnp.abs(f_raw)))
        b = log_1m_lb + log_sig
        lf_ref[rows, :] = jnp.maximum(log_lb, b) + jnp.log1p(jnp.exp(-jnp.abs(log_lb - b)))
        kc_ref[rows, :] = (1.0 - lb) * (1.0 / (1.0 + jnp.exp(f_raw)))
        ic_ref[rows, :] = p[:, o:o + W_C]
        o += W_C
        gc_ref[rows, :] = p[:, o:o + W_C]


def _proj(x2d, nm, w16, qna, kna, qnb, knb, lower_bounds, *, layer, tm, time_minor, carried=()):
    m, d = x2d.shape
    assert m % tm == 0 and len(carried) == (2 if time_minor and layer > 0 else 0)
    full = _resident
    row = lambda w, dt: (pl.BlockSpec((tm, w), lambda i: (i, 0)), jax.ShapeDtypeStruct((m, w), dt))
    col = lambda w, dt: (pl.BlockSpec((w, tm), lambda i: (0, i)), jax.ShapeDtypeStruct((w, m), dt))
    layers = lambda n, w: pl.BlockSpec((n, w, tm), lambda i: (0, 0, i))
    layered = lambda w, dt: (layers(layer + 1, w), jax.ShapeDtypeStruct((layer + 1, w, m), dt))
    kv = col if time_minor else row
    kv_a = layered if time_minor else row
    att_a = [row(W_ATT, BF16), kv_a(W_ATT, F32), kv_a(W_ATT, F32), kv(W_ATT, BF16), row(W_ATT, BF16)]
    att_b = [row(W_ATT, BF16), kv(W_ATT, F32), kv(W_ATT, F32), kv(W_ATT, BF16), row(W_ATT, BF16)]
    outs = att_a + att_b + [row(W_C, F32)] * 5
    return pl.pallas_call(
        functools.partial(_proj_kernel, layer=layer, time_minor=time_minor),
        grid=(m // tm,),
        in_specs=[row(d, F32)[0], full(nm), _resident_layer(w16, layer), full(qna), full(kna),
                  full(qnb), full(knb),
                  full(lower_bounds)] + [layers(layer, W_ATT)] * len(carried),
        out_specs=[spec for spec, _ in outs],
        out_shape=[shape for _, shape in outs],
        compiler_params=_cparams(1),
        name="proj",
    )(x2d, nm, w16, qna, kna, qnb, knb, lower_bounds, *carried)


def _head_masks(width):
    lane_head = _iota((1, width), 1) // D_HEAD
    return [lane_head == h for h in range(N_HEADS)]


def _stack_heads(x, head_lane):
    return jnp.concatenate([jnp.where(m, x, jnp.zeros_like(x)) for m in head_lane], axis=0)


def _stack_heads_t(x_t, n):
    row_head = _iota((W_ATT, 1), 0) // D_HEAD
    return jnp.concatenate(
        [jnp.where(row_head == h, x_t, jnp.zeros_like(x_t)) for h in range(N_HEADS)], axis=1)


def _stack_groups(x, head_lane, gq):
    return jnp.concatenate(
        [_stack_heads(x[g:g + gq], head_lane) for g in range(0, x.shape[0], gq)], axis=0)


def _sb_block(z, weigh_values, mask, tri, carry_ref, acc_ref, gq):
    tk = z.shape[1]
    t = jnp.log(1.0 + jnp.exp2(jnp.abs(z) * -math.log2(math.e)))
    log_1m = jnp.minimum(-z, 0.0) - t
    log_beta = log_1m + z
    if mask is not None:
        log_1m = jnp.where(mask, log_1m, 0.0)
    hi = log_1m.astype(BF16)
    lo = (log_1m - hi.astype(F32)).astype(BF16)
    later = _dot(jnp.concatenate([hi, lo], axis=1), tri)
    carry = carry_ref[...]
    w = jnp.exp(log_beta + later + carry[:, :tk])
    if mask is not None:
        w = jnp.where(mask, w, 0.0)
    w = w.astype(BF16)
    blocks = [w[n * gq:(n + 1) * gq] for n in range(z.shape[0] // gq)]
    w_heads = jnp.concatenate(
        [jnp.concatenate(blocks[g:g + N_HEADS], axis=1) for g in range(0, len(blocks), N_HEADS)],
        axis=0)
    acc_ref[...] += weigh_values(w_heads)
    carry_ref[...] = carry + jnp.sum(log_1m, axis=-1, keepdims=True)


def _sb_live(carry_ref):
    return (jnp.max(carry_ref[...]) > -SB_UNDERFLOW).astype(jnp.int32)


def _strict_upper(n):
    return jnp.where(_iota((2 * n, n), 0) % n > _iota((2 * n, n), 1), 1.0, 0.0).astype(BF16)


def _sb_prompt_kernel(q_ref, kt_ref, v_ref, o_ref, carry_ref, acc_ref, *, tq):
    i = pl.program_id(0)
    head_lane = _head_masks(W_ATT)
    gq = SB_TK
    n_groups = tq // gq
    q_stack = _stack_groups(q_ref[...], head_lane, gq)
    tri = _strict_upper(SB_TK)
    carry_ref[...] = jnp.zeros_like(carry_ref)
    acc_ref[...] = jnp.zeros_like(acc_ref)

    def add_block(j, first_group, mask):
        r0 = pl.multiple_of(j * SB_TK, SB_TK)
        s0 = first_group * N_HEADS * gq
        z = _dot(q_stack[s0:], kt_ref[:, pl.ds(r0, SB_TK)])
        v_stack = _stack_heads(v_ref[pl.ds(r0, SB_TK), :], head_lane)
        _sb_block(z, lambda w: _dot(w, v_stack), mask, tri,
                  carry_ref.at[pl.ds(s0, z.shape[0])],
                  acc_ref.at[pl.ds(first_group * gq, (n_groups - first_group) * gq)], gq)

    for g in reversed(range(n_groups)):
        shape = ((n_groups - g) * N_HEADS * gq, SB_TK)
        row = _iota(shape, 0)
        add_block(i * n_groups + g, g, jnp.logical_or(_iota(shape, 1) < row % gq, row >= N_HEADS * gq))
    add_block(jnp.maximum(i * n_groups - 1, 0), 0,
              jnp.broadcast_to(i > 0, (n_groups * N_HEADS * gq, SB_TK)))

    def cond(c):
        j, live = c
        return jnp.logical_and(j >= 0, live > 0)

    def body(c):
        j, _ = c
        add_block(j, 0, None)
        return j - 1, _sb_live(carry_ref)

    lax.while_loop(cond, body, (i * n_groups - 2, _sb_live(carry_ref)))
    o_ref[...] = acc_ref[...].astype(o_ref.dtype)


def _sb_prompt(q16, kt16, v16, *, tq):
    t = q16.shape[0]
    assert t % tq == 0 and tq % SB_TK == 0
    return pl.pallas_call(
        functools.partial(_sb_prompt_kernel, tq=tq),
        grid=(t // tq,),
        in_specs=[pl.BlockSpec((tq, W_ATT), lambda i: (i, 0)),
                  _resident(kt16), _resident(v16)],
        out_specs=pl.BlockSpec((tq, W_ATT), lambda i: (i, 0)),
        out_shape=jax.ShapeDtypeStruct((t, W_ATT), BF16),
        scratch_shapes=[pltpu.VMEM((N_HEADS * tq, SB_TK), F32), pltpu.VMEM((tq, W_ATT), F32)],
        compiler_params=_cparams(1),
        name="sb_prompt",
    )(q16, kt16, v16)


def _sb_sample_kernel(q_ref, kn_ref, vn_ref, ckt_ref, cvt_ref, o_ref, carry_ref, acc_ref, *, ts, past):
    head_lane = _head_masks(W_ATT)
    q_stack = _stack_heads(q_ref[...], head_lane)
    carry_ref[...] = jnp.zeros_like(carry_ref)
    acc_ref[...] = jnp.zeros_like(acc_ref)
    shape = (N_HEADS * ts, ts)
    mask = _iota(shape, 1) < _iota(shape, 0) % ts
    v_new = _stack_heads(vn_ref[...], head_lane)
    _sb_block(_dot_nt(q_stack, kn_ref[...]), lambda w: _dot(w, v_new), mask, _strict_upper(ts),
              carry_ref, acc_ref, ts)
    tri = _strict_upper(SB_TK)

    def cond(c):
        j, live = c
        return jnp.logical_and(j >= 0, live > 0)

    def body(c):
        j, _ = c
        r0 = pl.multiple_of(j * SB_TK, SB_TK)
        z = _dot(q_stack, ckt_ref[0, 0, :, pl.ds(r0, SB_TK)].astype(BF16))
        vt_stack = _stack_heads_t(cvt_ref[0, 0, :, pl.ds(r0, SB_TK)].astype(BF16), SB_TK)
        _sb_block(z, lambda w: _dot_nt(w, vt_stack), None, tri, carry_ref, acc_ref, ts)
        return j - 1, _sb_live(carry_ref)

    lax.while_loop(cond, body, (past // SB_TK - 1, _sb_live(carry_ref)))
    o_ref[...] = acc_ref[...].astype(o_ref.dtype)


def _sb_sample(q16, kn16, vn16, cache_kt, cache_vt, *, layer, nb, ts):
    past = cache_kt.shape[3]
    assert past % SB_TK == 0
    row = pl.BlockSpec((ts, W_ATT), lambda b: (b, 0))
    cache = pl.BlockSpec((1, 1, W_ATT, past), lambda b: (layer, b, 0, 0))
    return pl.pallas_call(
        functools.partial(_sb_sample_kernel, ts=ts, past=past),
        grid=(nb,),
        in_specs=[row, row, row, cache, cache],
        out_specs=row,
        out_shape=jax.ShapeDtypeStruct((nb * ts, W_ATT), BF16),
        scratch_shapes=[pltpu.VMEM((N_HEADS * ts, SB_TK), F32), pltpu.VMEM((ts, W_ATT), F32)],
        compiler_params=_cparams(1),
        name="sb_sample",
    )(q16, kn16, vn16, cache_kt, cache_vt)


def _rel_bias(table_ref, h, dist):
    idx = jnp.clip(dist, -REL_CLIP, REL_CLIP) + REL_CLIP

    def body(r, b):
        return jnp.where(idx == r, table_ref[h, r], b)

    return lax.fori_loop(0, N_REL, body, jnp.zeros(dist.shape, F32))


def _row_reduce(x, combine, reduce):
    width = x.shape[1]
    acc = x[:, :128]
    for c in range(128, width, 128):
        acc = combine(acc, x[:, c:c + 128])
    return reduce(acc, axis=-1, keepdims=True)


def _toeplitz_bias(table_ref, h, rows, back, width):
    lane = _iota((8, width), 1)
    by_diag = _rel_bias(table_ref, h, back + rows - lane)
    tiled = jnp.concatenate([by_diag] * (rows // 8), axis=0)
    return pltpu.roll(tiled, width - rows, 1, stride=1, stride_axis=0)


BAND_TQ = 128
BAND_BACK = N_PREV_CHUNKS * CHUNK
BAND_WIN = BAND_BACK + BAND_TQ


def _band_prompt_kernel(table_ref, q_ref, k0_ref, k1_ref, k2_ref, v0_ref, v1_ref, v2_ref,
                        o_ref, bias_ref, *, tq):
    i = pl.program_id(0)
    n_kb = 3
    rows = _iota((BAND_TQ, BAND_WIN), 0)
    cols = _iota((BAND_TQ, BAND_WIN), 1)

    @pl.when(i == 0)
    def _():
        q_chunk = rows // CHUNK + N_PREV_CHUNKS
        k_chunk = cols // CHUNK
        in_band = jnp.logical_and(k_chunk >= q_chunk - N_PREV_CHUNKS, k_chunk <= q_chunk)
        for h in range(N_HEADS):
            bias = _toeplitz_bias(table_ref, h, BAND_TQ, BAND_BACK, BAND_WIN + BAND_TQ)
            bias_ref[h] = jnp.where(in_band, bias[:, :BAND_WIN], NEG_BIG)

    q = q_ref[...]
    head_lane = _head_masks(W_ATT)
    k_t = jnp.concatenate([k0_ref[...], k1_ref[...], k2_ref[...]], axis=1)
    vcat = jnp.concatenate([v0_ref[...], v1_ref[...], v2_ref[...]], axis=0)
    parts = [p * BAND_TQ for p in range(tq // BAND_TQ)]
    starts = [r0 + (n_kb - 1) * tq - BAND_BACK for r0 in parts]
    scores = []
    for p in range(len(parts)):
        qp = q[parts[p]:parts[p] + BAND_TQ]
        k_win = k_t[:, starts[p]:starts[p] + BAND_WIN]
        scores.append([_dot(jnp.where(head_lane[h], qp, jnp.zeros_like(qp)), k_win)
                       for h in range(N_HEADS)])
    for p in range(len(parts)):
        exists = (i - (n_kb - 1)) * tq + starts[p] + cols >= 0
        weights = []
        inv_den = []
        for h, s in enumerate(scores[p]):
            s = jnp.where(exists, s + bias_ref[h], NEG_BIG)
            e = jnp.exp(s - _row_reduce(s, jnp.maximum, jnp.max))
            weights.append(e.astype(BF16))
            inv_den.append(1.0 / _row_reduce(e, jnp.add, jnp.sum))
        v_win = vcat[starts[p]:starts[p] + BAND_WIN]
        out = jnp.zeros((BAND_TQ, W_ATT), F32)
        for h in range(N_HEADS):
            out = jnp.where(head_lane[h], _dot(weights[h], v_win) * inv_den[h], out)
        o_ref[parts[p]:parts[p] + BAND_TQ, :] = out.astype(o_ref.dtype)


def _band_prompt(q16, kt16, v16, table, *, tq):
    t = q16.shape[0]
    assert t % tq == 0 and tq % BAND_TQ == 0 and 2 * tq >= BAND_BACK
    blk = lambda back: pl.BlockSpec((tq, W_ATT), lambda i: (jnp.maximum(i - back, 0), 0))
    blk_t = lambda back: pl.BlockSpec((W_ATT, tq), lambda i: (0, jnp.maximum(i - back, 0)))
    return pl.pallas_call(
        functools.partial(_band_prompt_kernel, tq=tq),
        grid=(t // tq,),
        in_specs=[pl.BlockSpec(memory_space=pltpu.SMEM),
                  blk(0), blk_t(2), blk_t(1), blk_t(0), blk(2), blk(1), blk(0)],
        out_specs=blk(0),
        out_shape=jax.ShapeDtypeStruct((t, W_ATT), BF16),
        scratch_shapes=[pltpu.VMEM((N_HEADS, BAND_TQ, BAND_WIN), F32)],
        compiler_params=_cparams(1),
        name="band_prompt",
    )(table, q16, kt16, kt16, kt16, v16, v16, v16)


def _band_sample_kernel(table_ref, q_ref, kn_ref, vn_ref, ckt_ref, cvt_ref, o_ref, bias_ref, *, ts, past):
    lb = ckt_ref.shape[3]

    @pl.when(pl.program_id(0) == 0)
    def _():
        for h in range(N_HEADS):
            bias_ref[h] = _toeplitz_bias(table_ref, h, ts, lb, bias_ref.shape[2])

    q = q_ref[...]
    head_lane = _head_masks(W_ATT)
    kc_t = ckt_ref[0, 0].astype(BF16)
    vc_t = cvt_ref[0, 0].astype(BF16)
    kn = kn_ref[...]
    vn = vn_ref[...]
    q_pos_c = past + _iota((ts, lb), 0)
    k_pos_c = past - lb + _iota((ts, lb), 1)
    q_pos_n = past + _iota((ts, ts), 0)
    k_pos_n = past + _iota((ts, ts), 1)

    def allowed(q_pos, k_pos):
        qc, kc_ = q_pos // CHUNK, k_pos // CHUNK
        return jnp.logical_and(kc_ >= qc - N_PREV_CHUNKS, kc_ <= qc)

    ok_c = allowed(q_pos_c, k_pos_c)
    ok_n = allowed(q_pos_n, k_pos_n)
    out = jnp.zeros((ts, W_ATT), F32)
    for h in range(N_HEADS):
        qh = jnp.where(head_lane[h], q, jnp.zeros_like(q))
        bias = bias_ref[h]
        s_c = jnp.where(ok_c, _dot(qh, kc_t) + bias[:, :lb], NEG_BIG)
        s_n = jnp.where(ok_n, _dot_nt(qh, kn) + bias[:, lb:lb + ts], NEG_BIG)
        m = jnp.maximum(jnp.max(s_c, axis=-1, keepdims=True), jnp.max(s_n, axis=-1, keepdims=True))
        e_c = jnp.exp(s_c - m)
        e_n = jnp.exp(s_n - m)
        den = jnp.sum(e_c, axis=-1, keepdims=True) + jnp.sum(e_n, axis=-1, keepdims=True)
        pv = (_dot_nt(e_c.astype(BF16), vc_t) + _dot(e_n.astype(BF16), vn)) / den
        out = jnp.where(head_lane[h], pv, out)
    o_ref[...] = out.astype(o_ref.dtype)


def _band_sample(q16, kn16, vn16, cache_kt, cache_vt, table, *, layer, nb, ts, past):
    lb = cache_kt.shape[3]
    row = pl.BlockSpec((ts, W_ATT), lambda b: (b, 0))
    cache = pl.BlockSpec((1, 1, W_ATT, lb), lambda b: (layer, b, 0, 0))
    return pl.pallas_call(
        functools.partial(_band_sample_kernel, ts=ts, past=past),
        grid=(nb,),
        in_specs=[pl.BlockSpec(memory_space=pltpu.SMEM), row, row, row, cache, cache],
        out_specs=row,
        out_shape=jax.ShapeDtypeStruct((nb * ts, W_ATT), BF16),
        scratch_shapes=[pltpu.VMEM((N_HEADS, ts, pl.cdiv(lb + 2 * ts, 128) * 128), F32)],
        compiler_params=_cparams(1),
        name="band_sample",
    )(table, q16, kn16, vn16, cache_kt, cache_vt)


def _hgrn_spans(q, k, lf, v, st_ref):
    tr = q.shape[0]
    heads = [slice(h * DK_C, (h + 1) * DK_C) for h in range(N_HEADS)]
    tri = jnp.where(_iota((tr, tr), 0) >= _iota((tr, tr), 1), 1.0, 0.0).astype(BF16)
    f1, f2, f3 = _split3(lf)
    g = (_dot(tri, f1) + _dot(tri, f2) + _dot(tri, f3)) * math.log2(math.e)
    v16 = v.astype(BF16)

    g_end = g[tr - 1:tr, :]
    qt = (q * jnp.exp2(g)).astype(BF16)
    kd = (k * jnp.exp2(g_end - g)).astype(BF16)
    dec = jnp.exp2(g_end)
    states = [st_ref[h] for h in range(N_HEADS)]
    o_heads = [_dot_nt(qt[:, hs], st.astype(BF16)) for hs, st in zip(heads, states)]
    grown = [_dot_tn(v16[:, hs], kd[:, hs]) for hs in heads]
    for h, hs in enumerate(heads):
        st_ref[h] = states[h] * dec[:, hs] + grown[h]
    o = jnp.concatenate(o_heads, axis=1)

    sizes = []
    h = HGRN_DIAG
    while h < tr:
        sizes.append(h)
        h *= 2
    split = lambda x, h: x.reshape(tr // (2 * h), 2 * h, W_C)
    operands = []
    for h in sizes:
        g3, q3, k3, v3 = split(g, h), split(q, h), split(k, h), split(v16, h)
        c = g3[:, h - 1:h, :]
        qe = (q3[:, h:] * jnp.exp2(g3[:, h:] - c)).reshape(tr // 2, W_C).astype(BF16)
        ke = (k3[:, :h] * jnp.exp2(c - g3[:, :h])).reshape(tr // 2, W_C).astype(BF16)
        operands.append((qe, ke, v3[:, :h].reshape(tr // 2, W_C)))
    pair = [[_dot_nt(qe[:, hs], ke[:, hs]) for hs in heads] for qe, ke, _ in operands]
    span_of_row = _iota((tr // 2, tr // 2), 0)
    span_of_col = _iota((tr // 2, tr // 2), 1)
    for n, h in enumerate(sizes):
        if 2 * h < tr:
            same_span = (span_of_row // h) == (span_of_col // h)
            pair[n] = [jnp.where(same_span, p, 0.0) for p in pair[n]]
    added = [jnp.concatenate([_dot(p.astype(BF16), vl[:, hs]) for p, hs in zip(pair[n], heads)],
                             axis=1) for n, (_, _, vl) in enumerate(operands)]
    for n, h in enumerate(sizes):
        o3 = split(o, h)
        upper = o3[:, h:] + added[n].reshape(tr // (2 * h), h, W_C)
        o = jnp.concatenate([o3[:, :h], upper], axis=1).reshape(tr, W_C)
    return o, g


def _hgrn_diagonal(g2, q2, k2, v2):
    n = HGRN_DIAG
    lane = _iota((n, DK_C), 1)
    weights = [[jnp.zeros((n, DK_C), F32) for _ in range(N_HEADS)] for _ in range(2)]
    for s in range(2):
        g8, q8, k8 = g2[s * n:(s + 1) * n], q2[s * n:(s + 1) * n], k2[s * n:(s + 1) * n]
        for j in range(n):
            p = q8 * (k8[j:j + 1, :] * jnp.exp2(g8 - g8[j:j + 1, :]))
            for h in range(N_HEADS):
                a = jnp.sum(p[:, h * DK_C:(h + 1) * DK_C], axis=-1, keepdims=True)
                weights[s][h] = jnp.where(lane == s * n + j, a, weights[s][h])
    row = _iota((2 * n, DK_C), 0)
    col = _iota((2 * n, DK_C), 1)
    keep = jnp.logical_and(row >= col, row // n == col // n)
    v16 = v2.astype(BF16)
    out = []
    for h in range(N_HEADS):
        a = jnp.where(keep, jnp.concatenate([weights[0][h], weights[1][h]], axis=0), 0.0)
        out.append(_dot(a[:, :2 * n].astype(BF16), v16[:, h * DK_C:(h + 1) * DK_C]))
    return jnp.concatenate(out, axis=1)


def _hgrn_block(q, k, lf, v, st_ref):
    o, g = _hgrn_spans(q, k, lf, v, st_ref)
    pairs = [slice(r0, r0 + 2 * HGRN_DIAG) for r0 in range(0, q.shape[0], 2 * HGRN_DIAG)]
    return o + jnp.concatenate([_hgrn_diagonal(g[s], q[s], k[s], v[s]) for s in pairs], axis=0)


def _hgrn_kernel(q_ref, k_ref, lf_ref, v_ref, s0_ref, o_ref, sout_ref, st_ref):
    r = pl.program_id(1)

    @pl.when(r == 0)
    def _():
        for h in range(N_HEADS):
            st_ref[h] = s0_ref[0, h].T

    o_ref[...] = _hgrn_block(q_ref[...], k_ref[...], lf_ref[...], v_ref[...], st_ref)

    @pl.when(r == pl.num_programs(1) - 1)
    def _():
        for h in range(N_HEADS):
            sout_ref[0, h] = st_ref[h].T


def _hgrn(qc, kc, lf, ic, s0, *, nb, t, tr):
    assert t % tr == 0 and tr % (2 * HGRN_DIAG) == 0 and tr & (tr - 1) == 0
    nr = t // tr
    row = pl.BlockSpec((tr, W_C), lambda b, r: (b * nr + r, 0))
    state = pl.BlockSpec((1, N_HEADS, DK_C, DK_C), lambda b, r: (b, 0, 0, 0))
    return pl.pallas_call(
        _hgrn_kernel,
        grid=(nb, nr),
        in_specs=[row, row, row, row, state],
        out_specs=[row, state],
        out_shape=[jax.ShapeDtypeStruct((nb * t, W_C), F32),
                   jax.ShapeDtypeStruct((nb, N_HEADS, DK_C, DK_C), F32)],
        scratch_shapes=[pltpu.VMEM((N_HEADS, DK_C, DK_C), F32)],
        compiler_params=_cparams(2),
        name="hgrn",
    )(qc, kc, lf, ic, s0)


def _merge_block(x, oa, ob, oc, gc, on, wo_ref, nf, wu_ref, wd_ref):
    parts = []
    for h in range(N_HEADS):
        och = oc[:, h * DK_C:(h + 1) * DK_C]
        ms = jnp.mean(och * och, axis=-1, keepdims=True)
        parts.append(och * lax.rsqrt(ms + EPS) * on)
    ocn = jnp.concatenate(parts, axis=-1) * (gc * (1.0 / (1.0 + jnp.exp(-gc))))
    mixed = jnp.concatenate([oa, ob, ocn.astype(BF16)], axis=-1)
    h_res = x + _dot(mixed, wo_ref[0])
    ms = jnp.mean(h_res * h_res, axis=-1, keepdims=True)
    hn = (h_res * lax.rsqrt(ms + EPS) * nf).astype(BF16)
    y = h_res
    for c in range(0, wu_ref.shape[2], FFN_CHUNK):
        u = jnp.maximum(_dot(hn, wu_ref[0, :, c:c + FFN_CHUNK]), 0.0)
        y = y + _dot((u * u).astype(BF16), wd_ref[0, c:c + FFN_CHUNK, :])
    return y


def _merge_kernel(x_ref, oa_ref, ob_ref, oc_ref, gc_ref, on_ref, wo_ref, nf_ref, wu_ref, wd_ref,
                  y_ref):
    y_ref[...] = _merge_block(x_ref[...], oa_ref[...], ob_ref[...], oc_ref[...], gc_ref[...],
                              on_ref[...], wo_ref, nf_ref[...], wu_ref, wd_ref)


def _merge(x2d, oa, ob, oc, gc, onorm, wo16, nf, wu16, wd16, *, layer, tm):
    m, d = x2d.shape
    assert m % tm == 0
    row = lambda w: pl.BlockSpec((tm, w), lambda i: (i, 0))
    full = _resident
    return pl.pallas_call(
        _merge_kernel,
        grid=(m // tm,),
        in_specs=[row(d), row(W_ATT), row(W_ATT), row(W_C), row(W_C),
                  full(onorm), _resident_layer(wo16, layer), full(nf),
                  _resident_layer(wu16, layer), _resident_layer(wd16, layer)],
        out_specs=row(d),
        out_shape=jax.ShapeDtypeStruct((m, d), F32),
        compiler_params=_cparams(1),
        name="merge_ffn",
    )(x2d, oa, ob, oc, gc, onorm, wo16, nf, wu16, wd16)


def kernel(x_prompt, x_sample, cache_a_k, cache_a_v, cache_b_k, cache_b_v, state_c, norm_mix, w_in, qnorm_a, knorm_a, qnorm_b, knorm_b, rel_bias_b, lower_bounds, onorm_c, w_o, norm_ffn, w_up, w_down):
    depth = w_in.shape[0]
    bp, tp, d = x_prompt.shape
    bs, ts, _ = x_sample.shape
    past = cache_a_k.shape[2]
    band_rows_p = min(N_PREV_CHUNKS * CHUNK, tp)
    assert bp == 1

    xp = x_prompt.reshape(bp * tp, d)
    xs = x_sample.reshape(bs * ts, d)
    time_minor = lambda c: jnp.transpose(c, (0, 1, 3, 4, 2)).reshape(depth, bs, W_ATT, c.shape[2])
    cak, cav, cbk, cbv = (time_minor(c) for c in (cache_a_k, cache_a_v, cache_b_k, cache_b_v))
    heads_of = lambda a_t, n: jnp.transpose(
        a_t.reshape(N_HEADS, D_HEAD, n, a_t.shape[1] // n), (2, 3, 0, 1))
    lbounds = lower_bounds.astype(F32)
    w16, wo16, wu16, wd16 = (w.astype(BF16) for w in (w_in, w_o, w_up, w_down))
    zero_state = jnp.zeros((bp, N_HEADS, DK_C, DK_C), F32)

    tm_p = min(512, tp)
    tq_band = min(256, tp)
    tr_p = min(256, tp)

    outs_p = [[] for _ in range(5)]
    outs_s = [[] for _ in range(5)]
    carried = ()
    for l in range(depth):
        nm = norm_mix[l].reshape(1, d)
        nf = norm_ffn[l].reshape(1, d)
        tile_h = lambda g: jnp.tile(g.reshape(1, D_HEAD), (1, N_HEADS))
        qna, kna, qnb, knb = (tile_h(g[l]) for g in (qnorm_a, knorm_a, qnorm_b, knorm_b))
        onorm = onorm_c[l].reshape(1, DK_C)
        table = rel_bias_b[l].astype(F32)

        (qa, ka_all, va_all, ka16_t, va16, qb, kb_t, vb_t, kb16_t, vb16, qc, kc, lf, ic, gc) = _proj(
            xp, nm, w16, qna, kna, qnb, knb, lbounds, layer=l, tm=tm_p, time_minor=True,
            carried=carried)
        carried = (ka_all, va_all)
        oa = _sb_prompt(qa, ka16_t, va16, tq=min(SB_TQ, tp))
        ob = _band_prompt(qb, kb16_t, vb16, table, tq=tq_band)
        oc, s_p = _hgrn(qc, kc, lf, ic, zero_state, nb=bp, t=tp, tr=tr_p)
        xp = _merge(xp, oa, ob, oc, gc, onorm, wo16, nf, wu16, wd16, layer=l, tm=tm_p)
        outs_p[2].append(heads_of(kb_t, bp)[:, tp - band_rows_p:])
        outs_p[3].append(heads_of(vb_t, bp)[:, tp - band_rows_p:])
        outs_p[4].append(s_p)

        head = lambda a, n: a.reshape(n, -1, N_HEADS, D_HEAD)
        (qa, ka, va, ka16, va16, qb, kb, vb, kb16, vb16, qc, kc, lf, ic, gc) = _proj(
            xs, nm, w16, qna, kna, qnb, knb, lbounds, layer=l, tm=bs * ts, time_minor=False)
        oa = _sb_sample(qa, ka16, va16, cak, cav, layer=l, nb=bs, ts=ts)
        ob = _band_sample(qb, kb16, vb16, cbk, cbv, table, layer=l, nb=bs, ts=ts, past=past)
        oc, s_s = _hgrn(qc, kc, lf, ic, state_c[l].astype(F32), nb=bs, t=ts, tr=ts)
        xs = _merge(xs, oa, ob, oc, gc, onorm, wo16, nf, wu16, wd16, layer=l, tm=bs * ts)
        outs_s[0].append(head(ka, bs))
        outs_s[1].append(head(va, bs))
        outs_s[2].append(head(kb, bs))
        outs_s[3].append(head(vb, bs))
        outs_s[4].append(s_s)

    stack = lambda xs_: jnp.stack(xs_)
    all_heads_of = lambda a: jnp.transpose(
        a.reshape(depth, N_HEADS, D_HEAD, bp, tp), (0, 3, 4, 1, 2))
    ka_all, va_all = carried
    return (xp.reshape(bp, tp, d), xs.reshape(bs, ts, d),
            all_heads_of(ka_all), all_heads_of(va_all),
            stack(outs_p[2]), stack(outs_p[3]), stack(outs_p[4]),
            stack(outs_s[0]), stack(outs_s[1]), stack(outs_s[2]), stack(outs_s[3]), stack(outs_s[4]))
```

```python
import functools
import math

import jax
import jax.numpy as jnp
from jax import lax
from jax.experimental import pallas as pl
from jax.experimental.pallas import tpu as pltpu

F32 = jnp.float32
BF16 = jnp.bfloat16

D_HEAD = 64
N_HEADS = 4
W_ATT = N_HEADS * D_HEAD
DK_C = 128
W_C = N_HEADS * DK_C
CHUNK = 64
N_PREV_CHUNKS = 8
REL_CLIP = 128
N_REL = 2 * REL_CLIP + 1
EPS = 1e-6
NEG_BIG = -1e30
LB_FLOOR = 1e-30
LOG2E = math.log2(math.e)
SB_UNDERFLOW = 88.0
PROJ_PART = 256
FFN_CHUNK = 1024
SB_TK = 128
SB_TQ = 256
HGRN_DIAG = 8
VMEM_LIMIT = 56 * 1024 * 1024


def _cparams(n_axes):
    return pltpu.CompilerParams(dimension_semantics=("arbitrary",) * n_axes,
                                vmem_limit_bytes=VMEM_LIMIT)


def _resident(a):
    zeros = (0,) * a.ndim
    return pl.BlockSpec(a.shape, lambda *_: zeros, pipeline_mode=pl.Buffered(1))


def _resident_layer(a, layer):
    index = (layer,) + (0,) * (a.ndim - 1)
    return pl.BlockSpec((1,) + a.shape[1:], lambda *_: index, pipeline_mode=pl.Buffered(1))


def _split3(x):
    h1 = x.astype(BF16)
    r1 = x - h1.astype(F32)
    h2 = r1.astype(BF16)
    h3 = (r1 - h2.astype(F32)).astype(BF16)
    return h1, h2, h3


def _dot(a, b):
    return jnp.dot(a, b, preferred_element_type=F32)


def _dot_nt(a, b):
    return lax.dot_general(a, b, (((1,), (1,)), ((), ())), preferred_element_type=F32)


def _dot_tn(a, b):
    return lax.dot_general(a, b, (((0,), (0,)), ((), ())), preferred_element_type=F32)


def _iota(shape, dim):
    return lax.broadcasted_iota(jnp.int32, shape, dim)


def _proj_kernel(x_ref, nm_ref, w_ref, qna_ref, kna_ref, qnb_ref, knb_ref, lb_ref, *rest,
                 layer, time_minor):
    (qa_ref, ka_ref, va_ref, ka16_ref, va16_ref, qb_ref, kb_ref, vb_ref, kb16_ref, vb16_ref,
     qc_ref, kc_ref, lf_ref, ic_ref, gc_ref) = rest[-15:]
    if time_minor:
        for earlier_ref, all_ref in zip(rest[:-15], (ka_ref, va_ref)):
            all_ref[0:layer] = earlier_ref[...]
        ka_ref, va_ref = ka_ref.at[layer], va_ref.at[layer]
    x = x_ref[...]
    ms = jnp.mean(x * x, axis=-1, keepdims=True)
    xn = (x * lax.rsqrt(ms + EPS) * nm_ref[...]).astype(BF16)

    tm = x.shape[0]
    part = PROJ_PART if tm % PROJ_PART == 0 else tm
    projected = [_dot(xn[r0:r0 + part], w_ref[0]) for r0 in range(0, tm, part)]

    same_head = (_iota((W_ATT, W_ATT), 0) // D_HEAD) == (_iota((W_ATT, W_ATT), 1) // D_HEAD)
    seg_mean = jnp.where(same_head, 1.0 / D_HEAD, 0.0).astype(BF16)
    inv_sqrt_d = 1.0 / math.sqrt(D_HEAD)

    lbr = lb_ref[...]
    e = jnp.exp(lbr - jnp.max(lbr, axis=0, keepdims=True))
    sm = e / jnp.sum(e, axis=0, keepdims=True)
    lb = jnp.sum(sm[0:layer + 1], axis=0, keepdims=True) - sm[0:1]
    log_lb = jnp.log(jnp.maximum(lb, LB_FLOOR))
    log_1m_lb = jnp.log1p(-lb)

    def head_norm(p, g_ref):
        m = _dot((p * p).astype(BF16), seg_mean)
        return p * lax.rsqrt(m + EPS) * g_ref[...]

    for n, p in enumerate(projected):
        rows = slice(n * part, (n + 1) * part)

        def put(x, f32_ref, b16_ref, b16_time_minor):
            x_t = x.T if time_minor else None
            if time_minor:
                f32_ref[:, rows] = x_t
            else:
                f32_ref[rows, :] = x
            if time_minor and b16_time_minor:
                b16_ref[:, rows] = x_t.astype(BF16)
            else:
                b16_ref[rows, :] = x.astype(BF16)

        o = 0
        qa_ref[rows, :] = (head_norm(p[:, o:o + W_ATT], qna_ref) * inv_sqrt_d).astype(BF16)
        o += W_ATT
        put(head_norm(p[:, o:o + W_ATT], kna_ref), ka_ref, ka16_ref, True)
        o += W_ATT
        put(p[:, o:o + W_ATT], va_ref, va16_ref, False)
        o += W_ATT
        qb_ref[rows, :] = (head_norm(p[:, o:o + W_ATT], qnb_ref)
                           * (inv_sqrt_d * LOG2E)).astype(BF16)
        o += W_ATT
        put(head_norm(p[:, o:o + W_ATT], knb_ref), kb_ref, kb16_ref, False)
        o += W_ATT
        put(p[:, o:o + W_ATT], vb_ref, vb16_ref, True)
        o += W_ATT
        qc_ref[rows, :] = p[:, o:o + W_C] * (DK_C ** -0.5)
        o += W_C
        f_raw = p[:, o:o + W_C]
        o += W_C
        log_sig = jnp.minimum(f_raw, 0.0) - jnp.log1p(jnp.exp(-jnp.abs(f_raw)))
        b = log_1m_lb + log_sig
        lf_ref[rows, :] = jnp.maximum(log_lb, b) + jnp.log1p(jnp.exp(-jnp.abs(log_lb - b)))
        kc_ref[rows, :] = (1.0 - lb) * (1.0 / (1.0 + jnp.exp(f_raw)))
        ic_ref[rows, :] = p[:, o:o + W_C]
        o += W_C
        gc_ref[rows, :] = p[:, o:o + W_C]


def _proj(x2d, nm, w16, qna, kna, qnb, knb, lower_bounds, *, layer, tm, time_minor, carried=()):
    m, d = x2d.shape
    assert m % tm == 0 and len(carried) == (2 if time_minor and layer > 0 else 0)
    full = _resident
    row = lambda w, dt: (pl.BlockSpec((tm, w), lambda i: (i, 0)), jax.ShapeDtypeStruct((m, w), dt))
    col = lambda w, dt: (pl.BlockSpec((w, tm), lambda i: (0, i)), jax.ShapeDtypeStruct((w, m), dt))
    layers = lambda n, w: pl.BlockSpec((n, w, tm), lambda i: (0, 0, i))
    layered = lambda w, dt: (layers(layer + 1, w), jax.ShapeDtypeStruct((layer + 1, w, m), dt))
    kv = col if time_minor else row
    kv_a = layered if time_minor else row
    att_a = [row(W_ATT, BF16), kv_a(W_ATT, F32), kv_a(W_ATT, F32), kv(W_ATT, BF16), row(W_ATT, BF16)]
    att_b = [row(W_ATT, BF16), kv(W_ATT, F32), kv(W_ATT, F32), row(W_ATT, BF16), kv(W_ATT, BF16)]
    outs = att_a + att_b + [row(W_C, F32)] * 5
    return pl.pallas_call(
        functools.partial(_proj_kernel, layer=layer, time_minor=time_minor),
        grid=(m // tm,),
        in_specs=[row(d, F32)[0], full(nm), _resident_layer(w16, layer), full(qna), full(kna),
                  full(qnb), full(knb),
                  full(lower_bounds)] + [layers(layer, W_ATT)] * len(carried),
        out_specs=[spec for spec, _ in outs],
        out_shape=[shape for _, shape in outs],
        compiler_params=_cparams(1),
        name="proj",
    )(x2d, nm, w16, qna, kna, qnb, knb, lower_bounds, *carried)


def _head_masks(width):
    lane_head = _iota((1, width), 1) // D_HEAD
    return [lane_head == h for h in range(N_HEADS)]


def _stack_heads(x, head_lane):
    return jnp.concatenate([jnp.where(m, x, jnp.zeros_like(x)) for m in head_lane], axis=0)


def _stack_heads_t(x_t, n):
    row_head = _iota((W_ATT, 1), 0) // D_HEAD
    return jnp.concatenate(
        [jnp.where(row_head == h, x_t, jnp.zeros_like(x_t)) for h in range(N_HEADS)], axis=1)


def _stack_groups(x, head_lane, gq):
    return jnp.concatenate(
        [_stack_heads(x[g:g + gq], head_lane) for g in range(0, x.shape[0], gq)], axis=0)


def _sb_block(z, weigh_values, mask, tri, carry_ref, acc_ref, gq):
    tk = z.shape[1]
    t = jnp.log(1.0 + jnp.exp2(jnp.abs(z) * -math.log2(math.e)))
    log_1m = jnp.minimum(-z, 0.0) - t
    log_beta = log_1m + z
    if mask is not None:
        log_1m = jnp.where(mask, log_1m, 0.0)
    hi = log_1m.astype(BF16)
    lo = (log_1m - hi.astype(F32)).astype(BF16)
    later = _dot(jnp.concatenate([hi, lo], axis=1), tri)
    carry = carry_ref[...]
    w = jnp.exp(log_beta + later + carry[:, :tk])
    if mask is not None:
        w = jnp.where(mask, w, 0.0)
    w = w.astype(BF16)
    blocks = [w[n * gq:(n + 1) * gq] for n in range(z.shape[0] // gq)]
    w_heads = jnp.concatenate(
        [jnp.concatenate(blocks[g:g + N_HEADS], axis=1) for g in range(0, len(blocks), N_HEADS)],
        axis=0)
    acc_ref[...] += weigh_values(w_heads)
    carry_ref[...] = carry + jnp.sum(log_1m, axis=-1, keepdims=True)


def _sb_live(carry_ref):
    return (jnp.max(carry_ref[...]) > -SB_UNDERFLOW).astype(jnp.int32)


def _strict_upper(n):
    return jnp.where(_iota((2 * n, n), 0) % n > _iota((2 * n, n), 1), 1.0, 0.0).astype(BF16)


def _sb_prompt_kernel(q_ref, kt_ref, v_ref, o_ref, carry_ref, acc_ref, *, tq):
    i = pl.program_id(0)
    head_lane = _head_masks(W_ATT)
    gq = SB_TK
    n_groups = tq // gq
    q_stack = _stack_groups(q_ref[...], head_lane, gq)
    tri = _strict_upper(SB_TK)
    carry_ref[...] = jnp.zeros_like(carry_ref)
    acc_ref[...] = jnp.zeros_like(acc_ref)

    def add_block(j, first_group, mask):
        r0 = pl.multiple_of(j * SB_TK, SB_TK)
        s0 = first_group * N_HEADS * gq
        z = _dot(q_stack[s0:], kt_ref[:, pl.ds(r0, SB_TK)])
        v_stack = _stack_heads(v_ref[pl.ds(r0, SB_TK), :], head_lane)
        _sb_block(z, lambda w: _dot(w, v_stack), mask, tri,
                  carry_ref.at[pl.ds(s0, z.shape[0])],
                  acc_ref.at[pl.ds(first_group * gq, (n_groups - first_group) * gq)], gq)

    for g in reversed(range(n_groups)):
        shape = ((n_groups - g) * N_HEADS * gq, SB_TK)
        row = _iota(shape, 0)
        add_block(i * n_groups + g, g, jnp.logical_or(_iota(shape, 1) < row % gq, row >= N_HEADS * gq))
    add_block(jnp.maximum(i * n_groups - 1, 0), 0,
              jnp.broadcast_to(i > 0, (n_groups * N_HEADS * gq, SB_TK)))

    def cond(c):
        j, live = c
        return jnp.logical_and(j >= 0, live > 0)

    def body(c):
        j, _ = c
        add_block(j, 0, None)
        return j - 1, _sb_live(carry_ref)

    lax.while_loop(cond, body, (i * n_groups - 2, _sb_live(carry_ref)))
    o_ref[...] = acc_ref[...].astype(o_ref.dtype)


def _sb_prompt(q16, kt16, v16, *, tq):
    t = q16.shape[0]
    assert t % tq == 0 and tq % SB_TK == 0
    return pl.pallas_call(
        functools.partial(_sb_prompt_kernel, tq=tq),
        grid=(t // tq,),
        in_specs=[pl.BlockSpec((tq, W_ATT), lambda i: (i, 0)),
                  _resident(kt16), _resident(v16)],
        out_specs=pl.BlockSpec((tq, W_ATT), lambda i: (i, 0)),
        out_shape=jax.ShapeDtypeStruct((t, W_ATT), BF16),
        scratch_shapes=[pltpu.VMEM((N_HEADS * tq, SB_TK), F32), pltpu.VMEM((tq, W_ATT), F32)],
        compiler_params=_cparams(1),
        name="sb_prompt",
    )(q16, kt16, v16)


def _sb_sample_kernel(q_ref, kn_ref, vn_ref, ckt_ref, cvt_ref, o_ref, carry_ref, acc_ref, *, ts, past):
    head_lane = _head_masks(W_ATT)
    q_stack = _stack_heads(q_ref[...], head_lane)
    carry_ref[...] = jnp.zeros_like(carry_ref)
    acc_ref[...] = jnp.zeros_like(acc_ref)
    shape = (N_HEADS * ts, ts)
    mask = _iota(shape, 1) < _iota(shape, 0) % ts
    v_new = _stack_heads(vn_ref[...], head_lane)
    _sb_block(_dot_nt(q_stack, kn_ref[...]), lambda w: _dot(w, v_new), mask, _strict_upper(ts),
              carry_ref, acc_ref, ts)
    tri = _strict_upper(SB_TK)

    def cond(c):
        j, live = c
        return jnp.logical_and(j >= 0, live > 0)

    def body(c):
        j, _ = c
        r0 = pl.multiple_of(j * SB_TK, SB_TK)
        z = _dot(q_stack, ckt_ref[0, 0, :, pl.ds(r0, SB_TK)].astype(BF16))
        vt_stack = _stack_heads_t(cvt_ref[0, 0, :, pl.ds(r0, SB_TK)].astype(BF16), SB_TK)
        _sb_block(z, lambda w: _dot_nt(w, vt_stack), None, tri, carry_ref, acc_ref, ts)
        return j - 1, _sb_live(carry_ref)

    lax.while_loop(cond, body, (past // SB_TK - 1, _sb_live(carry_ref)))
    o_ref[...] = acc_ref[...].astype(o_ref.dtype)


def _sb_sample(q16, kn16, vn16, cache_kt, cache_vt, *, layer, nb, ts):
    past = cache_kt.shape[3]
    assert past % SB_TK == 0
    row = pl.BlockSpec((ts, W_ATT), lambda b: (b, 0))
    cache = pl.BlockSpec((1, 1, W_ATT, past), lambda b: (layer, b, 0, 0))
    return pl.pallas_call(
        functools.partial(_sb_sample_kernel, ts=ts, past=past),
        grid=(nb,),
        in_specs=[row, row, row, cache, cache],
        out_specs=row,
        out_shape=jax.ShapeDtypeStruct((nb * ts, W_ATT), BF16),
        scratch_shapes=[pltpu.VMEM((N_HEADS * ts, SB_TK), F32), pltpu.VMEM((ts, W_ATT), F32)],
        compiler_params=_cparams(1),
        name="sb_sample",
    )(q16, kn16, vn16, cache_kt, cache_vt)


def _rel_bias(table_ref, h, dist):
    idx = jnp.clip(dist, -REL_CLIP, REL_CLIP) + REL_CLIP

    def body(r, b):
        return jnp.where(idx == r, table_ref[h, r], b)

    return lax.fori_loop(0, N_REL, body, jnp.zeros(dist.shape, F32))


def _row_reduce(x, combine, reduce):
    width = x.shape[1]
    acc = x[:, :128]
    for c in range(128, width, 128):
        acc = combine(acc, x[:, c:c + 128])
    return reduce(acc, axis=-1, keepdims=True)


def _toeplitz_bias(table_ref, h, rows, back, width):
    lane = _iota((8, width), 1)
    by_diag = _rel_bias(table_ref, h, back + rows - lane) * LOG2E
    tiled = jnp.concatenate([by_diag] * (rows // 8), axis=0)
    return pltpu.roll(tiled, width - rows, 1, stride=1, stride_axis=0)


BAND_TQ = 128
BAND_BACK = N_PREV_CHUNKS * CHUNK
BAND_WIN = BAND_BACK + BAND_TQ


def _band_prompt_kernel(table_ref, q_ref, k0_ref, k1_ref, k2_ref, v0_ref, v1_ref, v2_ref,
                        o_ref, bias_ref, *, tq):
    i = pl.program_id(0)
    n_kb = 3

    @pl.when(i == 0)
    def _():
        rows = _iota((BAND_TQ, BAND_WIN), 0)
        cols = _iota((BAND_TQ, BAND_WIN), 1)
        q_chunk = rows // CHUNK + N_PREV_CHUNKS
        k_chunk = cols // CHUNK
        in_band = jnp.logical_and(k_chunk >= q_chunk - N_PREV_CHUNKS, k_chunk <= q_chunk)
        for h in range(N_HEADS):
            bias = _toeplitz_bias(table_ref, h, BAND_TQ, BAND_BACK, BAND_WIN + BAND_TQ)
            bias_ref[h] = jnp.where(in_band, bias[:, :BAND_WIN], NEG_BIG).T

    q = q_ref[...]
    k_rows = jnp.concatenate([k0_ref[...], k1_ref[...], k2_ref[...]], axis=0)
    v_t = jnp.concatenate([v0_ref[...], v1_ref[...], v2_ref[...]], axis=1)
    feature_head = _iota((W_ATT, 1), 0) // D_HEAD
    key_row = _iota((BAND_WIN, BAND_TQ), 0)
    for p in range(tq // BAND_TQ):
        r0 = p * BAND_TQ
        w0 = r0 + (n_kb - 1) * tq - BAND_BACK
        q_t = q[r0:r0 + BAND_TQ].T
        q_heads = jnp.concatenate(
            [jnp.where(feature_head == h, q_t, jnp.zeros_like(q_t)) for h in range(N_HEADS)], axis=1)
        s_all = _dot(k_rows[w0:w0 + BAND_WIN], q_heads)
        exists = (i - (n_kb - 1)) * tq + w0 + key_row >= 0
        out_t = []
        for h in range(N_HEADS):
            s = s_all[:, h * BAND_TQ:(h + 1) * BAND_TQ]
            s = jnp.where(exists, s + bias_ref[h], NEG_BIG)
            e = jnp.exp2(s - jnp.max(s, axis=0, keepdims=True))
            inv_den = 1.0 / jnp.sum(e, axis=0, keepdims=True)
            v_h = v_t[h * D_HEAD:(h + 1) * D_HEAD, w0:w0 + BAND_WIN]
            out_t.append(_dot(v_h, e.astype(BF16)) * inv_den)
        o_ref[r0:r0 + BAND_TQ, :] = jnp.concatenate(out_t, axis=0).T.astype(o_ref.dtype)


def _band_prompt(q16, k16, vt16, table, *, tq):
    t = q16.shape[0]
    assert t % tq == 0 and tq % BAND_TQ == 0 and 2 * tq >= BAND_BACK
    blk = lambda back: pl.BlockSpec((tq, W_ATT), lambda i: (jnp.maximum(i - back, 0), 0))
    blk_t = lambda back: pl.BlockSpec((W_ATT, tq), lambda i: (0, jnp.maximum(i - back, 0)))
    return pl.pallas_call(
        functools.partial(_band_prompt_kernel, tq=tq),
        grid=(t // tq,),
        in_specs=[pl.BlockSpec(memory_space=pltpu.SMEM),
                  blk(0), blk(2), blk(1), blk(0), blk_t(2), blk_t(1), blk_t(0)],
        out_specs=blk(0),
        out_shape=jax.ShapeDtypeStruct((t, W_ATT), BF16),
        scratch_shapes=[pltpu.VMEM((N_HEADS, BAND_WIN, BAND_TQ), F32)],
        compiler_params=_cparams(1),
        name="band_prompt",
    )(table, q16, k16, k16, k16, vt16, vt16, vt16)


def _band_sample_kernel(table_ref, q_ref, kn_ref, vn_ref, ckt_ref, cvt_ref, o_ref, bias_ref, *, ts, past):
    lb = ckt_ref.shape[3]

    @pl.when(pl.program_id(0) == 0)
    def _():
        for h in range(N_HEADS):
            bias_ref[h] = _toeplitz_bias(table_ref, h, ts, lb, bias_ref.shape[2])

    q = q_ref[...]
    head_lane = _head_masks(W_ATT)
    kc_t = ckt_ref[0, 0].astype(BF16)
    vc_t = cvt_ref[0, 0].astype(BF16)
    kn = kn_ref[...]
    vn = vn_ref[...]
    q_pos_c = past + _iota((ts, lb), 0)
    k_pos_c = past - lb + _iota((ts, lb), 1)
    q_pos_n = past + _iota((ts, ts), 0)
    k_pos_n = past + _iota((ts, ts), 1)

    def allowed(q_pos, k_pos):
        qc, kc_ = q_pos // CHUNK, k_pos // CHUNK
        return jnp.logical_and(kc_ >= qc - N_PREV_CHUNKS, kc_ <= qc)

    ok_c = allowed(q_pos_c, k_pos_c)
    ok_n = allowed(q_pos_n, k_pos_n)
    out = jnp.zeros((ts, W_ATT), F32)
    for h in range(N_HEADS):
        qh = jnp.where(head_lane[h], q, jnp.zeros_like(q))
        bias = bias_ref[h]
        s_c = jnp.where(ok_c, _dot(qh, kc_t) + bias[:, :lb], NEG_BIG)
        s_n = jnp.where(ok_n, _dot_nt(qh, kn) + bias[:, lb:lb + ts], NEG_BIG)
        m = jnp.maximum(jnp.max(s_c, axis=-1, keepdims=True), jnp.max(s_n, axis=-1, keepdims=True))
        e_c = jnp.exp2(s_c - m)
        e_n = jnp.exp2(s_n - m)
        den = jnp.sum(e_c, axis=-1, keepdims=True) + jnp.sum(e_n, axis=-1, keepdims=True)
        pv = (_dot_nt(e_c.astype(BF16), vc_t) + _dot(e_n.astype(BF16), vn)) / den
        out = jnp.where(head_lane[h], pv, out)
    o_ref[...] = out.astype(o_ref.dtype)


def _band_sample(q16, kn16, vn16, cache_kt, cache_vt, table, *, layer, nb, ts, past):
    lb = cache_kt.shape[3]
    row = pl.BlockSpec((ts, W_ATT), lambda b: (b, 0))
    cache = pl.BlockSpec((1, 1, W_ATT, lb), lambda b: (layer, b, 0, 0))
    return pl.pallas_call(
        functools.partial(_band_sample_kernel, ts=ts, past=past),
        grid=(nb,),
        in_specs=[pl.BlockSpec(memory_space=pltpu.SMEM), row, row, row, cache, cache],
        out_specs=row,
        out_shape=jax.ShapeDtypeStruct((nb * ts, W_ATT), BF16),
        scratch_shapes=[pltpu.VMEM((N_HEADS, ts, pl.cdiv(lb + 2 * ts, 128) * 128), F32)],
        compiler_params=_cparams(1),
        name="band_sample",
    )(table, q16, kn16, vn16, cache_kt, cache_vt)


def _hgrn_spans(q, k, lf, v, st_ref):
    tr = q.shape[0]
    heads = [slice(h * DK_C, (h + 1) * DK_C) for h in range(N_HEADS)]
    tri = jnp.where(_iota((tr, tr), 0) >= _iota((tr, tr), 1), 1.0, 0.0).astype(BF16)
    f1, f2, f3 = _split3(lf)
    g = (_dot(tri, f1) + _dot(tri, f2) + _dot(tri, f3)) * math.log2(math.e)
    v16 = v.astype(BF16)

    g_end = g[tr - 1:tr, :]
    qt = (q * jnp.exp2(g)).astype(BF16)
    kd = (k * jnp.exp2(g_end - g)).astype(BF16)
    dec = jnp.exp2(g_end)
    states = [st_ref[h] for h in range(N_HEADS)]
    o_heads = [_dot_nt(qt[:, hs], st.astype(BF16)) for hs, st in zip(heads, states)]
    grown = [_dot_tn(v16[:, hs], kd[:, hs]) for hs in heads]
    for h, hs in enumerate(heads):
        st_ref[h] = states[h] * dec[:, hs] + grown[h]
    o = jnp.concatenate(o_heads, axis=1)

    sizes = []
    h = HGRN_DIAG
    while h < tr:
        sizes.append(h)
        h *= 2
    split = lambda x, h: x.reshape(tr // (2 * h), 2 * h, W_C)
    operands = []
    for h in sizes:
        g3, q3, k3, v3 = split(g, h), split(q, h), split(k, h), split(v16, h)
        c = g3[:, h - 1:h, :]
        qe = (q3[:, h:] * jnp.exp2(g3[:, h:] - c)).reshape(tr // 2, W_C).astype(BF16)
        ke = (k3[:, :h] * jnp.exp2(c - g3[:, :h])).reshape(tr // 2, W_C).astype(BF16)
        operands.append((qe, ke, v3[:, :h].reshape(tr // 2, W_C)))
    pair = [[_dot_nt(qe[:, hs], ke[:, hs]) for hs in heads] for qe, ke, _ in operands]
    span_of_row = _iota((tr // 2, tr // 2), 0)
    span_of_col = _iota((tr // 2, tr // 2), 1)
    for n, h in enumerate(sizes):
        if 2 * h < tr:
            same_span = (span_of_row // h) == (span_of_col // h)
            pair[n] = [jnp.where(same_span, p, 0.0) for p in pair[n]]
    added = [jnp.concatenate([_dot(p.astype(BF16), vl[:, hs]) for p, hs in zip(pair[n], heads)],
                             axis=1) for n, (_, _, vl) in enumerate(operands)]
    for n, h in enumerate(sizes):
        o3 = split(o, h)
        upper = o3[:, h:] + added[n].reshape(tr // (2 * h), h, W_C)
        o = jnp.concatenate([o3[:, :h], upper], axis=1).reshape(tr, W_C)
    return o, g


def _hgrn_diagonal(g2, q2, k2, v2):
    n = HGRN_DIAG
    lane = _iota((n, DK_C), 1)
    weights = [[jnp.zeros((n, DK_C), F32) for _ in range(N_HEADS)] for _ in range(2)]
    for s in range(2):
        g8, q8, k8 = g2[s * n:(s + 1) * n], q2[s * n:(s + 1) * n], k2[s * n:(s + 1) * n]
        for j in range(n):
            p = q8 * (k8[j:j + 1, :] * jnp.exp2(g8 - g8[j:j + 1, :]))
            for h in range(N_HEADS):
                a = jnp.sum(p[:, h * DK_C:(h + 1) * DK_C], axis=-1, keepdims=True)
                weights[s][h] = jnp.where(lane == s * n + j, a, weights[s][h])
    row = _iota((2 * n, DK_C), 0)
    col = _iota((2 * n, DK_C), 1)
    keep = jnp.logical_and(row >= col, row // n == col // n)
    v16 = v2.astype(BF16)
    out = []
    for h in range(N_HEADS):
        a = jnp.where(keep, jnp.concatenate([weights[0][h], weights[1][h]], axis=0), 0.0)
        out.append(_dot(a[:, :2 * n].astype(BF16), v16[:, h * DK_C:(h + 1) * DK_C]))
    return jnp.concatenate(out, axis=1)


def _hgrn_block(q, k, lf, v, st_ref):
    o, g = _hgrn_spans(q, k, lf, v, st_ref)
    pairs = [slice(r0, r0 + 2 * HGRN_DIAG) for r0 in range(0, q.shape[0], 2 * HGRN_DIAG)]
    return o + jnp.concatenate([_hgrn_diagonal(g[s], q[s], k[s], v[s]) for s in pairs], axis=0)


def _hgrn_kernel(q_ref, k_ref, lf_ref, v_ref, s0_ref, o_ref, sout_ref, st_ref):
    r = pl.program_id(1)

    @pl.when(r == 0)
    def _():
        for h in range(N_HEADS):
            st_ref[h] = s0_ref[0, h].T

    o_ref[...] = _hgrn_block(q_ref[...], k_ref[...], lf_ref[...], v_ref[...], st_ref)

    @pl.when(r == pl.num_programs(1) - 1)
    def _():
        for h in range(N_HEADS):
            sout_ref[0, h] = st_ref[h].T


def _hgrn(qc, kc, lf, ic, s0, *, nb, t, tr):
    assert t % tr == 0 and tr % (2 * HGRN_DIAG) == 0 and tr & (tr - 1) == 0
    nr = t // tr
    row = pl.BlockSpec((tr, W_C), lambda b, r: (b * nr + r, 0))
    state = pl.BlockSpec((1, N_HEADS, DK_C, DK_C), lambda b, r: (b, 0, 0, 0))
    return pl.pallas_call(
        _hgrn_kernel,
        grid=(nb, nr),
        in_specs=[row, row, row, row, state],
        out_specs=[row, state],
        out_shape=[jax.ShapeDtypeStruct((nb * t, W_C), F32),
                   jax.ShapeDtypeStruct((nb, N_HEADS, DK_C, DK_C), F32)],
        scratch_shapes=[pltpu.VMEM((N_HEADS, DK_C, DK_C), F32)],
        compiler_params=_cparams(2),
        name="hgrn",
    )(qc, kc, lf, ic, s0)


def _merge_block(x, oa, ob, oc, gc, on, wo_ref, nf, wu_ref, wd_ref):
    parts = []
    for h in range(N_HEADS):
        och = oc[:, h * DK_C:(h + 1) * DK_C]
        ms = jnp.mean(och * och, axis=-1, keepdims=True)
        parts.append(och * lax.rsqrt(ms + EPS) * on)
    ocn = jnp.concatenate(parts, axis=-1) * (gc * (1.0 / (1.0 + jnp.exp(-gc))))
    mixed = jnp.concatenate([oa, ob, ocn.astype(BF16)], axis=-1)
    h_res = x + _dot(mixed, wo_ref[0])
    ms = jnp.mean(h_res * h_res, axis=-1, keepdims=True)
    hn = (h_res * lax.rsqrt(ms + EPS) * nf).astype(BF16)
    y = h_res
    for c in range(0, wu_ref.shape[2], FFN_CHUNK):
        u = jnp.maximum(_dot(hn, wu_ref[0, :, c:c + FFN_CHUNK]), 0.0)
        y = y + _dot((u * u).astype(BF16), wd_ref[0, c:c + FFN_CHUNK, :])
    return y


def _merge_kernel(x_ref, oa_ref, ob_ref, oc_ref, gc_ref, on_ref, wo_ref, nf_ref, wu_ref, wd_ref,
                  y_ref):
    y_ref[...] = _merge_block(x_ref[...], oa_ref[...], ob_ref[...], oc_ref[...], gc_ref[...],
                              on_ref[...], wo_ref, nf_ref[...], wu_ref, wd_ref)


def _merge(x2d, oa, ob, oc, gc, onorm, wo16, nf, wu16, wd16, *, layer, tm):
    m, d = x2d.shape
    assert m % tm == 0
    row = lambda w: pl.BlockSpec((tm, w), lambda i: (i, 0))
    full = _resident
    return pl.pallas_call(
        _merge_kernel,
        grid=(m // tm,),
        in_specs=[row(d), row(W_ATT), row(W_ATT), row(W_C), row(W_C),
                  full(onorm), _resident_layer(wo16, layer), full(nf),
                  _resident_layer(wu16, layer), _resident_layer(wd16, layer)],
        out_specs=row(d),
        out_shape=jax.ShapeDtypeStruct((m, d), F32),
        compiler_params=_cparams(1),
        name="merge_ffn",
    )(x2d, oa, ob, oc, gc, onorm, wo16, nf, wu16, wd16)


def kernel(x_prompt, x_sample, cache_a_k, cache_a_v, cache_b_k, cache_b_v, state_c, norm_mix, w_in, qnorm_a, knorm_a, qnorm_b, knorm_b, rel_bias_b, lower_bounds, onorm_c, w_o, norm_ffn, w_up, w_down):
    depth = w_in.shape[0]
    bp, tp, d = x_prompt.shape
    bs, ts, _ = x_sample.shape
    past = cache_a_k.shape[2]
    band_rows_p = min(N_PREV_CHUNKS * CHUNK, tp)
    assert bp == 1

    xp = x_prompt.reshape(bp * tp, d)
    xs = x_sample.reshape(bs * ts, d)
    time_minor = lambda c: jnp.transpose(c, (0, 1, 3, 4, 2)).reshape(depth, bs, W_ATT, c.shape[2])
    cak, cav, cbk, cbv = (time_minor(c) for c in (cache_a_k, cache_a_v, cache_b_k, cache_b_v))
    heads_of = lambda a_t, n: jnp.transpose(
        a_t.reshape(N_HEADS, D_HEAD, n, a_t.shape[1] // n), (2, 3, 0, 1))
    lbounds = lower_bounds.astype(F32)
    w16, wo16, wu16, wd16 = (w.astype(BF16) for w in (w_in, w_o, w_up, w_down))
    zero_state = jnp.zeros((bp, N_HEADS, DK_C, DK_C), F32)

    tm_p = min(512, tp)
    tq_band = min(256, tp)
    tr_p = min(256, tp)

    outs_p = [[] for _ in range(5)]
    outs_s = [[] for _ in range(5)]
    carried = ()
    for l in range(depth):
        nm = norm_mix[l].reshape(1, d)
        nf = norm_ffn[l].reshape(1, d)
        tile_h = lambda g: jnp.tile(g.reshape(1, D_HEAD), (1, N_HEADS))
        qna, kna, qnb, knb = (tile_h(g[l]) for g in (qnorm_a, knorm_a, qnorm_b, knorm_b))
        onorm = onorm_c[l].reshape(1, DK_C)
        table = rel_bias_b[l].astype(F32)

        (qa, ka_all, va_all, ka16_t, va16, qb, kb_t, vb_t, kb16, vb16_t, qc, kc, lf, ic, gc) = _proj(
            xp, nm, w16, qna, kna, qnb, knb, lbounds, layer=l, tm=tm_p, time_minor=True,
            carried=carried)
        carried = (ka_all, va_all)
        oa = _sb_prompt(qa, ka16_t, va16, tq=min(SB_TQ, tp))
        ob = _band_prompt(qb, kb16, vb16_t, table, tq=tq_band)
        oc, s_p = _hgrn(qc, kc, lf, ic, zero_state, nb=bp, t=tp, tr=tr_p)
        xp = _merge(xp, oa, ob, oc, gc, onorm, wo16, nf, wu16, wd16, layer=l, tm=tm_p)
        outs_p[2].append(heads_of(kb_t, bp)[:, tp - band_rows_p:])
        outs_p[3].append(heads_of(vb_t, bp)[:, tp - band_rows_p:])
        outs_p[4].append(s_p)

        head = lambda a, n: a.reshape(n, -1, N_HEADS, D_HEAD)
        (qa, ka, va, ka16, va16, qb, kb, vb, kb16, vb16, qc, kc, lf, ic, gc) = _proj(
            xs, nm, w16, qna, kna, qnb, knb, lbounds, layer=l, tm=bs * ts, time_minor=False)
        oa = _sb_sample(qa, ka16, va16, cak, cav, layer=l, nb=bs, ts=ts)
        ob = _band_sample(qb, kb16, vb16, cbk, cbv, table, layer=l, nb=bs, ts=ts, past=past)
        oc, s_s = _hgrn(qc, kc, lf, ic, state_c[l].astype(F32), nb=bs, t=ts, tr=ts)
        xs = _merge(xs, oa, ob, oc, gc, onorm, wo16, nf, wu16, wd16, layer=l, tm=bs * ts)
        outs_s[0].append(head(ka, bs))
        outs_s[1].append(head(va, bs))
        outs_s[2].append(head(kb, bs))
        outs_s[3].append(head(vb, bs))
        outs_s[4].append(s_s)

    stack = lambda xs_: jnp.stack(xs_)
    all_heads_of = lambda a: jnp.transpose(
        a.reshape(depth, N_HEADS, D_HEAD, bp, tp), (0, 3, 4, 1, 2))
    ka_all, va_all = carried
    return (xp.reshape(bp, tp, d), xs.reshape(bs, ts, d),
            all_heads_of(ka_all), all_heads_of(va_all),
            stack(outs_p[2]), stack(outs_p[3]), stack(outs_p[4]),
            stack(outs_s[0]), stack(outs_s[1]), stack(outs_s[2]), stack(outs_s[3]), stack(outs_s[4]))
```

```python
import functools
import math

import jax
import jax.numpy as jnp
from jax import lax
from jax.experimental import pallas as pl
from jax.experimental.pallas import tpu as pltpu

F32 = jnp.float32
BF16 = jnp.bfloat16

D_HEAD = 64
N_HEADS = 4
W_ATT = N_HEADS * D_HEAD
DK_C = 128
W_C = N_HEADS * DK_C
CHUNK = 64
N_PREV_CHUNKS = 8
REL_CLIP = 128
N_REL = 2 * REL_CLIP + 1
EPS = 1e-6
NEG_BIG = -1e30
LB_FLOOR = 1e-30
LOG2E = math.log2(math.e)
SB_UNDERFLOW = 88.0
PROJ_PART = 256
FFN_CHUNK = 1024
SB_TK = 128
SB_TQ = 256
HGRN_DIAG = 8
VMEM_LIMIT = 56 * 1024 * 1024


def _cparams(n_axes):
    return pltpu.CompilerParams(dimension_semantics=("arbitrary",) * n_axes,
                                vmem_limit_bytes=VMEM_LIMIT)


def _resident(a):
    zeros = (0,) * a.ndim
    return pl.BlockSpec(a.shape, lambda *_: zeros, pipeline_mode=pl.Buffered(1))


def _resident_layer(a, layer):
    index = (layer,) + (0,) * (a.ndim - 1)
    return pl.BlockSpec((1,) + a.shape[1:], lambda *_: index, pipeline_mode=pl.Buffered(1))


def _split3(x):
    h1 = x.astype(BF16)
    r1 = x - h1.astype(F32)
    h2 = r1.astype(BF16)
    h3 = (r1 - h2.astype(F32)).astype(BF16)
    return h1, h2, h3


def _dot(a, b):
    return jnp.dot(a, b, preferred_element_type=F32)


def _dot_nt(a, b):
    return lax.dot_general(a, b, (((1,), (1,)), ((), ())), preferred_element_type=F32)


def _dot_tn(a, b):
    return lax.dot_general(a, b, (((0,), (0,)), ((), ())), preferred_element_type=F32)


def _iota(shape, dim):
    return lax.broadcasted_iota(jnp.int32, shape, dim)


def _proj_kernel(x_ref, nm_ref, w_ref, qna_ref, kna_ref, qnb_ref, knb_ref, lb_ref, *rest,
                 layer, time_minor):
    (qa_ref, ka_ref, va_ref, ka16_ref, va16_ref, qb_ref, kb_ref, vb_ref, kb16_ref, vb16_ref,
     qc_ref, kc_ref, lf_ref, ic_ref, gc_ref) = rest[-15:]
    if time_minor:
        for earlier_ref, all_ref in zip(rest[:-15], (ka_ref, va_ref)):
            all_ref[0:layer] = earlier_ref[...]
        ka_ref, va_ref = ka_ref.at[layer], va_ref.at[layer]
    x = x_ref[...]
    ms = jnp.mean(x * x, axis=-1, keepdims=True)
    xn = (x * lax.rsqrt(ms + EPS) * nm_ref[...]).astype(BF16)

    tm = x.shape[0]
    part = PROJ_PART if tm % PROJ_PART == 0 else tm
    projected = [_dot(xn[r0:r0 + part], w_ref[0]) for r0 in range(0, tm, part)]

    same_head = (_iota((W_ATT, W_ATT), 0) // D_HEAD) == (_iota((W_ATT, W_ATT), 1) // D_HEAD)
    seg_mean = jnp.where(same_head, 1.0 / D_HEAD, 0.0).astype(BF16)
    inv_sqrt_d = 1.0 / math.sqrt(D_HEAD)

    lbr = lb_ref[...]
    e = jnp.exp(lbr - jnp.max(lbr, axis=0, keepdims=True))
    sm = e / jnp.sum(e, axis=0, keepdims=True)
    lb = jnp.sum(sm[0:layer + 1], axis=0, keepdims=True) - sm[0:1]
    log_lb = jnp.log(jnp.maximum(lb, LB_FLOOR))
    log_1m_lb = jnp.log1p(-lb)

    def head_norm(p, g_ref):
        m = _dot((p * p).astype(BF16), seg_mean)
        return p * lax.rsqrt(m + EPS) * g_ref[...]

    for n, p in enumerate(projected):
        rows = slice(n * part, (n + 1) * part)

        def put(x, f32_ref, b16_ref, b16_time_minor):
            x_t = x.T if time_minor else None
            if time_minor:
                f32_ref[:, rows] = x_t
            else:
                f32_ref[rows, :] = x
            if time_minor and b16_time_minor:
                b16_ref[:, rows] = x_t.astype(BF16)
            else:
                b16_ref[rows, :] = x.astype(BF16)

        o = 0
        qa_ref[rows, :] = (head_norm(p[:, o:o + W_ATT], qna_ref) * inv_sqrt_d).astype(BF16)
        o += W_ATT
        put(head_norm(p[:, o:o + W_ATT], kna_ref), ka_ref, ka16_ref, True)
        o += W_ATT
        put(p[:, o:o + W_ATT], va_ref, va16_ref, False)
        o += W_ATT
        qb_ref[rows, :] = (head_norm(p[:, o:o + W_ATT], qnb_ref)
                           * (inv_sqrt_d * LOG2E)).astype(BF16)
        o += W_ATT
        put(head_norm(p[:, o:o + W_ATT], knb_ref), kb_ref, kb16_ref, True)
        o += W_ATT
        put(p[:, o:o + W_ATT], vb_ref, vb16_ref, False)
        o += W_ATT
        qc_ref[rows, :] = p[:, o:o + W_C] * (DK_C ** -0.5)
        o += W_C
        f_raw = p[:, o:o + W_C]
        o += W_C
        log_sig = jnp.minimum(f_raw, 0.0) - jnp.log(1.0 + jnp.exp(-jnp.abs(f_raw)))
        b = log_1m_lb + log_sig
        lf_ref[rows, :] = jnp.maximum(log_lb, b) + jnp.log(1.0 + jnp.exp(-jnp.abs(log_lb - b)))
        kc_ref[rows, :] = (1.0 - lb) * (1.0 / (1.0 + jnp.exp(f_raw)))
        ic_ref[rows, :] = p[:, o:o + W_C]
        o += W_C
        gc_ref[rows, :] = p[:, o:o + W_C]


def _proj(x2d, nm, w16, qna, kna, qnb, knb, lower_bounds, *, layer, tm, time_minor, carried=()):
    m, d = x2d.shape
    assert m % tm == 0 and len(carried) == (2 if time_minor and layer > 0 else 0)
    full = _resident
    row = lambda w, dt: (pl.BlockSpec((tm, w), lambda i: (i, 0)), jax.ShapeDtypeStruct((m, w), dt))
    col = lambda w, dt: (pl.BlockSpec((w, tm), lambda i: (0, i)), jax.ShapeDtypeStruct((w, m), dt))
    layers = lambda n, w: pl.BlockSpec((n, w, tm), lambda i: (0, 0, i))
    layered = lambda w, dt: (layers(layer + 1, w), jax.ShapeDtypeStruct((layer + 1, w, m), dt))
    kv = col if time_minor else row
    kv_a = layered if time_minor else row
    att_a = [row(W_ATT, BF16), kv_a(W_ATT, F32), kv_a(W_ATT, F32), kv(W_ATT, BF16), row(W_ATT, BF16)]
    att_b = [row(W_ATT, BF16), kv(W_ATT, F32), kv(W_ATT, F32), kv(W_ATT, BF16), row(W_ATT, BF16)]
    outs = att_a + att_b + [row(W_C, F32)] * 5
    return pl.pallas_call(
        functools.partial(_proj_kernel, layer=layer, time_minor=time_minor),
        grid=(m // tm,),
        in_specs=[row(d, F32)[0], full(nm), _resident_layer(w16, layer), full(qna), full(kna),
                  full(qnb), full(knb),
                  full(lower_bounds)] + [layers(layer, W_ATT)] * len(carried),
        out_specs=[spec for spec, _ in outs],
        out_shape=[shape for _, shape in outs],
        compiler_params=_cparams(1),
        name="proj",
    )(x2d, nm, w16, qna, kna, qnb, knb, lower_bounds, *carried)


def _head_masks(width):
    lane_head = _iota((1, width), 1) // D_HEAD
    return [lane_head == h for h in range(N_HEADS)]


def _stack_heads(x, head_lane):
    return jnp.concatenate([jnp.where(m, x, jnp.zeros_like(x)) for m in head_lane], axis=0)


def _stack_heads_t(x_t, n):
    row_head = _iota((W_ATT, 1), 0) // D_HEAD
    return jnp.concatenate(
        [jnp.where(row_head == h, x_t, jnp.zeros_like(x_t)) for h in range(N_HEADS)], axis=1)


def _stack_groups(x, head_lane, gq):
    return jnp.concatenate(
        [_stack_heads(x[g:g + gq], head_lane) for g in range(0, x.shape[0], gq)], axis=0)


def _sb_block(z, weigh_values, mask, tri, carry_ref, acc_ref, gq):
    tk = z.shape[1]
    t = jnp.log(1.0 + jnp.exp2(jnp.abs(z) * -math.log2(math.e)))
    log_1m = jnp.minimum(-z, 0.0) - t
    log_beta = log_1m + z
    if mask is not None:
        log_1m = jnp.where(mask, log_1m, 0.0)
    hi = log_1m.astype(BF16)
    lo = (log_1m - hi.astype(F32)).astype(BF16)
    later = _dot(jnp.concatenate([hi, lo], axis=1), tri)
    carry = carry_ref[...]
    w = jnp.exp(log_beta + later + carry[:, :tk])
    if mask is not None:
        w = jnp.where(mask, w, 0.0)
    w = w.astype(BF16)
    blocks = [w[n * gq:(n + 1) * gq] for n in range(z.shape[0] // gq)]
    w_heads = jnp.concatenate(
        [jnp.concatenate(blocks[g:g + N_HEADS], axis=1) for g in range(0, len(blocks), N_HEADS)],
        axis=0)
    acc_ref[...] += weigh_values(w_heads)
    carry_ref[...] = carry + jnp.sum(log_1m, axis=-1, keepdims=True)


def _sb_live(carry_ref):
    return (jnp.max(carry_ref[...]) > -SB_UNDERFLOW).astype(jnp.int32)


def _strict_upper(n):
    return jnp.where(_iota((2 * n, n), 0) % n > _iota((2 * n, n), 1), 1.0, 0.0).astype(BF16)


def _sb_prompt_kernel(q_ref, kt_ref, v_ref, o_ref, carry_ref, acc_ref, *, tq):
    i = pl.program_id(0)
    head_lane = _head_masks(W_ATT)
    gq = SB_TK
    n_groups = tq // gq
    q_stack = _stack_groups(q_ref[...], head_lane, gq)
    tri = _strict_upper(SB_TK)
    carry_ref[...] = jnp.zeros_like(carry_ref)
    acc_ref[...] = jnp.zeros_like(acc_ref)

    def add_block(j, first_group, mask):
        r0 = pl.multiple_of(j * SB_TK, SB_TK)
        s0 = first_group * N_HEADS * gq
        z = _dot(q_stack[s0:], kt_ref[:, pl.ds(r0, SB_TK)])
        v_stack = _stack_heads(v_ref[pl.ds(r0, SB_TK), :], head_lane)
        _sb_block(z, lambda w: _dot(w, v_stack), mask, tri,
                  carry_ref.at[pl.ds(s0, z.shape[0])],
                  acc_ref.at[pl.ds(first_group * gq, (n_groups - first_group) * gq)], gq)

    for g in reversed(range(n_groups)):
        shape = ((n_groups - g) * N_HEADS * gq, SB_TK)
        row = _iota(shape, 0)
        add_block(i * n_groups + g, g, jnp.logical_or(_iota(shape, 1) < row % gq, row >= N_HEADS * gq))
    add_block(jnp.maximum(i * n_groups - 1, 0), 0,
              jnp.broadcast_to(i > 0, (n_groups * N_HEADS * gq, SB_TK)))

    def cond(c):
        j, live = c
        return jnp.logical_and(j >= 0, live > 0)

    def body(c):
        j, _ = c
        add_block(j, 0, None)
        return j - 1, _sb_live(carry_ref)

    lax.while_loop(cond, body, (i * n_groups - 2, _sb_live(carry_ref)))
    o_ref[...] = acc_ref[...].astype(o_ref.dtype)


def _sb_prompt(q16, kt16, v16, *, tq):
    t = q16.shape[0]
    assert t % tq == 0 and tq % SB_TK == 0
    return pl.pallas_call(
        functools.partial(_sb_prompt_kernel, tq=tq),
        grid=(t // tq,),
        in_specs=[pl.BlockSpec((tq, W_ATT), lambda i: (i, 0)),
                  _resident(kt16), _resident(v16)],
        out_specs=pl.BlockSpec((tq, W_ATT), lambda i: (i, 0)),
        out_shape=jax.ShapeDtypeStruct((t, W_ATT), BF16),
        scratch_shapes=[pltpu.VMEM((N_HEADS * tq, SB_TK), F32), pltpu.VMEM((tq, W_ATT), F32)],
        compiler_params=_cparams(1),
        name="sb_prompt",
    )(q16, kt16, v16)


def _sb_sample_kernel(q_ref, kn_ref, vn_ref, ckt_ref, cvt_ref, o_ref, carry_ref, acc_ref, *, ts, past):
    head_lane = _head_masks(W_ATT)
    q_stack = _stack_heads(q_ref[...], head_lane)
    carry_ref[...] = jnp.zeros_like(carry_ref)
    acc_ref[...] = jnp.zeros_like(acc_ref)
    shape = (N_HEADS * ts, ts)
    mask = _iota(shape, 1) < _iota(shape, 0) % ts
    v_new = _stack_heads(vn_ref[...], head_lane)
    _sb_block(_dot_nt(q_stack, kn_ref[...]), lambda w: _dot(w, v_new), mask, _strict_upper(ts),
              carry_ref, acc_ref, ts)
    tri = _strict_upper(SB_TK)

    def cond(c):
        j, live = c
        return jnp.logical_and(j >= 0, live > 0)

    def body(c):
        j, _ = c
        r0 = pl.multiple_of(j * SB_TK, SB_TK)
        z = _dot(q_stack, ckt_ref[0, 0, :, pl.ds(r0, SB_TK)].astype(BF16))
        vt_stack = _stack_heads_t(cvt_ref[0, 0, :, pl.ds(r0, SB_TK)].astype(BF16), SB_TK)
        _sb_block(z, lambda w: _dot_nt(w, vt_stack), None, tri, carry_ref, acc_ref, ts)
        return j - 1, _sb_live(carry_ref)

    lax.while_loop(cond, body, (past // SB_TK - 1, _sb_live(carry_ref)))
    o_ref[...] = acc_ref[...].astype(o_ref.dtype)


def _sb_sample(q16, kn16, vn16, cache_kt, cache_vt, *, layer, nb, ts):
    past = cache_kt.shape[3]
    assert past % SB_TK == 0
    row = pl.BlockSpec((ts, W_ATT), lambda b: (b, 0))
    cache = pl.BlockSpec((1, 1, W_ATT, past), lambda b: (layer, b, 0, 0))
    return pl.pallas_call(
        functools.partial(_sb_sample_kernel, ts=ts, past=past),
        grid=(nb,),
        in_specs=[row, row, row, cache, cache],
        out_specs=row,
        out_shape=jax.ShapeDtypeStruct((nb * ts, W_ATT), BF16),
        scratch_shapes=[pltpu.VMEM((N_HEADS * ts, SB_TK), F32), pltpu.VMEM((ts, W_ATT), F32)],
        compiler_params=_cparams(1),
        name="sb_sample",
    )(q16, kn16, vn16, cache_kt, cache_vt)


def _rel_bias(table_ref, h, dist):
    idx = jnp.clip(dist, -REL_CLIP, REL_CLIP) + REL_CLIP

    def body(r, b):
        return jnp.where(idx == r, table_ref[h, r], b)

    return lax.fori_loop(0, N_REL, body, jnp.zeros(dist.shape, F32))


def _row_reduce(x, combine, reduce):
    width = x.shape[1]
    acc = x[:, :128]
    for c in range(128, width, 128):
        acc = combine(acc, x[:, c:c + 128])
    return reduce(acc, axis=-1, keepdims=True)


def _toeplitz_bias(table_ref, h, rows, back, width):
    lane = _iota((8, width), 1)
    by_diag = _rel_bias(table_ref, h, back + rows - lane) * LOG2E
    tiled = jnp.concatenate([by_diag] * (rows // 8), axis=0)
    return pltpu.roll(tiled, width - rows, 1, stride=1, stride_axis=0)


BAND_TQ = 128
BAND_BACK = N_PREV_CHUNKS * CHUNK
BAND_WIN = BAND_BACK + BAND_TQ


def _band_prompt_kernel(table_ref, q_ref, k0_ref, k1_ref, k2_ref, v0_ref, v1_ref, v2_ref,
                        o_ref, bias_ref, *, tq):
    i = pl.program_id(0)
    n_kb = 3
    rows = _iota((BAND_TQ, BAND_WIN), 0)
    cols = _iota((BAND_TQ, BAND_WIN), 1)

    @pl.when(i == 0)
    def _():
        q_chunk = rows // CHUNK + N_PREV_CHUNKS
        k_chunk = cols // CHUNK
        in_band = jnp.logical_and(k_chunk >= q_chunk - N_PREV_CHUNKS, k_chunk <= q_chunk)
        for h in range(N_HEADS):
            bias = _toeplitz_bias(table_ref, h, BAND_TQ, BAND_BACK, BAND_WIN + BAND_TQ)
            bias_ref[h] = jnp.where(in_band, bias[:, :BAND_WIN], NEG_BIG)

    q = q_ref[...]
    head_lane = _head_masks(W_ATT)
    k_t = jnp.concatenate([k0_ref[...], k1_ref[...], k2_ref[...]], axis=1)
    vcat = jnp.concatenate([v0_ref[...], v1_ref[...], v2_ref[...]], axis=0)
    parts = [p * BAND_TQ for p in range(tq // BAND_TQ)]
    starts = [r0 + (n_kb - 1) * tq - BAND_BACK for r0 in parts]
    scores = []
    for p in range(len(parts)):
        qp = q[parts[p]:parts[p] + BAND_TQ]
        k_win = k_t[:, starts[p]:starts[p] + BAND_WIN]
        scores.append([_dot(jnp.where(head_lane[h], qp, jnp.zeros_like(qp)), k_win)
                       for h in range(N_HEADS)])
    for p in range(len(parts)):
        exists = (i - (n_kb - 1)) * tq + starts[p] + cols >= 0
        weights = []
        inv_den = []
        for h, s in enumerate(scores[p]):
            s = jnp.where(exists, s + bias_ref[h], NEG_BIG)
            e = jnp.exp2(s - _row_reduce(s, jnp.maximum, jnp.max))
            weights.append(e.astype(BF16))
            inv_den.append(1.0 / _row_reduce(e, jnp.add, jnp.sum))
        v_win = vcat[starts[p]:starts[p] + BAND_WIN]
        out = jnp.zeros((BAND_TQ, W_ATT), F32)
        for h in range(N_HEADS):
            out = jnp.where(head_lane[h], _dot(weights[h], v_win) * inv_den[h], out)
        o_ref[parts[p]:parts[p] + BAND_TQ, :] = out.astype(o_ref.dtype)


def _band_prompt(q16, kt16, v16, table, *, tq):
    t = q16.shape[0]
    assert t % tq == 0 and tq % BAND_TQ == 0 and 2 * tq >= BAND_BACK
    blk = lambda back: pl.BlockSpec((tq, W_ATT), lambda i: (jnp.maximum(i - back, 0), 0))
    blk_t = lambda back: pl.BlockSpec((W_ATT, tq), lambda i: (0, jnp.maximum(i - back, 0)))
    return pl.pallas_call(
        functools.partial(_band_prompt_kernel, tq=tq),
        grid=(t // tq,),
        in_specs=[pl.BlockSpec(memory_space=pltpu.SMEM),
                  blk(0), blk_t(2), blk_t(1), blk_t(0), blk(2), blk(1), blk(0)],
        out_specs=blk(0),
        out_shape=jax.ShapeDtypeStruct((t, W_ATT), BF16),
        scratch_shapes=[pltpu.VMEM((N_HEADS, BAND_TQ, BAND_WIN), F32)],
        compiler_params=_cparams(1),
        name="band_prompt",
    )(table, q16, kt16, kt16, kt16, v16, v16, v16)


def _band_sample_kernel(table_ref, q_ref, kn_ref, vn_ref, ckt_ref, cvt_ref, o_ref, bias_ref, *, ts, past):
    lb = ckt_ref.shape[3]

    @pl.when(pl.program_id(0) == 0)
    def _():
        for h in range(N_HEADS):
            bias_ref[h] = _toeplitz_bias(table_ref, h, ts, lb, bias_ref.shape[2])

    q = q_ref[...]
    head_lane = _head_masks(W_ATT)
    kc_t = ckt_ref[0, 0].astype(BF16)
    vc_t = cvt_ref[0, 0].astype(BF16)
    kn = kn_ref[...]
    vn = vn_ref[...]
    q_pos_c = past + _iota((ts, lb), 0)
    k_pos_c = past - lb + _iota((ts, lb), 1)
    q_pos_n = past + _iota((ts, ts), 0)
    k_pos_n = past + _iota((ts, ts), 1)

    def allowed(q_pos, k_pos):
        qc, kc_ = q_pos // CHUNK, k_pos // CHUNK
        return jnp.logical_and(kc_ >= qc - N_PREV_CHUNKS, kc_ <= qc)

    ok_c = allowed(q_pos_c, k_pos_c)
    ok_n = allowed(q_pos_n, k_pos_n)
    out = jnp.zeros((ts, W_ATT), F32)
    for h in range(N_HEADS):
        qh = jnp.where(head_lane[h], q, jnp.zeros_like(q))
        bias = bias_ref[h]
        s_c = jnp.where(ok_c, _dot(qh, kc_t) + bias[:, :lb], NEG_BIG)
        s_n = jnp.where(ok_n, _dot_nt(qh, kn) + bias[:, lb:lb + ts], NEG_BIG)
        m = jnp.maximum(jnp.max(s_c, axis=-1, keepdims=True), jnp.max(s_n, axis=-1, keepdims=True))
        e_c = jnp.exp2(s_c - m)
        e_n = jnp.exp2(s_n - m)
        den = jnp.sum(e_c, axis=-1, keepdims=True) + jnp.sum(e_n, axis=-1, keepdims=True)
        pv = (_dot_nt(e_c.astype(BF16), vc_t) + _dot(e_n.astype(BF16), vn)) / den
        out = jnp.where(head_lane[h], pv, out)
    o_ref[...] = out.astype(o_ref.dtype)


def _band_sample(q16, kn16, vn16, cache_kt, cache_vt, table, *, layer, nb, ts, past):
    lb = cache_kt.shape[3]
    row = pl.BlockSpec((ts, W_ATT), lambda b: (b, 0))
    cache = pl.BlockSpec((1, 1, W_ATT, lb), lambda b: (layer, b, 0, 0))
    return pl.pallas_call(
        functools.partial(_band_sample_kernel, ts=ts, past=past),
        grid=(nb,),
        in_specs=[pl.BlockSpec(memory_space=pltpu.SMEM), row, row, row, cache, cache],
        out_specs=row,
        out_shape=jax.ShapeDtypeStruct((nb * ts, W_ATT), BF16),
        scratch_shapes=[pltpu.VMEM((N_HEADS, ts, pl.cdiv(lb + 2 * ts, 128) * 128), F32)],
        compiler_params=_cparams(1),
        name="band_sample",
    )(table, q16, kn16, vn16, cache_kt, cache_vt)


def _hgrn_spans(q, k, lf, v, st_ref):
    tr = q.shape[0]
    heads = [slice(h * DK_C, (h + 1) * DK_C) for h in range(N_HEADS)]
    tri = jnp.where(_iota((tr, tr), 0) >= _iota((tr, tr), 1), 1.0, 0.0).astype(BF16)
    f1, f2, f3 = _split3(lf)
    g = (_dot(tri, f1) + _dot(tri, f2) + _dot(tri, f3)) * math.log2(math.e)
    v16 = v.astype(BF16)

    g_end = g[tr - 1:tr, :]
    qt = (q * jnp.exp2(g)).astype(BF16)
    kd = (k * jnp.exp2(g_end - g)).astype(BF16)
    dec = jnp.exp2(g_end)
    states = [st_ref[h] for h in range(N_HEADS)]
    o_heads = [_dot_nt(qt[:, hs], st.astype(BF16)) for hs, st in zip(heads, states)]
    grown = [_dot_tn(v16[:, hs], kd[:, hs]) for hs in heads]
    for h, hs in enumerate(heads):
        st_ref[h] = states[h] * dec[:, hs] + grown[h]
    o = jnp.concatenate(o_heads, axis=1)

    sizes = []
    h = HGRN_DIAG
    while h < tr:
        sizes.append(h)
        h *= 2
    split = lambda x, h: x.reshape(tr // (2 * h), 2 * h, W_C)
    operands = []
    for h in sizes:
        g3, q3, k3, v3 = split(g, h), split(q, h), split(k, h), split(v16, h)
        c = g3[:, h - 1:h, :]
        qe = (q3[:, h:] * jnp.exp2(g3[:, h:] - c)).reshape(tr // 2, W_C).astype(BF16)
        ke = (k3[:, :h] * jnp.exp2(c - g3[:, :h])).reshape(tr // 2, W_C).astype(BF16)
        operands.append((qe, ke, v3[:, :h].reshape(tr // 2, W_C)))
    pair = [[_dot_nt(qe[:, hs], ke[:, hs]) for hs in heads] for qe, ke, _ in operands]
    span_of_row = _iota((tr // 2, tr // 2), 0)
    span_of_col = _iota((tr // 2, tr // 2), 1)
    for n, h in enumerate(sizes):
        if 2 * h < tr:
            same_span = (span_of_row // h) == (span_of_col // h)
            pair[n] = [jnp.where(same_span, p, 0.0) for p in pair[n]]
    added = [jnp.concatenate([_dot(p.astype(BF16), vl[:, hs]) for p, hs in zip(pair[n], heads)],
                             axis=1) for n, (_, _, vl) in enumerate(operands)]
    for n, h in enumerate(sizes):
        o3 = split(o, h)
        upper = o3[:, h:] + added[n].reshape(tr // (2 * h), h, W_C)
        o = jnp.concatenate([o3[:, :h], upper], axis=1).reshape(tr, W_C)
    return o, g


def _hgrn_diagonal(g2, q2, k2, v2):
    n = HGRN_DIAG
    lane = _iota((n, DK_C), 1)
    weights = [[jnp.zeros((n, DK_C), F32) for _ in range(N_HEADS)] for _ in range(2)]
    for s in range(2):
        g8, q8, k8 = g2[s * n:(s + 1) * n], q2[s * n:(s + 1) * n], k2[s * n:(s + 1) * n]
        for j in range(n):
            p = q8 * (k8[j:j + 1, :] * jnp.exp2(g8 - g8[j:j + 1, :]))
            for h in range(N_HEADS):
                a = jnp.sum(p[:, h * DK_C:(h + 1) * DK_C], axis=-1, keepdims=True)
                weights[s][h] = jnp.where(lane == s * n + j, a, weights[s][h])
    row = _iota((2 * n, DK_C), 0)
    col = _iota((2 * n, DK_C), 1)
    keep = jnp.logical_and(row >= col, row // n == col // n)
    v16 = v2.astype(BF16)
    out = []
    for h in range(N_HEADS):
        a = jnp.where(keep, jnp.concatenate([weights[0][h], weights[1][h]], axis=0), 0.0)
        out.append(_dot(a[:, :2 * n].astype(BF16), v16[:, h * DK_C:(h + 1) * DK_C]))
    return jnp.concatenate(out, axis=1)


def _hgrn_block(q, k, lf, v, st_ref):
    o, g = _hgrn_spans(q, k, lf, v, st_ref)
    pairs = [slice(r0, r0 + 2 * HGRN_DIAG) for r0 in range(0, q.shape[0], 2 * HGRN_DIAG)]
    return o + jnp.concatenate([_hgrn_diagonal(g[s], q[s], k[s], v[s]) for s in pairs], axis=0)


def _hgrn_kernel(q_ref, k_ref, lf_ref, v_ref, s0_ref, o_ref, sout_ref, st_ref):
    r = pl.program_id(1)

    @pl.when(r == 0)
    def _():
        for h in range(N_HEADS):
            st_ref[h] = s0_ref[0, h].T

    o_ref[...] = _hgrn_block(q_ref[...], k_ref[...], lf_ref[...], v_ref[...], st_ref)

    @pl.when(r == pl.num_programs(1) - 1)
    def _():
        for h in range(N_HEADS):
            sout_ref[0, h] = st_ref[h].T


def _hgrn(qc, kc, lf, ic, s0, *, nb, t, tr):
    assert t % tr == 0 and tr % (2 * HGRN_DIAG) == 0 and tr & (tr - 1) == 0
    nr = t // tr
    row = pl.BlockSpec((tr, W_C), lambda b, r: (b * nr + r, 0))
    state = pl.BlockSpec((1, N_HEADS, DK_C, DK_C), lambda b, r: (b, 0, 0, 0))
    return pl.pallas_call(
        _hgrn_kernel,
        grid=(nb, nr),
        in_specs=[row, row, row, row, state],
        out_specs=[row, state],
        out_shape=[jax.ShapeDtypeStruct((nb * t, W_C), F32),
                   jax.ShapeDtypeStruct((nb, N_HEADS, DK_C, DK_C), F32)],
        scratch_shapes=[pltpu.VMEM((N_HEADS, DK_C, DK_C), F32)],
        compiler_params=_cparams(2),
        name="hgrn",
    )(qc, kc, lf, ic, s0)


def _merge_block(x, oa, ob, oc, gc, on, wo_ref, nf, wu_ref, wd_ref):
    parts = []
    for h in range(N_HEADS):
        och = oc[:, h * DK_C:(h + 1) * DK_C]
        ms = jnp.mean(och * och, axis=-1, keepdims=True)
        parts.append(och * lax.rsqrt(ms + EPS) * on)
    ocn = jnp.concatenate(parts, axis=-1) * (gc * (1.0 / (1.0 + jnp.exp(-gc))))
    mixed = jnp.concatenate([oa, ob, ocn.astype(BF16)], axis=-1)
    h_res = x + _dot(mixed, wo_ref[0])
    ms = jnp.mean(h_res * h_res, axis=-1, keepdims=True)
    hn = (h_res * lax.rsqrt(ms + EPS) * nf).astype(BF16)
    y = h_res
    for c in range(0, wu_ref.shape[2], FFN_CHUNK):
        u = jnp.maximum(_dot(hn, wu_ref[0, :, c:c + FFN_CHUNK]), 0.0)
        y = y + _dot((u * u).astype(BF16), wd_ref[0, c:c + FFN_CHUNK, :])
    return y


def _merge_kernel(x_ref, oa_ref, ob_ref, oc_ref, gc_ref, on_ref, wo_ref, nf_ref, wu_ref, wd_ref,
                  y_ref):
    y_ref[...] = _merge_block(x_ref[...], oa_ref[...], ob_ref[...], oc_ref[...], gc_ref[...],
                              on_ref[...], wo_ref, nf_ref[...], wu_ref, wd_ref)


def _merge(x2d, oa, ob, oc, gc, onorm, wo16, nf, wu16, wd16, *, layer, tm):
    m, d = x2d.shape
    assert m % tm == 0
    row = lambda w: pl.BlockSpec((tm, w), lambda i: (i, 0))
    full = _resident
    return pl.pallas_call(
        _merge_kernel,
        grid=(m // tm,),
        in_specs=[row(d), row(W_ATT), row(W_ATT), row(W_C), row(W_C),
                  full(onorm), _resident_layer(wo16, layer), full(nf),
                  _resident_layer(wu16, layer), _resident_layer(wd16, layer)],
        out_specs=row(d),
        out_shape=jax.ShapeDtypeStruct((m, d), F32),
        compiler_params=_cparams(1),
        name="merge_ffn",
    )(x2d, oa, ob, oc, gc, onorm, wo16, nf, wu16, wd16)


def kernel(x_prompt, x_sample, cache_a_k, cache_a_v, cache_b_k, cache_b_v, state_c, norm_mix, w_in, qnorm_a, knorm_a, qnorm_b, knorm_b, rel_bias_b, lower_bounds, onorm_c, w_o, norm_ffn, w_up, w_down):
    depth = w_in.shape[0]
    bp, tp, d = x_prompt.shape
    bs, ts, _ = x_sample.shape
    past = cache_a_k.shape[2]
    band_rows_p = min(N_PREV_CHUNKS * CHUNK, tp)
    assert bp == 1

    xp = x_prompt.reshape(bp * tp, d)
    xs = x_sample.reshape(bs * ts, d)
    time_minor = lambda c: jnp.transpose(c, (0, 1, 3, 4, 2)).reshape(depth, bs, W_ATT, c.shape[2])
    cak, cav, cbk, cbv = (time_minor(c) for c in (cache_a_k, cache_a_v, cache_b_k, cache_b_v))
    heads_of = lambda a_t, n: jnp.transpose(
        a_t.reshape(N_HEADS, D_HEAD, n, a_t.shape[1] // n), (2, 3, 0, 1))
    lbounds = lower_bounds.astype(F32)
    w16, wo16, wu16, wd16 = (w.astype(BF16) for w in (w_in, w_o, w_up, w_down))
    zero_state = jnp.zeros((bp, N_HEADS, DK_C, DK_C), F32)

    tm_p = min(512, tp)
    tq_band = min(256, tp)
    tr_p = min(256, tp)

    outs_p = [[] for _ in range(5)]
    outs_s = [[] for _ in range(5)]
    carried = ()
    for l in range(depth):
        nm = norm_mix[l].reshape(1, d)
        nf = norm_ffn[l].reshape(1, d)
        tile_h = lambda g: jnp.tile(g.reshape(1, D_HEAD), (1, N_HEADS))
        qna, kna, qnb, knb = (tile_h(g[l]) for g in (qnorm_a, knorm_a, qnorm_b, knorm_b))
        onorm = onorm_c[l].reshape(1, DK_C)
        table = rel_bias_b[l].astype(F32)

        (qa, ka_all, va_all, ka16_t, va16, qb, kb_t, vb_t, kb16_t, vb16, qc, kc, lf, ic, gc) = _proj(
            xp, nm, w16, qna, kna, qnb, knb, lbounds, layer=l, tm=tm_p, time_minor=True,
            carried=carried)
        carried = (ka_all, va_all)
        oa = _sb_prompt(qa, ka16_t, va16, tq=min(SB_TQ, tp))
        ob = _band_prompt(qb, kb16_t, vb16, table, tq=tq_band)
        oc, s_p = _hgrn(qc, kc, lf, ic, zero_state, nb=bp, t=tp, tr=tr_p)
        xp = _merge(xp, oa, ob, oc, gc, onorm, wo16, nf, wu16, wd16, layer=l, tm=tm_p)
        outs_p[2].append(heads_of(kb_t, bp)[:, tp - band_rows_p:])
        outs_p[3].append(heads_of(vb_t, bp)[:, tp - band_rows_p:])
        outs_p[4].append(s_p)

        head = lambda a, n: a.reshape(n, -1, N_HEADS, D_HEAD)
        (qa, ka, va, ka16, va16, qb, kb, vb, kb16, vb16, qc, kc, lf, ic, gc) = _proj(
            xs, nm, w16, qna, kna, qnb, knb, lbounds, layer=l, tm=bs * ts, time_minor=False)
        oa = _sb_sample(qa, ka16, va16, cak, cav, layer=l, nb=bs, ts=ts)
        ob = _band_sample(qb, kb16, vb16, cbk, cbv, table, layer=l, nb=bs, ts=ts, past=past)
        oc, s_s = _hgrn(qc, kc, lf, ic, state_c[l].astype(F32), nb=bs, t=ts, tr=ts)
        xs = _merge(xs, oa, ob, oc, gc, onorm, wo16, nf, wu16, wd16, layer=l, tm=bs * ts)
        outs_s[0].append(head(ka, bs))
        outs_s[1].append(head(va, bs))
        outs_s[2].append(head(kb, bs))
        outs_s[3].append(head(vb, bs))
        outs_s[4].append(s_s)

    stack = lambda xs_: jnp.stack(xs_)
    all_heads_of = lambda a: jnp.transpose(
        a.reshape(depth, N_HEADS, D_HEAD, bp, tp), (0, 3, 4, 1, 2))
    ka_all, va_all = carried
    return (xp.reshape(bp, tp, d), xs.reshape(bs, ts, d),
            all_heads_of(ka_all), all_heads_of(va_all),
            stack(outs_p[2]), stack(outs_p[3]), stack(outs_p[4]),
            stack(outs_s[0]), stack(outs_s[1]), stack(outs_s[2]), stack(outs_s[3]), stack(outs_s[4]))
```

```python
import functools
import math

import jax
import jax.numpy as jnp
from jax import lax
from jax.experimental import pallas as pl
from jax.experimental.pallas import tpu as pltpu

F32 = jnp.float32
BF16 = jnp.bfloat16

D_HEAD = 64
N_HEADS = 4
W_ATT = N_HEADS * D_HEAD
DK_C = 128
W_C = N_HEADS * DK_C
CHUNK = 64
N_PREV_CHUNKS = 8
REL_CLIP = 128
N_REL = 2 * REL_CLIP + 1
EPS = 1e-6
NEG_BIG = -1e30
LB_FLOOR = 1e-30
LOG2E = math.log2(math.e)
SB_UNDERFLOW = 88.0
PROJ_PART = 256
FFN_CHUNK = 1024
SB_TK = 128
SB_TQ = 256
HGRN_DIAG = 8
VMEM_LIMIT = 56 * 1024 * 1024


def _cparams(n_axes):
    return pltpu.CompilerParams(dimension_semantics=("arbitrary",) * n_axes,
                                vmem_limit_bytes=VMEM_LIMIT)


def _resident(a):
    zeros = (0,) * a.ndim
    return pl.BlockSpec(a.shape, lambda *_: zeros, pipeline_mode=pl.Buffered(1))


def _resident_layer(a, layer):
    index = (layer,) + (0,) * (a.ndim - 1)
    return pl.BlockSpec((1,) + a.shape[1:], lambda *_: index, pipeline_mode=pl.Buffered(1))


def _split3(x):
    h1 = x.astype(BF16)
    r1 = x - h1.astype(F32)
    h2 = r1.astype(BF16)
    h3 = (r1 - h2.astype(F32)).astype(BF16)
    return h1, h2, h3


def _dot(a, b):
    return jnp.dot(a, b, preferred_element_type=F32)


def _dot_nt(a, b):
    return lax.dot_general(a, b, (((1,), (1,)), ((), ())), preferred_element_type=F32)


def _dot_tn(a, b):
    return lax.dot_general(a, b, (((0,), (0,)), ((), ())), preferred_element_type=F32)


def _iota(shape, dim):
    return lax.broadcasted_iota(jnp.int32, shape, dim)


def _proj_kernel(x_ref, nm_ref, w_ref, qna_ref, kna_ref, qnb_ref, knb_ref, lb_ref, *rest,
                 layer, time_minor):
    (qa_ref, ka_ref, va_ref, ka16_ref, va16_ref, qb_ref, kb_ref, vb_ref, kb16_ref, vb16_ref,
     qc_ref, kc_ref, lf_ref, ic_ref, gc_ref) = rest[-15:]
    if time_minor:
        for earlier_ref, all_ref in zip(rest[:-15], (ka_ref, va_ref)):
            all_ref[0:layer] = earlier_ref[...]
        ka_ref, va_ref = ka_ref.at[layer], va_ref.at[layer]
    x = x_ref[...]
    ms = jnp.mean(x * x, axis=-1, keepdims=True)
    xn = (x * lax.rsqrt(ms + EPS) * nm_ref[...]).astype(BF16)

    tm = x.shape[0]
    part = PROJ_PART if tm % PROJ_PART == 0 else tm
    projected = [_dot(xn[r0:r0 + part], w_ref[0]) for r0 in range(0, tm, part)]

    same_head = (_iota((W_ATT, W_ATT), 0) // D_HEAD) == (_iota((W_ATT, W_ATT), 1) // D_HEAD)
    seg_mean = jnp.where(same_head, 1.0 / D_HEAD, 0.0).astype(BF16)
    inv_sqrt_d = 1.0 / math.sqrt(D_HEAD)

    lbr = lb_ref[...]
    e = jnp.exp(lbr - jnp.max(lbr, axis=0, keepdims=True))
    sm = e / jnp.sum(e, axis=0, keepdims=True)
    lb = jnp.sum(sm[0:layer + 1], axis=0, keepdims=True) - sm[0:1]
    log_lb = jnp.log(jnp.maximum(lb, LB_FLOOR))
    log_1m_lb = jnp.log1p(-lb)

    def head_norm(p, g_ref):
        m = _dot((p * p).astype(BF16), seg_mean)
        return p * lax.rsqrt(m + EPS) * g_ref[...]

    for n, p in enumerate(projected):
        rows = slice(n * part, (n + 1) * part)

        def put(x, f32_ref, b16_ref, b16_time_minor):
            x_t = x.T if time_minor else None
            if time_minor:
                f32_ref[:, rows] = x_t
            else:
                f32_ref[rows, :] = x
            if time_minor and b16_time_minor:
                b16_ref[:, rows] = x_t.astype(BF16)
            else:
                b16_ref[rows, :] = x.astype(BF16)

        o = 0
        qa_ref[rows, :] = (head_norm(p[:, o:o + W_ATT], qna_ref) * inv_sqrt_d).astype(BF16)
        o += W_ATT
        put(head_norm(p[:, o:o + W_ATT], kna_ref), ka_ref, ka16_ref, True)
        o += W_ATT
        put(p[:, o:o + W_ATT], va_ref, va16_ref, False)
        o += W_ATT
        qb_ref[rows, :] = (head_norm(p[:, o:o + W_ATT], qnb_ref)
                           * (inv_sqrt_d * LOG2E)).astype(BF16)
        o += W_ATT
        put(head_norm(p[:, o:o + W_ATT], knb_ref), kb_ref, kb16_ref, True)
        o += W_ATT
        put(p[:, o:o + W_ATT], vb_ref, vb16_ref, False)
        o += W_ATT
        qc_ref[rows, :] = p[:, o:o + W_C] * (DK_C ** -0.5)
        o += W_C
        f_raw = p[:, o:o + W_C]
        o += W_C
        log_sig = jnp.minimum(f_raw, 0.0) - jnp.log(1.0 + jnp.exp(-jnp.abs(f_raw)))
        b = log_1m_lb + log_sig
        lf_ref[rows, :] = jnp.maximum(log_lb, b) + jnp.log(1.0 + jnp.exp(-jnp.abs(log_lb - b)))
        kc_ref[rows, :] = (1.0 - lb) * (1.0 / (1.0 + jnp.exp(f_raw)))
        ic_ref[rows, :] = p[:, o:o + W_C]
        o += W_C
        gc_ref[rows, :] = p[:, o:o + W_C]


def _proj(x2d, nm, w16, qna, kna, qnb, knb, lower_bounds, *, layer, tm, time_minor, carried=()):
    m, d = x2d.shape
    assert m % tm == 0 and len(carried) == (2 if time_minor and layer > 0 else 0)
    full = _resident
    row = lambda w, dt: (pl.BlockSpec((tm, w), lambda i: (i, 0)), jax.ShapeDtypeStruct((m, w), dt))
    col = lambda w, dt: (pl.BlockSpec((w, tm), lambda i: (0, i)), jax.ShapeDtypeStruct((w, m), dt))
    layers = lambda n, w: pl.BlockSpec((n, w, tm), lambda i: (0, 0, i))
    layered = lambda w, dt: (layers(layer + 1, w), jax.ShapeDtypeStruct((layer + 1, w, m), dt))
    kv = col if time_minor else row
    kv_a = layered if time_minor else row
    att_a = [row(W_ATT, BF16), kv_a(W_ATT, F32), kv_a(W_ATT, F32), kv(W_ATT, BF16), row(W_ATT, BF16)]
    att_b = [row(W_ATT, BF16), kv(W_ATT, F32), kv(W_ATT, F32), kv(W_ATT, BF16), row(W_ATT, BF16)]
    outs = att_a + att_b + [row(W_C, F32)] * 5
    return pl.pallas_call(
        functools.partial(_proj_kernel, layer=layer, time_minor=time_minor),
        grid=(m // tm,),
        in_specs=[row(d, F32)[0], full(nm), _resident_layer(w16, layer), full(qna), full(kna),
                  full(qnb), full(knb),
                  full(lower_bounds)] + [layers(layer, W_ATT)] * len(carried),
        out_specs=[spec for spec, _ in outs],
        out_shape=[shape for _, shape in outs],
        compiler_params=_cparams(1),
        name="proj",
    )(x2d, nm, w16, qna, kna, qnb, knb, lower_bounds, *carried)


def _head_masks(width):
    lane_head = _iota((1, width), 1) // D_HEAD
    return [lane_head == h for h in range(N_HEADS)]


def _stack_heads(x, head_lane):
    return jnp.concatenate([jnp.where(m, x, jnp.zeros_like(x)) for m in head_lane], axis=0)


def _stack_heads_t(x_t, n):
    row_head = _iota((W_ATT, 1), 0) // D_HEAD
    return jnp.concatenate(
        [jnp.where(row_head == h, x_t, jnp.zeros_like(x_t)) for h in range(N_HEADS)], axis=1)


def _stack_groups(x, head_lane, gq):
    return jnp.concatenate(
        [_stack_heads(x[g:g + gq], head_lane) for g in range(0, x.shape[0], gq)], axis=0)


def _sb_block(z, weigh_values, mask, tri, carry_ref, acc_ref, gq):
    tk = z.shape[1]
    t = jnp.log(1.0 + jnp.exp2(jnp.abs(z) * -math.log2(math.e)))
    log_1m = jnp.minimum(-z, 0.0) - t
    log_beta = log_1m + z
    if mask is not None:
        log_1m = jnp.where(mask, log_1m, 0.0)
    hi = log_1m.astype(BF16)
    lo = (log_1m - hi.astype(F32)).astype(BF16)
    later = _dot(jnp.concatenate([hi, lo], axis=1), tri)
    carry = carry_ref[...]
    w = jnp.exp(log_beta + later + carry[:, :tk])
    if mask is not None:
        w = jnp.where(mask, w, 0.0)
    w = w.astype(BF16)
    blocks = [w[n * gq:(n + 1) * gq] for n in range(z.shape[0] // gq)]
    w_heads = jnp.concatenate(
        [jnp.concatenate(blocks[g:g + N_HEADS], axis=1) for g in range(0, len(blocks), N_HEADS)],
        axis=0)
    acc_ref[...] += weigh_values(w_heads)
    carry_ref[...] = carry + jnp.sum(log_1m, axis=-1, keepdims=True)


def _sb_live(carry_ref):
    return (jnp.max(carry_ref[...]) > -SB_UNDERFLOW).astype(jnp.int32)


def _strict_upper(n):
    return jnp.where(_iota((2 * n, n), 0) % n > _iota((2 * n, n), 1), 1.0, 0.0).astype(BF16)


def _sb_prompt_kernel(q_ref, kt_ref, v_ref, o_ref, carry_ref, acc_ref, *, tq):
    i = pl.program_id(0)
    head_lane = _head_masks(W_ATT)
    gq = SB_TK
    n_groups = tq // gq
    q_stack = _stack_groups(q_ref[...], head_lane, gq)
    tri = _strict_upper(SB_TK)
    carry_ref[...] = jnp.zeros_like(carry_ref)
    acc_ref[...] = jnp.zeros_like(acc_ref)

    def add_block(j, first_group, mask):
        r0 = pl.multiple_of(j * SB_TK, SB_TK)
        s0 = first_group * N_HEADS * gq
        z = _dot(q_stack[s0:], kt_ref[:, pl.ds(r0, SB_TK)])
        v_stack = _stack_heads(v_ref[pl.ds(r0, SB_TK), :], head_lane)
        _sb_block(z, lambda w: _dot(w, v_stack), mask, tri,
                  carry_ref.at[pl.ds(s0, z.shape[0])],
                  acc_ref.at[pl.ds(first_group * gq, (n_groups - first_group) * gq)], gq)

    for g in reversed(range(n_groups)):
        shape = ((n_groups - g) * N_HEADS * gq, SB_TK)
        row = _iota(shape, 0)
        add_block(i * n_groups + g, g, jnp.logical_or(_iota(shape, 1) < row % gq, row >= N_HEADS * gq))
    add_block(jnp.maximum(i * n_groups - 1, 0), 0,
              jnp.broadcast_to(i > 0, (n_groups * N_HEADS * gq, SB_TK)))

    def cond(c):
        j, live = c
        return jnp.logical_and(j >= 0, live > 0)

    def body(c):
        j, _ = c
        add_block(j, 0, None)
        return j - 1, _sb_live(carry_ref)

    lax.while_loop(cond, body, (i * n_groups - 2, _sb_live(carry_ref)))
    o_ref[...] = acc_ref[...].astype(o_ref.dtype)


def _sb_prompt(q16, kt16, v16, *, tq):
    t = q16.shape[0]
    assert t % tq == 0 and tq % SB_TK == 0
    return pl.pallas_call(
        functools.partial(_sb_prompt_kernel, tq=tq),
        grid=(t // tq,),
        in_specs=[pl.BlockSpec((tq, W_ATT), lambda i: (i, 0)),
                  _resident(kt16), _resident(v16)],
        out_specs=pl.BlockSpec((tq, W_ATT), lambda i: (i, 0)),
        out_shape=jax.ShapeDtypeStruct((t, W_ATT), BF16),
        scratch_shapes=[pltpu.VMEM((N_HEADS * tq, SB_TK), F32), pltpu.VMEM((tq, W_ATT), F32)],
        compiler_params=_cparams(1),
        name="sb_prompt",
    )(q16, kt16, v16)


def _sb_sample_kernel(q_ref, kn_ref, vn_ref, ckt_ref, cvt_ref, o_ref, carry_ref, acc_ref, *, ts, past):
    head_lane = _head_masks(W_ATT)
    q_stack = _stack_heads(q_ref[...], head_lane)
    carry_ref[...] = jnp.zeros_like(carry_ref)
    acc_ref[...] = jnp.zeros_like(acc_ref)
    shape = (N_HEADS * ts, ts)
    mask = _iota(shape, 1) < _iota(shape, 0) % ts
    v_new = _stack_heads(vn_ref[...], head_lane)
    _sb_block(_dot_nt(q_stack, kn_ref[...]), lambda w: _dot(w, v_new), mask, _strict_upper(ts),
              carry_ref, acc_ref, ts)
    tri = _strict_upper(SB_TK)

    def cond(c):
        j, live = c
        return jnp.logical_and(j >= 0, live > 0)

    def body(c):
        j, _ = c
        r0 = pl.multiple_of(j * SB_TK, SB_TK)
        z = _dot(q_stack, ckt_ref[0, 0, :, pl.ds(r0, SB_TK)].astype(BF16))
        vt_stack = _stack_heads_t(cvt_ref[0, 0, :, pl.ds(r0, SB_TK)].astype(BF16), SB_TK)
        _sb_block(z, lambda w: _dot_nt(w, vt_stack), None, tri, carry_ref, acc_ref, ts)
        return j - 1, _sb_live(carry_ref)

    lax.while_loop(cond, body, (past // SB_TK - 1, _sb_live(carry_ref)))
    o_ref[...] = acc_ref[...].astype(o_ref.dtype)


def _sb_sample(q16, kn16, vn16, cache_kt, cache_vt, *, layer, nb, ts):
    past = cache_kt.shape[3]
    assert past % SB_TK == 0
    row = pl.BlockSpec((ts, W_ATT), lambda b: (b, 0))
    cache = pl.BlockSpec((1, 1, W_ATT, past), lambda b: (layer, b, 0, 0))
    return pl.pallas_call(
        functools.partial(_sb_sample_kernel, ts=ts, past=past),
        grid=(nb,),
        in_specs=[row, row, row, cache, cache],
        out_specs=row,
        out_shape=jax.ShapeDtypeStruct((nb * ts, W_ATT), BF16),
        scratch_shapes=[pltpu.VMEM((N_HEADS * ts, SB_TK), F32), pltpu.VMEM((ts, W_ATT), F32)],
        compiler_params=_cparams(1),
        name="sb_sample",
    )(q16, kn16, vn16, cache_kt, cache_vt)


def _rel_bias(table_ref, h, dist):
    idx = jnp.clip(dist, -REL_CLIP, REL_CLIP) + REL_CLIP

    def body(r, b):
        return jnp.where(idx == r, table_ref[h, r], b)

    return lax.fori_loop(0, N_REL, body, jnp.zeros(dist.shape, F32))


def _row_reduce(x, combine, reduce):
    width = x.shape[1]
    acc = x[:, :128]
    for c in range(128, width, 128):
        acc = combine(acc, x[:, c:c + 128])
    return reduce(acc, axis=-1, keepdims=True)


def _toeplitz_bias(table_ref, h, rows, back, width):
    lane = _iota((8, width), 1)
    by_diag = _rel_bias(table_ref, h, back + rows - lane) * LOG2E
    tiled = jnp.concatenate([by_diag] * (rows // 8), axis=0)
    return pltpu.roll(tiled, width - rows, 1, stride=1, stride_axis=0)


BAND_TQ = 128
BAND_BACK = N_PREV_CHUNKS * CHUNK
BAND_WIN = BAND_BACK + BAND_TQ


def _band_prompt_kernel(table_ref, q_ref, k0_ref, k1_ref, k2_ref, v0_ref, v1_ref, v2_ref,
                        o_ref, bias_ref, *, tq):
    i = pl.program_id(0)
    n_kb = 3
    rows = _iota((BAND_TQ, BAND_WIN), 0)
    cols = _iota((BAND_TQ, BAND_WIN), 1)

    @pl.when(i == 0)
    def _():
        q_chunk = rows // CHUNK + N_PREV_CHUNKS
        k_chunk = cols // CHUNK
        in_band = jnp.logical_and(k_chunk >= q_chunk - N_PREV_CHUNKS, k_chunk <= q_chunk)
        for h in range(N_HEADS):
            bias = _toeplitz_bias(table_ref, h, BAND_TQ, BAND_BACK, BAND_WIN + BAND_TQ)
            bias_ref[h] = jnp.where(in_band, bias[:, :BAND_WIN], NEG_BIG)

    q = q_ref[...]
    head_lane = _head_masks(W_ATT)
    k_t = jnp.concatenate([k0_ref[...], k1_ref[...], k2_ref[...]], axis=1)
    vcat = jnp.concatenate([v0_ref[...], v1_ref[...], v2_ref[...]], axis=0)
    parts = [p * BAND_TQ for p in range(tq // BAND_TQ)]
    starts = [r0 + (n_kb - 1) * tq - BAND_BACK for r0 in parts]
    scores = []
    for p in range(len(parts)):
        qp = q[parts[p]:parts[p] + BAND_TQ]
        k_win = k_t[:, starts[p]:starts[p] + BAND_WIN]
        scores.append([_dot(jnp.where(head_lane[h], qp, jnp.zeros_like(qp)), k_win)
                       for h in range(N_HEADS)])
    for p in range(len(parts)):
        exists = (i - (n_kb - 1)) * tq + starts[p] + cols >= 0
        weights = []
        inv_den = []
        for h, s in enumerate(scores[p]):
            s = jnp.where(exists, s + bias_ref[h], NEG_BIG)
            e = jnp.exp2(s - _row_reduce(s, jnp.maximum, jnp.max))
            weights.append(e.astype(BF16))
            inv_den.append(1.0 / _row_reduce(e, jnp.add, jnp.sum))
        v_win = vcat[starts[p]:starts[p] + BAND_WIN]
        out = jnp.zeros((BAND_TQ, W_ATT), F32)
        for h in range(N_HEADS):
            out = jnp.where(head_lane[h], _dot(weights[h], v_win) * inv_den[h], out)
        o_ref[parts[p]:parts[p] + BAND_TQ, :] = out.astype(o_ref.dtype)


def _band_prompt(q16, kt16, v16, table, *, tq):
    t = q16.shape[0]
    assert t % tq == 0 and tq % BAND_TQ == 0 and 2 * tq >= BAND_BACK
    blk = lambda back: pl.BlockSpec((tq, W_ATT), lambda i: (jnp.maximum(i - back, 0), 0))
    blk_t = lambda back: pl.BlockSpec((W_ATT, tq), lambda i: (0, jnp.maximum(i - back, 0)))
    return pl.pallas_call(
        functools.partial(_band_prompt_kernel, tq=tq),
        grid=(t // tq,),
        in_specs=[pl.BlockSpec(memory_space=pltpu.SMEM),
                  blk(0), blk_t(2), blk_t(1), blk_t(0), blk(2), blk(1), blk(0)],
        out_specs=blk(0),
        out_shape=jax.ShapeDtypeStruct((t, W_ATT), BF16),
        scratch_shapes=[pltpu.VMEM((N_HEADS, BAND_TQ, BAND_WIN), F32)],
        compiler_params=_cparams(1),
        name="band_prompt",
    )(table, q16, kt16, kt16, kt16, v16, v16, v16)


def _band_sample_kernel(table_ref, q_ref, kn_ref, vn_ref, ckt_ref, cvt_ref, o_ref, bias_ref, *, ts, past):
    lb = ckt_ref.shape[3]

    @pl.when(pl.program_id(0) == 0)
    def _():
        for h in range(N_HEADS):
            bias_ref[h] = _toeplitz_bias(table_ref, h, ts, lb, bias_ref.shape[2])

    q = q_ref[...]
    head_lane = _head_masks(W_ATT)
    kc_t = ckt_ref[0, 0].astype(BF16)
    vc_t = cvt_ref[0, 0].astype(BF16)
    kn = kn_ref[...]
    vn = vn_ref[...]
    q_pos_c = past + _iota((ts, lb), 0)
    k_pos_c = past - lb + _iota((ts, lb), 1)
    q_pos_n = past + _iota((ts, ts), 0)
    k_pos_n = past + _iota((ts, ts), 1)

    def allowed(q_pos, k_pos):
        qc, kc_ = q_pos // CHUNK, k_pos // CHUNK
        return jnp.logical_and(kc_ >= qc - N_PREV_CHUNKS, kc_ <= qc)

    ok_c = allowed(q_pos_c, k_pos_c)
    ok_n = allowed(q_pos_n, k_pos_n)
    out = jnp.zeros((ts, W_ATT), F32)
    for h in range(N_HEADS):
        qh = jnp.where(head_lane[h], q, jnp.zeros_like(q))
        bias = bias_ref[h]
        s_c = jnp.where(ok_c, _dot(qh, kc_t) + bias[:, :lb], NEG_BIG)
        s_n = jnp.where(ok_n, _dot_nt(qh, kn) + bias[:, lb:lb + ts], NEG_BIG)
        m = jnp.maximum(jnp.max(s_c, axis=-1, keepdims=True), jnp.max(s_n, axis=-1, keepdims=True))
        e_c = jnp.exp2(s_c - m)
        e_n = jnp.exp2(s_n - m)
        den = jnp.sum(e_c, axis=-1, keepdims=True) + jnp.sum(e_n, axis=-1, keepdims=True)
        pv = (_dot_nt(e_c.astype(BF16), vc_t) + _dot(e_n.astype(BF16), vn)) / den
        out = jnp.where(head_lane[h], pv, out)
    o_ref[...] = out.astype(o_ref.dtype)


def _band_sample(q16, kn16, vn16, cache_kt, cache_vt, table, *, layer, nb, ts, past):
    lb = cache_kt.shape[3]
    row = pl.BlockSpec((ts, W_ATT), lambda b: (b, 0))
    cache = pl.BlockSpec((1, 1, W_ATT, lb), lambda b: (layer, b, 0, 0))
    return pl.pallas_call(
        functools.partial(_band_sample_kernel, ts=ts, past=past),
        grid=(nb,),
        in_specs=[pl.BlockSpec(memory_space=pltpu.SMEM), row, row, row, cache, cache],
        out_specs=row,
        out_shape=jax.ShapeDtypeStruct((nb * ts, W_ATT), BF16),
        scratch_shapes=[pltpu.VMEM((N_HEADS, ts, pl.cdiv(lb + 2 * ts, 128) * 128), F32)],
        compiler_params=_cparams(1),
        name="band_sample",
    )(table, q16, kn16, vn16, cache_kt, cache_vt)


def _hgrn_spans(q, k, lf, v, st_ref):
    tr = q.shape[0]
    heads = [slice(h * DK_C, (h + 1) * DK_C) for h in range(N_HEADS)]
    tri = jnp.where(_iota((tr, tr), 0) >= _iota((tr, tr), 1), 1.0, 0.0).astype(BF16)
    f1, f2, f3 = _split3(lf)
    g = (_dot(tri, f1) + _dot(tri, f2) + _dot(tri, f3)) * math.log2(math.e)
    v16 = v.astype(BF16)

    g_end = g[tr - 1:tr, :]
    qt = (q * jnp.exp2(g)).astype(BF16)
    kd = (k * jnp.exp2(g_end - g)).astype(BF16)
    dec = jnp.exp2(g_end)
    states = [st_ref[h] for h in range(N_HEADS)]
    o_heads = [_dot_nt(qt[:, hs], st.astype(BF16)) for hs, st in zip(heads, states)]
    grown = [_dot_tn(v16[:, hs], kd[:, hs]) for hs in heads]
    for h, hs in enumerate(heads):
        st_ref[h] = states[h] * dec[:, hs] + grown[h]
    o = jnp.concatenate(o_heads, axis=1)

    sizes = []
    h = HGRN_DIAG
    while h < tr:
        sizes.append(h)
        h *= 2
    split = lambda x, h: x.reshape(tr // (2 * h), 2 * h, W_C)
    operands = []
    for h in sizes:
        g3, q3, k3, v3 = split(g, h), split(q, h), split(k, h), split(v16, h)
        c = g3[:, h - 1:h, :]
        qe = (q3[:, h:] * jnp.exp2(g3[:, h:] - c)).reshape(tr // 2, W_C).astype(BF16)
        ke = (k3[:, :h] * jnp.exp2(c - g3[:, :h])).reshape(tr // 2, W_C).astype(BF16)
        operands.append((qe, ke, v3[:, :h].reshape(tr // 2, W_C)))
    pair = [[_dot_nt(qe[:, hs], ke[:, hs]) for hs in heads] for qe, ke, _ in operands]
    span_of_row = _iota((tr // 2, tr // 2), 0)
    span_of_col = _iota((tr // 2, tr // 2), 1)
    for n, h in enumerate(sizes):
        if 2 * h < tr:
            same_span = (span_of_row // h) == (span_of_col // h)
            pair[n] = [jnp.where(same_span, p, 0.0) for p in pair[n]]
    added = [jnp.concatenate([_dot(p.astype(BF16), vl[:, hs]) for p, hs in zip(pair[n], heads)],
                             axis=1) for n, (_, _, vl) in enumerate(operands)]
    for n, h in enumerate(sizes):
        o3 = split(o, h)
        upper = o3[:, h:] + added[n].reshape(tr // (2 * h), h, W_C)
        o = jnp.concatenate([o3[:, :h], upper], axis=1).reshape(tr, W_C)
    return o, g


def _hgrn_diagonal(g2, q2, k2, v2):
    n = HGRN_DIAG
    lane = _iota((n, DK_C), 1)
    weights = [[jnp.zeros((n, DK_C), F32) for _ in range(N_HEADS)] for _ in range(2)]
    for s in range(2):
        g8, q8, k8 = g2[s * n:(s + 1) * n], q2[s * n:(s + 1) * n], k2[s * n:(s + 1) * n]
        for j in range(n):
            p = q8 * (k8[j:j + 1, :] * jnp.exp2(g8 - g8[j:j + 1, :]))
            for h in range(N_HEADS):
                a = jnp.sum(p[:, h * DK_C:(h + 1) * DK_C], axis=-1, keepdims=True)
                weights[s][h] = jnp.where(lane == s * n + j, a, weights[s][h])
    row = _iota((2 * n, DK_C), 0)
    col = _iota((2 * n, DK_C), 1)
    keep = jnp.logical_and(row >= col, row // n == col // n)
    v16 = v2.astype(BF16)
    out = []
    for h in range(N_HEADS):
        a = jnp.where(keep, jnp.concatenate([weights[0][h], weights[1][h]], axis=0), 0.0)
        out.append(_dot(a[:, :2 * n].astype(BF16), v16[:, h * DK_C:(h + 1) * DK_C]))
    return jnp.concatenate(out, axis=1)


def _hgrn_block(q, k, lf, v, st_ref):
    o, g = _hgrn_spans(q, k, lf, v, st_ref)
    pairs = [slice(r0, r0 + 2 * HGRN_DIAG) for r0 in range(0, q.shape[0], 2 * HGRN_DIAG)]
    return o + jnp.concatenate([_hgrn_diagonal(g[s], q[s], k[s], v[s]) for s in pairs], axis=0)


def _hgrn_kernel(q_ref, k_ref, lf_ref, v_ref, s0_ref, o_ref, sout_ref, st_ref):
    r = pl.program_id(1)

    @pl.when(r == 0)
    def _():
        for h in range(N_HEADS):
            st_ref[h] = s0_ref[0, h].T

    o_ref[...] = _hgrn_block(q_ref[...], k_ref[...], lf_ref[...], v_ref[...], st_ref)

    @pl.when(r == pl.num_programs(1) - 1)
    def _():
        for h in range(N_HEADS):
            sout_ref[0, h] = st_ref[h].T


def _hgrn(qc, kc, lf, ic, s0, *, nb, t, tr):
    assert t % tr == 0 and tr % (2 * HGRN_DIAG) == 0 and tr & (tr - 1) == 0
    nr = t // tr
    row = pl.BlockSpec((tr, W_C), lambda b, r: (b * nr + r, 0))
    state = pl.BlockSpec((1, N_HEADS, DK_C, DK_C), lambda b, r: (b, 0, 0, 0))
    return pl.pallas_call(
        _hgrn_kernel,
        grid=(nb, nr),
        in_specs=[row, row, row, row, state],
        out_specs=[row, state],
        out_shape=[jax.ShapeDtypeStruct((nb * t, W_C), F32),
                   jax.ShapeDtypeStruct((nb, N_HEADS, DK_C, DK_C), F32)],
        scratch_shapes=[pltpu.VMEM((N_HEADS, DK_C, DK_C), F32)],
        compiler_params=_cparams(2),
        name="hgrn",
    )(qc, kc, lf, ic, s0)


def _merge_block(x, oa, ob, oc, gc, on, wo_ref, nf, wu_ref, wd_ref):
    parts = []
    for h in range(N_HEADS):
        och = oc[:, h * DK_C:(h + 1) * DK_C]
        ms = jnp.mean(och * och, axis=-1, keepdims=True)
        parts.append(och * lax.rsqrt(ms + EPS) * on)
    ocn = jnp.concatenate(parts, axis=-1) * (gc * (1.0 / (1.0 + jnp.exp(-gc))))
    mixed = jnp.concatenate([oa, ob, ocn.astype(BF16)], axis=-1)
    h_res = x + _dot(mixed, wo_ref[0])
    ms = jnp.mean(h_res * h_res, axis=-1, keepdims=True)
    hn = (h_res * lax.rsqrt(ms + EPS) * nf).astype(BF16)
    y = h_res
    for c in range(0, wu_ref.shape[2], FFN_CHUNK):
        u = jnp.maximum(_dot(hn, wu_ref[0, :, c:c + FFN_CHUNK]), 0.0)
        y = y + _dot((u * u).astype(BF16), wd_ref[0, c:c + FFN_CHUNK, :])
    return y


def _merge_kernel(x_ref, oa_ref, ob_ref, oc_ref, gc_ref, on_ref, wo_ref, nf_ref, wu_ref, wd_ref,
                  y_ref):
    y_ref[...] = _merge_block(x_ref[...], oa_ref[...], ob_ref[...], oc_ref[...], gc_ref[...],
                              on_ref[...], wo_ref, nf_ref[...], wu_ref, wd_ref)


def _merge(x2d, oa, ob, oc, gc, onorm, wo16, nf, wu16, wd16, *, layer, tm):
    m, d = x2d.shape
    assert m % tm == 0
    row = lambda w: pl.BlockSpec((tm, w), lambda i: (i, 0))
    full = _resident
    return pl.pallas_call(
        _merge_kernel,
        grid=(m // tm,),
        in_specs=[row(d), row(W_ATT), row(W_ATT), row(W_C), row(W_C),
                  full(onorm), _resident_layer(wo16, layer), full(nf),
                  _resident_layer(wu16, layer), _resident_layer(wd16, layer)],
        out_specs=row(d),
        out_shape=jax.ShapeDtypeStruct((m, d), F32),
        compiler_params=_cparams(1),
        name="merge_ffn",
    )(x2d, oa, ob, oc, gc, onorm, wo16, nf, wu16, wd16)


def kernel(x_prompt, x_sample, cache_a_k, cache_a_v, cache_b_k, cache_b_v, state_c, norm_mix, w_in, qnorm_a, knorm_a, qnorm_b, knorm_b, rel_bias_b, lower_bounds, onorm_c, w_o, norm_ffn, w_up, w_down):
    depth = w_in.shape[0]
    bp, tp, d = x_prompt.shape
    bs, ts, _ = x_sample.shape
    past = cache_a_k.shape[2]
    band_rows_p = min(N_PREV_CHUNKS * CHUNK, tp)
    assert bp == 1

    xp = x_prompt.reshape(bp * tp, d)
    xs = x_sample.reshape(bs * ts, d)
    time_minor = lambda c: jnp.transpose(c, (0, 1, 3, 4, 2)).reshape(depth, bs, W_ATT, c.shape[2])
    cak, cav, cbk, cbv = (time_minor(c) for c in (cache_a_k, cache_a_v, cache_b_k, cache_b_v))
    heads_of = lambda a_t, n: jnp.transpose(
        a_t.reshape(N_HEADS, D_HEAD, n, a_t.shape[1] // n), (2, 3, 0, 1))
    lbounds = lower_bounds.astype(F32)
    w16, wo16, wu16, wd16 = (w.astype(BF16) for w in (w_in, w_o, w_up, w_down))
    zero_state = jnp.zeros((bp, N_HEADS, DK_C, DK_C), F32)

    tm_p = min(512, tp)
    tq_band = min(512, tp)
    tr_p = min(512, tp)

    outs_p = [[] for _ in range(5)]
    outs_s = [[] for _ in range(5)]
    carried = ()
    for l in range(depth):
        nm = norm_mix[l].reshape(1, d)
        nf = norm_ffn[l].reshape(1, d)
        tile_h = lambda g: jnp.tile(g.reshape(1, D_HEAD), (1, N_HEADS))
        qna, kna, qnb, knb = (tile_h(g[l]) for g in (qnorm_a, knorm_a, qnorm_b, knorm_b))
        onorm = onorm_c[l].reshape(1, DK_C)
        table = rel_bias_b[l].astype(F32)

        (qa, ka_all, va_all, ka16_t, va16, qb, kb_t, vb_t, kb16_t, vb16, qc, kc, lf, ic, gc) = _proj(
            xp, nm, w16, qna, kna, qnb, knb, lbounds, layer=l, tm=tm_p, time_minor=True,
            carried=carried)
        carried = (ka_all, va_all)
        oa = _sb_prompt(qa, ka16_t, va16, tq=min(SB_TQ, tp))
        ob = _band_prompt(qb, kb16_t, vb16, table, tq=tq_band)
        oc, s_p = _hgrn(qc, kc, lf, ic, zero_state, nb=bp, t=tp, tr=tr_p)
        xp = _merge(xp, oa, ob, oc, gc, onorm, wo16, nf, wu16, wd16, layer=l, tm=tm_p)
        outs_p[2].append(heads_of(kb_t, bp)[:, tp - band_rows_p:])
        outs_p[3].append(heads_of(vb_t, bp)[:, tp - band_rows_p:])
        outs_p[4].append(s_p)

        head = lambda a, n: a.reshape(n, -1, N_HEADS, D_HEAD)
        (qa, ka, va, ka16, va16, qb, kb, vb, kb16, vb16, qc, kc, lf, ic, gc) = _proj(
            xs, nm, w16, qna, kna, qnb, knb, lbounds, layer=l, tm=bs * ts, time_minor=False)
        oa = _sb_sample(qa, ka16, va16, cak, cav, layer=l, nb=bs, ts=ts)
        ob = _band_sample(qb, kb16, vb16, cbk, cbv, table, layer=l, nb=bs, ts=ts, past=past)
        oc, s_s = _hgrn(qc, kc, lf, ic, state_c[l].astype(F32), nb=bs, t=ts, tr=ts)
        xs = _merge(xs, oa, ob, oc, gc, onorm, wo16, nf, wu16, wd16, layer=l, tm=bs * ts)
        outs_s[0].append(head(ka, bs))
        outs_s[1].append(head(va, bs))
        outs_s[2].append(head(kb, bs))
        outs_s[3].append(head(vb, bs))
        outs_s[4].append(s_s)

    stack = lambda xs_: jnp.stack(xs_)
    all_heads_of = lambda a: jnp.transpose(
        a.reshape(depth, N_HEADS, D_HEAD, bp, tp), (0, 3, 4, 1, 2))
    ka_all, va_all = carried
    return (xp.reshape(bp, tp, d), xs.reshape(bs, ts, d),
            all_heads_of(ka_all), all_heads_of(va_all),
            stack(outs_p[2]), stack(outs_p[3]), stack(outs_p[4]),
            stack(outs_s[0]), stack(outs_s[1]), stack(outs_s[2]), stack(outs_s[3]), stack(outs_s[4]))
```

```python
import functools
import math

import jax
import jax.numpy as jnp
from jax import lax
from jax.experimental import pallas as pl
from jax.experimental.pallas import tpu as pltpu

F32 = jnp.float32
BF16 = jnp.bfloat16

D_HEAD = 64
N_HEADS = 4
W_ATT = N_HEADS * D_HEAD
DK_C = 128
W_C = N_HEADS * DK_C
CHUNK = 64
N_PREV_CHUNKS = 8
REL_CLIP = 128
N_REL = 2 * REL_CLIP + 1
EPS = 1e-6
NEG_BIG = -1e30
LB_FLOOR = 1e-30
LOG2E = math.log2(math.e)
SB_UNDERFLOW = 88.0
PROJ_PART = 256
FFN_CHUNK = 1024
SB_TK = 128
SB_TQ = 256
HGRN_DIAG = 8
VMEM_LIMIT = 56 * 1024 * 1024


def _cparams(n_axes):
    return pltpu.CompilerParams(dimension_semantics=("arbitrary",) * n_axes,
                                vmem_limit_bytes=VMEM_LIMIT)


def _resident(a):
    zeros = (0,) * a.ndim
    return pl.BlockSpec(a.shape, lambda *_: zeros, pipeline_mode=pl.Buffered(1))


def _resident_layer(a, layer):
    index = (layer,) + (0,) * (a.ndim - 1)
    return pl.BlockSpec((1,) + a.shape[1:], lambda *_: index, pipeline_mode=pl.Buffered(1))


def _split3(x):
    h1 = x.astype(BF16)
    r1 = x - h1.astype(F32)
    h2 = r1.astype(BF16)
    h3 = (r1 - h2.astype(F32)).astype(BF16)
    return h1, h2, h3


def _dot(a, b):
    return jnp.dot(a, b, preferred_element_type=F32)


def _dot_nt(a, b):
    return lax.dot_general(a, b, (((1,), (1,)), ((), ())), preferred_element_type=F32)


def _dot_tn(a, b):
    return lax.dot_general(a, b, (((0,), (0,)), ((), ())), preferred_element_type=F32)


def _iota(shape, dim):
    return lax.broadcasted_iota(jnp.int32, shape, dim)


def _proj_kernel(x_ref, nm_ref, w_ref, qna_ref, kna_ref, qnb_ref, knb_ref, lb_ref, *rest,
                 layer, time_minor):
    (qa_ref, ka_ref, va_ref, ka16_ref, va16_ref, qb_ref, kb_ref, vb_ref, kb16_ref, vb16_ref,
     qc_ref, kc_ref, lf_ref, ic_ref, gc_ref) = rest[-15:]
    if time_minor:
        for earlier_ref, all_ref in zip(rest[:-15], (ka_ref, va_ref)):
            all_ref[0:layer] = earlier_ref[...]
        ka_ref, va_ref = ka_ref.at[layer], va_ref.at[layer]
    x = x_ref[...]
    ms = jnp.mean(x * x, axis=-1, keepdims=True)
    xn = (x * lax.rsqrt(ms + EPS) * nm_ref[...]).astype(BF16)

    tm = x.shape[0]
    part = PROJ_PART if tm % PROJ_PART == 0 else tm
    projected = [_dot(xn[r0:r0 + part], w_ref[0]) for r0 in range(0, tm, part)]

    same_head = (_iota((W_ATT, W_ATT), 0) // D_HEAD) == (_iota((W_ATT, W_ATT), 1) // D_HEAD)
    seg_mean = jnp.where(same_head, 1.0 / D_HEAD, 0.0).astype(BF16)
    inv_sqrt_d = 1.0 / math.sqrt(D_HEAD)

    lbr = lb_ref[...]
    e = jnp.exp(lbr - jnp.max(lbr, axis=0, keepdims=True))
    sm = e / jnp.sum(e, axis=0, keepdims=True)
    lb = jnp.sum(sm[0:layer + 1], axis=0, keepdims=True) - sm[0:1]
    log_lb = jnp.log(jnp.maximum(lb, LB_FLOOR))
    log_1m_lb = jnp.log1p(-lb)

    def head_norm(p, g_ref):
        m = _dot((p * p).astype(BF16), seg_mean)
        return p * lax.rsqrt(m + EPS) * g_ref[...]

    for n, p in enumerate(projected):
        rows = slice(n * part, (n + 1) * part)

        def put(x, f32_ref, b16_ref, b16_time_minor):
            x_t = x.T if time_minor else None
            if time_minor:
                f32_ref[:, rows] = x_t
            else:
                f32_ref[rows, :] = x
            if time_minor and b16_time_minor:
                b16_ref[:, rows] = x_t.astype(BF16)
            else:
                b16_ref[rows, :] = x.astype(BF16)

        o = 0
        qa_ref[rows, :] = (head_norm(p[:, o:o + W_ATT], qna_ref) * inv_sqrt_d).astype(BF16)
        o += W_ATT
        put(head_norm(p[:, o:o + W_ATT], kna_ref), ka_ref, ka16_ref, True)
        o += W_ATT
        put(p[:, o:o + W_ATT], va_ref, va16_ref, False)
        o += W_ATT
        qb_ref[rows, :] = (head_norm(p[:, o:o + W_ATT], qnb_ref)
                           * (inv_sqrt_d * LOG2E)).astype(BF16)
        o += W_ATT
        put(head_norm(p[:, o:o + W_ATT], knb_ref), kb_ref, kb16_ref, True)
        o += W_ATT
        put(p[:, o:o + W_ATT], vb_ref, vb16_ref, False)
        o += W_ATT
        qc_ref[rows, :] = p[:, o:o + W_C] * (DK_C ** -0.5)
        o += W_C
        f_raw = p[:, o:o + W_C]
        o += W_C
        log_sig = jnp.minimum(f_raw, 0.0) - jnp.log(1.0 + jnp.exp(-jnp.abs(f_raw)))
        b = log_1m_lb + log_sig
        lf_ref[rows, :] = jnp.maximum(log_lb, b) + jnp.log(1.0 + jnp.exp(-jnp.abs(log_lb - b)))
        kc_ref[rows, :] = (1.0 - lb) * (1.0 / (1.0 + jnp.exp(f_raw)))
        ic_ref[rows, :] = p[:, o:o + W_C]
        o += W_C
        gc_ref[rows, :] = p[:, o:o + W_C]


def _proj(x2d, nm, w16, qna, kna, qnb, knb, lower_bounds, *, layer, tm, time_minor, carried=()):
    m, d = x2d.shape
    assert m % tm == 0 and len(carried) == (2 if time_minor and layer > 0 else 0)
    full = _resident
    row = lambda w, dt: (pl.BlockSpec((tm, w), lambda i: (i, 0)), jax.ShapeDtypeStruct((m, w), dt))
    col = lambda w, dt: (pl.BlockSpec((w, tm), lambda i: (0, i)), jax.ShapeDtypeStruct((w, m), dt))
    layers = lambda n, w: pl.BlockSpec((n, w, tm), lambda i: (0, 0, i))
    layered = lambda w, dt: (layers(layer + 1, w), jax.ShapeDtypeStruct((layer + 1, w, m), dt))
    kv = col if time_minor else row
    kv_a = layered if time_minor else row
    att_a = [row(W_ATT, BF16), kv_a(W_ATT, F32), kv_a(W_ATT, F32), kv(W_ATT, BF16), row(W_ATT, BF16)]
    att_b = [row(W_ATT, BF16), kv(W_ATT, F32), kv(W_ATT, F32), kv(W_ATT, BF16), row(W_ATT, BF16)]
    outs = att_a + att_b + [row(W_C, F32)] * 5
    return pl.pallas_call(
        functools.partial(_proj_kernel, layer=layer, time_minor=time_minor),
        grid=(m // tm,),
        in_specs=[row(d, F32)[0], full(nm), _resident_layer(w16, layer), full(qna), full(kna),
                  full(qnb), full(knb),
                  full(lower_bounds)] + [layers(layer, W_ATT)] * len(carried),
        out_specs=[spec for spec, _ in outs],
        out_shape=[shape for _, shape in outs],
        compiler_params=_cparams(1),
        name="proj",
    )(x2d, nm, w16, qna, kna, qnb, knb, lower_bounds, *carried)


def _head_masks(width):
    lane_head = _iota((1, width), 1) // D_HEAD
    return [lane_head == h for h in range(N_HEADS)]


def _stack_heads(x, head_lane):
    return jnp.concatenate([jnp.where(m, x, jnp.zeros_like(x)) for m in head_lane], axis=0)


def _stack_heads_t(x_t, n):
    row_head = _iota((W_ATT, 1), 0) // D_HEAD
    return jnp.concatenate(
        [jnp.where(row_head == h, x_t, jnp.zeros_like(x_t)) for h in range(N_HEADS)], axis=1)


def _stack_groups(x, head_lane, gq):
    return jnp.concatenate(
        [_stack_heads(x[g:g + gq], head_lane) for g in range(0, x.shape[0], gq)], axis=0)


def _sb_block(z, weigh_values, mask, tri, carry_ref, acc_ref, gq):
    tk = z.shape[1]
    t = jnp.log(1.0 + jnp.exp2(jnp.abs(z) * -math.log2(math.e)))
    log_1m = jnp.minimum(-z, 0.0) - t
    log_beta = log_1m + z
    if mask is not None:
        log_1m = jnp.where(mask, log_1m, 0.0)
    hi = log_1m.astype(BF16)
    lo = (log_1m - hi.astype(F32)).astype(BF16)
    later = _dot(jnp.concatenate([hi, lo], axis=1), tri)
    carry = carry_ref[...]
    w = jnp.exp(log_beta + later + carry[:, :tk])
    if mask is not None:
        w = jnp.where(mask, w, 0.0)
    w = w.astype(BF16)
    blocks = [w[n * gq:(n + 1) * gq] for n in range(z.shape[0] // gq)]
    w_heads = jnp.concatenate(
        [jnp.concatenate(blocks[g:g + N_HEADS], axis=1) for g in range(0, len(blocks), N_HEADS)],
        axis=0)
    acc_ref[...] += weigh_values(w_heads)
    carry_ref[...] = carry + jnp.sum(log_1m, axis=-1, keepdims=True)


def _sb_live(carry_ref):
    return (jnp.max(carry_ref[...]) > -SB_UNDERFLOW).astype(jnp.int32)


def _strict_upper(n):
    return jnp.where(_iota((2 * n, n), 0) % n > _iota((2 * n, n), 1), 1.0, 0.0).astype(BF16)


def _sb_prompt_kernel(q_ref, kt_ref, v_ref, o_ref, carry_ref, acc_ref, *, tq):
    i = pl.program_id(0)
    head_lane = _head_masks(W_ATT)
    gq = SB_TK
    n_groups = tq // gq
    q_stack = _stack_groups(q_ref[...], head_lane, gq)
    tri = _strict_upper(SB_TK)
    carry_ref[...] = jnp.zeros_like(carry_ref)
    acc_ref[...] = jnp.zeros_like(acc_ref)

    def add_block(j, first_group, mask):
        r0 = pl.multiple_of(j * SB_TK, SB_TK)
        s0 = first_group * N_HEADS * gq
        z = _dot(q_stack[s0:], kt_ref[:, pl.ds(r0, SB_TK)])
        v_stack = _stack_heads(v_ref[pl.ds(r0, SB_TK), :], head_lane)
        _sb_block(z, lambda w: _dot(w, v_stack), mask, tri,
                  carry_ref.at[pl.ds(s0, z.shape[0])],
                  acc_ref.at[pl.ds(first_group * gq, (n_groups - first_group) * gq)], gq)

    for g in reversed(range(n_groups)):
        shape = ((n_groups - g) * N_HEADS * gq, SB_TK)
        row = _iota(shape, 0)
        add_block(i * n_groups + g, g, jnp.logical_or(_iota(shape, 1) < row % gq, row >= N_HEADS * gq))
    add_block(jnp.maximum(i * n_groups - 1, 0), 0,
              jnp.broadcast_to(i > 0, (n_groups * N_HEADS * gq, SB_TK)))

    def cond(c):
        j, live = c
        return jnp.logical_and(j >= 0, live > 0)

    def body(c):
        j, _ = c
        add_block(j, 0, None)
        return j - 1, _sb_live(carry_ref)

    lax.while_loop(cond, body, (i * n_groups - 2, _sb_live(carry_ref)))
    o_ref[...] = acc_ref[...].astype(o_ref.dtype)


def _sb_prompt(q16, kt16, v16, *, tq):
    t = q16.shape[0]
    assert t % tq == 0 and tq % SB_TK == 0
    return pl.pallas_call(
        functools.partial(_sb_prompt_kernel, tq=tq),
        grid=(t // tq,),
        in_specs=[pl.BlockSpec((tq, W_ATT), lambda i: (i, 0)),
                  _resident(kt16), _resident(v16)],
        out_specs=pl.BlockSpec((tq, W_ATT), lambda i: (i, 0)),
        out_shape=jax.ShapeDtypeStruct((t, W_ATT), BF16),
        scratch_shapes=[pltpu.VMEM((N_HEADS * tq, SB_TK), F32), pltpu.VMEM((tq, W_ATT), F32)],
        compiler_params=_cparams(1),
        name="sb_prompt",
    )(q16, kt16, v16)


def _sb_sample_kernel(q_ref, kn_ref, vn_ref, ckt_ref, cvt_ref, o_ref, carry_ref, acc_ref, *, ts, past):
    head_lane = _head_masks(W_ATT)
    q_stack = _stack_heads(q_ref[...], head_lane)
    carry_ref[...] = jnp.zeros_like(carry_ref)
    acc_ref[...] = jnp.zeros_like(acc_ref)
    shape = (N_HEADS * ts, ts)
    mask = _iota(shape, 1) < _iota(shape, 0) % ts
    v_new = _stack_heads(vn_ref[...], head_lane)
    _sb_block(_dot_nt(q_stack, kn_ref[...]), lambda w: _dot(w, v_new), mask, _strict_upper(ts),
              carry_ref, acc_ref, ts)
    tri = _strict_upper(SB_TK)

    def cond(c):
        j, live = c
        return jnp.logical_and(j >= 0, live > 0)

    def body(c):
        j, _ = c
        r0 = pl.multiple_of(j * SB_TK, SB_TK)
        z = _dot(q_stack, ckt_ref[0, 0, :, pl.ds(r0, SB_TK)].astype(BF16))
        vt_stack = _stack_heads_t(cvt_ref[0, 0, :, pl.ds(r0, SB_TK)].astype(BF16), SB_TK)
        _sb_block(z, lambda w: _dot_nt(w, vt_stack), None, tri, carry_ref, acc_ref, ts)
        return j - 1, _sb_live(carry_ref)

    lax.while_loop(cond, body, (past // SB_TK - 1, _sb_live(carry_ref)))
    o_ref[...] = acc_ref[...].astype(o_ref.dtype)


def _sb_sample(q16, kn16, vn16, cache_kt, cache_vt, *, layer, nb, ts):
    past = cache_kt.shape[3]
    assert past % SB_TK == 0
    row = pl.BlockSpec((ts, W_ATT), lambda b: (b, 0))
    cache = pl.BlockSpec((1, 1, W_ATT, past), lambda b: (layer, b, 0, 0))
    return pl.pallas_call(
        functools.partial(_sb_sample_kernel, ts=ts, past=past),
        grid=(nb,),
        in_specs=[row, row, row, cache, cache],
        out_specs=row,
        out_shape=jax.ShapeDtypeStruct((nb * ts, W_ATT), BF16),
        scratch_shapes=[pltpu.VMEM((N_HEADS * ts, SB_TK), F32), pltpu.VMEM((ts, W_ATT), F32)],
        compiler_params=_cparams(1),
        name="sb_sample",
    )(q16, kn16, vn16, cache_kt, cache_vt)


def _rel_bias(table_ref, h, dist):
    idx = jnp.clip(dist, -REL_CLIP, REL_CLIP) + REL_CLIP

    def body(r, b):
        return jnp.where(idx == r, table_ref[h, r], b)

    return lax.fori_loop(0, N_REL, body, jnp.zeros(dist.shape, F32))


def _row_reduce(x, combine, reduce):
    width = x.shape[1]
    acc = x[:, :128]
    for c in range(128, width, 128):
        acc = combine(acc, x[:, c:c + 128])
    return reduce(acc, axis=-1, keepdims=True)


def _toeplitz_bias(table_ref, h, rows, back, width):
    lane = _iota((8, width), 1)
    by_diag = _rel_bias(table_ref, h, back + rows - lane) * LOG2E
    tiled = jnp.concatenate([by_diag] * (rows // 8), axis=0)
    return pltpu.roll(tiled, width - rows, 1, stride=1, stride_axis=0)


BAND_TQ = 128
BAND_BACK = N_PREV_CHUNKS * CHUNK
BAND_WIN = BAND_BACK + BAND_TQ


def _band_prompt_kernel(table_ref, q_ref, k0_ref, k1_ref, k2_ref, v0_ref, v1_ref, v2_ref,
                        o_ref, bias_ref, *, tq):
    i = pl.program_id(0)
    n_kb = 3
    rows = _iota((BAND_TQ, BAND_WIN), 0)
    cols = _iota((BAND_TQ, BAND_WIN), 1)

    @pl.when(i == 0)
    def _():
        q_chunk = rows // CHUNK + N_PREV_CHUNKS
        k_chunk = cols // CHUNK
        in_band = jnp.logical_and(k_chunk >= q_chunk - N_PREV_CHUNKS, k_chunk <= q_chunk)
        for h in range(N_HEADS):
            bias = _toeplitz_bias(table_ref, h, BAND_TQ, BAND_BACK, BAND_WIN + BAND_TQ)
            bias_ref[h] = jnp.where(in_band, bias[:, :BAND_WIN], NEG_BIG)

    q = q_ref[...]
    head_lane = _head_masks(W_ATT)
    k_t = jnp.concatenate([k0_ref[...], k1_ref[...], k2_ref[...]], axis=1)
    vcat = jnp.concatenate([v0_ref[...], v1_ref[...], v2_ref[...]], axis=0)
    parts = [p * BAND_TQ for p in range(tq // BAND_TQ)]
    starts = [r0 + (n_kb - 1) * tq - BAND_BACK for r0 in parts]
    scores = []
    for p in range(len(parts)):
        qp = q[parts[p]:parts[p] + BAND_TQ]
        k_win = k_t[:, starts[p]:starts[p] + BAND_WIN]
        scores.append([_dot(jnp.where(head_lane[h], qp, jnp.zeros_like(qp)), k_win)
                       for h in range(N_HEADS)])
    for p in range(len(parts)):
        exists = (i - (n_kb - 1)) * tq + starts[p] + cols >= 0
        weights = []
        inv_den = []
        for h, s in enumerate(scores[p]):
            s = jnp.where(exists, s + bias_ref[h], NEG_BIG)
            e = jnp.exp2(s - _row_reduce(s, jnp.maximum, jnp.max))
            weights.append(e.astype(BF16))
            inv_den.append(1.0 / _row_reduce(e, jnp.add, jnp.sum))
        v_win = vcat[starts[p]:starts[p] + BAND_WIN]
        out = jnp.zeros((BAND_TQ, W_ATT), F32)
        for h in range(N_HEADS):
            out = jnp.where(head_lane[h], _dot(weights[h], v_win) * inv_den[h], out)
        o_ref[parts[p]:parts[p] + BAND_TQ, :] = out.astype(o_ref.dtype)


def _band_prompt(q16, kt16, v16, table, *, tq):
    t = q16.shape[0]
    assert t % tq == 0 and tq % BAND_TQ == 0 and 2 * tq >= BAND_BACK
    blk = lambda back: pl.BlockSpec((tq, W_ATT), lambda i: (jnp.maximum(i - back, 0), 0))
    blk_t = lambda back: pl.BlockSpec((W_ATT, tq), lambda i: (0, jnp.maximum(i - back, 0)))
    return pl.pallas_call(
        functools.partial(_band_prompt_kernel, tq=tq),
        grid=(t // tq,),
        in_specs=[pl.BlockSpec(memory_space=pltpu.SMEM),
                  blk(0), blk_t(2), blk_t(1), blk_t(0), blk(2), blk(1), blk(0)],
        out_specs=blk(0),
        out_shape=jax.ShapeDtypeStruct((t, W_ATT), BF16),
        scratch_shapes=[pltpu.VMEM((N_HEADS, BAND_TQ, BAND_WIN), F32)],
        compiler_params=_cparams(1),
        name="band_prompt",
    )(table, q16, kt16, kt16, kt16, v16, v16, v16)


def _band_sample_kernel(table_ref, q_ref, kn_ref, vn_ref, ckt_ref, cvt_ref, o_ref, bias_ref, *, ts, past):
    lb = ckt_ref.shape[3]

    @pl.when(pl.program_id(0) == 0)
    def _():
        for h in range(N_HEADS):
            bias_ref[h] = _toeplitz_bias(table_ref, h, ts, lb, bias_ref.shape[2])

    q = q_ref[...]
    head_lane = _head_masks(W_ATT)
    kc_t = ckt_ref[0, 0].astype(BF16)
    vc_t = cvt_ref[0, 0].astype(BF16)
    kn = kn_ref[...]
    vn = vn_ref[...]
    q_pos_c = past + _iota((ts, lb), 0)
    k_pos_c = past - lb + _iota((ts, lb), 1)
    q_pos_n = past + _iota((ts, ts), 0)
    k_pos_n = past + _iota((ts, ts), 1)

    def allowed(q_pos, k_pos):
        qc, kc_ = q_pos // CHUNK, k_pos // CHUNK
        return jnp.logical_and(kc_ >= qc - N_PREV_CHUNKS, kc_ <= qc)

    ok_c = allowed(q_pos_c, k_pos_c)
    ok_n = allowed(q_pos_n, k_pos_n)
    out = jnp.zeros((ts, W_ATT), F32)
    for h in range(N_HEADS):
        qh = jnp.where(head_lane[h], q, jnp.zeros_like(q))
        bias = bias_ref[h]
        s_c = jnp.where(ok_c, _dot(qh, kc_t) + bias[:, :lb], NEG_BIG)
        s_n = jnp.where(ok_n, _dot_nt(qh, kn) + bias[:, lb:lb + ts], NEG_BIG)
        m = jnp.maximum(jnp.max(s_c, axis=-1, keepdims=True), jnp.max(s_n, axis=-1, keepdims=True))
        e_c = jnp.exp2(s_c - m)
        e_n = jnp.exp2(s_n - m)
        den = jnp.sum(e_c, axis=-1, keepdims=True) + jnp.sum(e_n, axis=-1, keepdims=True)
        pv = (_dot_nt(e_c.astype(BF16), vc_t) + _dot(e_n.astype(BF16), vn)) / den
        out = jnp.where(head_lane[h], pv, out)
    o_ref[...] = out.astype(o_ref.dtype)


def _band_sample(q16, kn16, vn16, cache_kt, cache_vt, table, *, layer, nb, ts, past):
    lb = cache_kt.shape[3]
    row = pl.BlockSpec((ts, W_ATT), lambda b: (b, 0))
    cache = pl.BlockSpec((1, 1, W_ATT, lb), lambda b: (layer, b, 0, 0))
    return pl.pallas_call(
        functools.partial(_band_sample_kernel, ts=ts, past=past),
        grid=(nb,),
        in_specs=[pl.BlockSpec(memory_space=pltpu.SMEM), row, row, row, cache, cache],
        out_specs=row,
        out_shape=jax.ShapeDtypeStruct((nb * ts, W_ATT), BF16),
        scratch_shapes=[pltpu.VMEM((N_HEADS, ts, pl.cdiv(lb + 2 * ts, 128) * 128), F32)],
        compiler_params=_cparams(1),
        name="band_sample",
    )(table, q16, kn16, vn16, cache_kt, cache_vt)


def _hgrn_spans(q, k, lf, v, st_ref):
    tr = q.shape[0]
    heads = [slice(h * DK_C, (h + 1) * DK_C) for h in range(N_HEADS)]
    tri = jnp.where(_iota((tr, tr), 0) >= _iota((tr, tr), 1), 1.0, 0.0).astype(BF16)
    f1, f2, f3 = _split3(lf)
    g = (_dot(tri, f1) + _dot(tri, f2) + _dot(tri, f3)) * math.log2(math.e)
    v16 = v.astype(BF16)

    g_end = g[tr - 1:tr, :]
    qt = (q * jnp.exp2(g)).astype(BF16)
    kd = (k * jnp.exp2(g_end - g)).astype(BF16)
    dec = jnp.exp2(g_end)
    states = [st_ref[h] for h in range(N_HEADS)]
    o_heads = [_dot_nt(qt[:, hs], st.astype(BF16)) for hs, st in zip(heads, states)]
    grown = [_dot_tn(v16[:, hs], kd[:, hs]) for hs in heads]
    for h, hs in enumerate(heads):
        st_ref[h] = states[h] * dec[:, hs] + grown[h]
    o = jnp.concatenate(o_heads, axis=1)

    sizes = []
    h = HGRN_DIAG
    while h < tr:
        sizes.append(h)
        h *= 2
    split = lambda x, h: x.reshape(tr // (2 * h), 2 * h, W_C)
    operands = []
    for h in sizes:
        g3, q3, k3, v3 = split(g, h), split(q, h), split(k, h), split(v16, h)
        c = g3[:, h - 1:h, :]
        qe = (q3[:, h:] * jnp.exp2(g3[:, h:] - c)).reshape(tr // 2, W_C).astype(BF16)
        ke = (k3[:, :h] * jnp.exp2(c - g3[:, :h])).reshape(tr // 2, W_C).astype(BF16)
        operands.append((qe, ke, v3[:, :h].reshape(tr // 2, W_C)))
    pair = [[_dot_nt(qe[:, hs], ke[:, hs]) for hs in heads] for qe, ke, _ in operands]
    span_of_row = _iota((tr // 2, tr // 2), 0)
    span_of_col = _iota((tr // 2, tr // 2), 1)
    for n, h in enumerate(sizes):
        if 2 * h < tr:
            same_span = (span_of_row // h) == (span_of_col // h)
            pair[n] = [jnp.where(same_span, p, 0.0) for p in pair[n]]
    added = [jnp.concatenate([_dot(p.astype(BF16), vl[:, hs]) for p, hs in zip(pair[n], heads)],
                             axis=1) for n, (_, _, vl) in enumerate(operands)]
    for n, h in enumerate(sizes):
        o3 = split(o, h)
        upper = o3[:, h:] + added[n].reshape(tr // (2 * h), h, W_C)
        o = jnp.concatenate([o3[:, :h], upper], axis=1).reshape(tr, W_C)
    return o, g


def _hgrn_diagonal(g2, q2, k2, v2):
    n = HGRN_DIAG
    lane = _iota((n, DK_C), 1)
    weights = [[jnp.zeros((n, DK_C), F32) for _ in range(N_HEADS)] for _ in range(2)]
    for s in range(2):
        g8, q8, k8 = g2[s * n:(s + 1) * n], q2[s * n:(s + 1) * n], k2[s * n:(s + 1) * n]
        for j in range(n):
            p = q8 * (k8[j:j + 1, :] * jnp.exp2(g8 - g8[j:j + 1, :]))
            for h in range(N_HEADS):
                a = jnp.sum(p[:, h * DK_C:(h + 1) * DK_C], axis=-1, keepdims=True)
                weights[s][h] = jnp.where(lane == s * n + j, a, weights[s][h])
    row = _iota((2 * n, DK_C), 0)
    col = _iota((2 * n, DK_C), 1)
    keep = jnp.logical_and(row >= col, row // n == col // n)
    v16 = v2.astype(BF16)
    out = []
    for h in range(N_HEADS):
        a = jnp.where(keep, jnp.concatenate([weights[0][h], weights[1][h]], axis=0), 0.0)
        out.append(_dot(a[:, :2 * n].astype(BF16), v16[:, h * DK_C:(h + 1) * DK_C]))
    return jnp.concatenate(out, axis=1)


def _hgrn_block(q, k, lf, v, st_ref):
    o, g = _hgrn_spans(q, k, lf, v, st_ref)
    pairs = [slice(r0, r0 + 2 * HGRN_DIAG) for r0 in range(0, q.shape[0], 2 * HGRN_DIAG)]
    return o + jnp.concatenate([_hgrn_diagonal(g[s], q[s], k[s], v[s]) for s in pairs], axis=0)


def _hgrn_kernel(q_ref, k_ref, lf_ref, v_ref, s0_ref, o_ref, sout_ref, st_ref):
    r = pl.program_id(1)

    @pl.when(r == 0)
    def _():
        for h in range(N_HEADS):
            st_ref[h] = s0_ref[0, h].T

    o_ref[...] = _hgrn_block(q_ref[...], k_ref[...], lf_ref[...], v_ref[...], st_ref)

    @pl.when(r == pl.num_programs(1) - 1)
    def _():
        for h in range(N_HEADS):
            sout_ref[0, h] = st_ref[h].T


def _hgrn(qc, kc, lf, ic, s0, *, nb, t, tr):
    assert t % tr == 0 and tr % (2 * HGRN_DIAG) == 0 and tr & (tr - 1) == 0
    nr = t // tr
    row = pl.BlockSpec((tr, W_C), lambda b, r: (b * nr + r, 0))
    state = pl.BlockSpec((1, N_HEADS, DK_C, DK_C), lambda b, r: (b, 0, 0, 0))
    return pl.pallas_call(
        _hgrn_kernel,
        grid=(nb, nr),
        in_specs=[row, row, row, row, state],
        out_specs=[row, state],
        out_shape=[jax.ShapeDtypeStruct((nb * t, W_C), F32),
                   jax.ShapeDtypeStruct((nb, N_HEADS, DK_C, DK_C), F32)],
        scratch_shapes=[pltpu.VMEM((N_HEADS, DK_C, DK_C), F32)],
        compiler_params=_cparams(2),
        name="hgrn",
    )(qc, kc, lf, ic, s0)


def _merge_block(x, oa, ob, oc, gc, on, wo_ref, nf, wu_ref, wd_ref):
    parts = []
    for h in range(N_HEADS):
        och = oc[:, h * DK_C:(h + 1) * DK_C]
        ms = jnp.mean(och * och, axis=-1, keepdims=True)
        parts.append(och * lax.rsqrt(ms + EPS) * on)
    ocn = jnp.concatenate(parts, axis=-1) * (gc * (1.0 / (1.0 + jnp.exp(-gc))))
    mixed = jnp.concatenate([oa, ob, ocn.astype(BF16)], axis=-1)
    h_res = x + _dot(mixed, wo_ref[0])
    ms = jnp.mean(h_res * h_res, axis=-1, keepdims=True)
    hn = (h_res * lax.rsqrt(ms + EPS) * nf).astype(BF16)
    y = h_res
    for c in range(0, wu_ref.shape[2], FFN_CHUNK):
        u = jnp.maximum(_dot(hn, wu_ref[0, :, c:c + FFN_CHUNK]), 0.0)
        y = y + _dot((u * u).astype(BF16), wd_ref[0, c:c + FFN_CHUNK, :])
    return y


def _merge_kernel(x_ref, oa_ref, ob_ref, oc_ref, gc_ref, on_ref, wo_ref, nf_ref, wu_ref, wd_ref,
                  y_ref):
    y_ref[...] = _merge_block(x_ref[...], oa_ref[...], ob_ref[...], oc_ref[...], gc_ref[...],
                              on_ref[...], wo_ref, nf_ref[...], wu_ref, wd_ref)


def _merge(x2d, oa, ob, oc, gc, onorm, wo16, nf, wu16, wd16, *, layer, tm):
    m, d = x2d.shape
    assert m % tm == 0
    row = lambda w: pl.BlockSpec((tm, w), lambda i: (i, 0))
    full = _resident
    return pl.pallas_call(
        _merge_kernel,
        grid=(m // tm,),
        in_specs=[row(d), row(W_ATT), row(W_ATT), row(W_C), row(W_C),
                  full(onorm), _resident_layer(wo16, layer), full(nf),
                  _resident_layer(wu16, layer), _resident_layer(wd16, layer)],
        out_specs=row(d),
        out_shape=jax.ShapeDtypeStruct((m, d), F32),
        compiler_params=_cparams(1),
        name="merge_ffn",
    )(x2d, oa, ob, oc, gc, onorm, wo16, nf, wu16, wd16)


def kernel(x_prompt, x_sample, cache_a_k, cache_a_v, cache_b_k, cache_b_v, state_c, norm_mix, w_in, qnorm_a, knorm_a, qnorm_b, knorm_b, rel_bias_b, lower_bounds, onorm_c, w_o, norm_ffn, w_up, w_down):
    depth = w_in.shape[0]
    bp, tp, d = x_prompt.shape
    bs, ts, _ = x_sample.shape
    past = cache_a_k.shape[2]
    band_rows_p = min(N_PREV_CHUNKS * CHUNK, tp)
    assert bp == 1

    xp = x_prompt.reshape(bp * tp, d)
    xs = x_sample.reshape(bs * ts, d)
    time_minor = lambda c: jnp.transpose(c, (0, 1, 3, 4, 2)).reshape(depth, bs, W_ATT, c.shape[2])
    cak, cav, cbk, cbv = (time_minor(c) for c in (cache_a_k, cache_a_v, cache_b_k, cache_b_v))
    heads_of = lambda a_t, n: jnp.transpose(
        a_t.reshape(N_HEADS, D_HEAD, n, a_t.shape[1] // n), (2, 3, 0, 1))
    lbounds = lower_bounds.astype(F32)
    w16, wo16, wu16, wd16 = (w.astype(BF16) for w in (w_in, w_o, w_up, w_down))
    zero_state = jnp.zeros((bp, N_HEADS, DK_C, DK_C), F32)

    tm_p = min(512, tp)
    tq_band = min(1024, tp)
    tr_p = min(256, tp)

    outs_p = [[] for _ in range(5)]
    outs_s = [[] for _ in range(5)]
    carried = ()
    for l in range(depth):
        nm = norm_mix[l].reshape(1, d)
        nf = norm_ffn[l].reshape(1, d)
        tile_h = lambda g: jnp.tile(g.reshape(1, D_HEAD), (1, N_HEADS))
        qna, kna, qnb, knb = (tile_h(g[l]) for g in (qnorm_a, knorm_a, qnorm_b, knorm_b))
        onorm = onorm_c[l].reshape(1, DK_C)
        table = rel_bias_b[l].astype(F32)

        (qa, ka_all, va_all, ka16_t, va16, qb, kb_t, vb_t, kb16_t, vb16, qc, kc, lf, ic, gc) = _proj(
            xp, nm, w16, qna, kna, qnb, knb, lbounds, layer=l, tm=tm_p, time_minor=True,
            carried=carried)
        carried = (ka_all, va_all)
        oa = _sb_prompt(qa, ka16_t, va16, tq=min(SB_TQ, tp))
        ob = _band_prompt(qb, kb16_t, vb16, table, tq=tq_band)
        oc, s_p = _hgrn(qc, kc, lf, ic, zero_state, nb=bp, t=tp, tr=tr_p)
        xp = _merge(xp, oa, ob, oc, gc, onorm, wo16, nf, wu16, wd16, layer=l, tm=tm_p)
        outs_p[2].append(heads_of(kb_t, bp)[:, tp - band_rows_p:])
        outs_p[3].append(heads_of(vb_t, bp)[:, tp - band_rows_p:])
        outs_p[4].append(s_p)

        head = lambda a, n: a.reshape(n, -1, N_HEADS, D_HEAD)
        (qa, ka, va, ka16, va16, qb, kb, vb, kb16, vb16, qc, kc, lf, ic, gc) = _proj(
            xs, nm, w16, qna, kna, qnb, knb, lbounds, layer=l, tm=bs * ts, time_minor=False)
        oa = _sb_sample(qa, ka16, va16, cak, cav, layer=l, nb=bs, ts=ts)
        ob = _band_sample(qb, kb16, vb16, cbk, cbv, table, layer=l, nb=bs, ts=ts, past=past)
        oc, s_s = _hgrn(qc, kc, lf, ic, state_c[l].astype(F32), nb=bs, t=ts, tr=ts)
        xs = _merge(xs, oa, ob, oc, gc, onorm, wo16, nf, wu16, wd16, layer=l, tm=bs * ts)
        outs_s[0].append(head(ka, bs))
        outs_s[1].append(head(va, bs))
        outs_s[2].append(head(kb, bs))
        outs_s[3].append(head(vb, bs))
        outs_s[4].append(s_s)

    stack = lambda xs_: jnp.stack(xs_)
    all_heads_of = lambda a: jnp.transpose(
        a.reshape(depth, N_HEADS, D_HEAD, bp, tp), (0, 3, 4, 1, 2))
    ka_all, va_all = carried
    return (xp.reshape(bp, tp, d), xs.reshape(bs, ts, d),
            all_heads_of(ka_all), all_heads_of(va_all),
            stack(outs_p[2]), stack(outs_p[3]), stack(outs_p[4]),
            stack(outs_s[0]), stack(outs_s[1]), stack(outs_s[2]), stack(outs_s[3]), stack(outs_s[4]))
```

```python
import functools
import math

import jax
import jax.numpy as jnp
from jax import lax
from jax.experimental import pallas as pl
from jax.experimental.pallas import tpu as pltpu

F32 = jnp.float32
BF16 = jnp.bfloat16

D_HEAD = 64
N_HEADS = 4
W_ATT = N_HEADS * D_HEAD
DK_C = 128
W_C = N_HEADS * DK_C
CHUNK = 64
N_PREV_CHUNKS = 8
REL_CLIP = 128
N_REL = 2 * REL_CLIP + 1
EPS = 1e-6
NEG_BIG = -1e30
LB_FLOOR = 1e-30
LOG2E = math.log2(math.e)
SB_UNDERFLOW = 88.0
PROJ_PART = 256
FFN_CHUNK = 1024
SB_TK = 128
SB_TQ = 256
HGRN_DIAG = 8
VMEM_LIMIT = 60 * 1024 * 1024


def _cparams(n_axes):
    return pltpu.CompilerParams(dimension_semantics=("arbitrary",) * n_axes,
                                vmem_limit_bytes=VMEM_LIMIT)


def _resident(a):
    zeros = (0,) * a.ndim
    return pl.BlockSpec(a.shape, lambda *_: zeros, pipeline_mode=pl.Buffered(1))


def _resident_layer(a, layer):
    index = (layer,) + (0,) * (a.ndim - 1)
    return pl.BlockSpec((1,) + a.shape[1:], lambda *_: index, pipeline_mode=pl.Buffered(1))


def _split3(x):
    h1 = x.astype(BF16)
    r1 = x - h1.astype(F32)
    h2 = r1.astype(BF16)
    h3 = (r1 - h2.astype(F32)).astype(BF16)
    return h1, h2, h3


def _dot(a, b):
    return jnp.dot(a, b, preferred_element_type=F32)


def _dot_nt(a, b):
    return lax.dot_general(a, b, (((1,), (1,)), ((), ())), preferred_element_type=F32)


def _dot_tn(a, b):
    return lax.dot_general(a, b, (((0,), (0,)), ((), ())), preferred_element_type=F32)


def _iota(shape, dim):
    return lax.broadcasted_iota(jnp.int32, shape, dim)


def _proj_kernel(x_ref, nm_ref, w_ref, qna_ref, kna_ref, qnb_ref, knb_ref, lb_ref, *rest,
                 layer, time_minor):
    (qa_ref, ka_ref, va_ref, ka16_ref, va16_ref, qb_ref, kb_ref, vb_ref, kb16_ref, vb16_ref,
     qc_ref, kc_ref, lf_ref, ic_ref, gc_ref) = rest[-15:]
    if time_minor:
        for earlier_ref, all_ref in zip(rest[:-15], (ka_ref, va_ref)):
            all_ref[0:layer] = earlier_ref[...]
        ka_ref, va_ref = ka_ref.at[layer], va_ref.at[layer]
    x = x_ref[...]
    ms = jnp.mean(x * x, axis=-1, keepdims=True)
    xn = (x * lax.rsqrt(ms + EPS) * nm_ref[...]).astype(BF16)

    tm = x.shape[0]
    part = PROJ_PART if tm % PROJ_PART == 0 else tm
    projected = [_dot(xn[r0:r0 + part], w_ref[0]) for r0 in range(0, tm, part)]

    same_head = (_iota((W_ATT, W_ATT), 0) // D_HEAD) == (_iota((W_ATT, W_ATT), 1) // D_HEAD)
    seg_mean = jnp.where(same_head, 1.0 / D_HEAD, 0.0).astype(BF16)
    inv_sqrt_d = 1.0 / math.sqrt(D_HEAD)

    lbr = lb_ref[...]
    e = jnp.exp(lbr - jnp.max(lbr, axis=0, keepdims=True))
    sm = e / jnp.sum(e, axis=0, keepdims=True)
    lb = jnp.sum(sm[0:layer + 1], axis=0, keepdims=True) - sm[0:1]
    log_lb = jnp.log(jnp.maximum(lb, LB_FLOOR))
    log_1m_lb = jnp.log1p(-lb)

    def head_norm(p, g_ref):
        m = _dot((p * p).astype(BF16), seg_mean)
        return p * lax.rsqrt(m + EPS) * g_ref[...]

    for n, p in enumerate(projected):
        rows = slice(n * part, (n + 1) * part)

        def put(x, f32_ref, b16_ref, b16_time_minor):
            x_t = x.T if time_minor else None
            if time_minor:
                f32_ref[:, rows] = x_t
            else:
                f32_ref[rows, :] = x
            if time_minor and b16_time_minor:
                b16_ref[:, rows] = x_t.astype(BF16)
            else:
                b16_ref[rows, :] = x.astype(BF16)

        o = 0
        qa_ref[rows, :] = (head_norm(p[:, o:o + W_ATT], qna_ref) * inv_sqrt_d).astype(BF16)
        o += W_ATT
        put(head_norm(p[:, o:o + W_ATT], kna_ref), ka_ref, ka16_ref, True)
        o += W_ATT
        put(p[:, o:o + W_ATT], va_ref, va16_ref, False)
        o += W_ATT
        qb_ref[rows, :] = (head_norm(p[:, o:o + W_ATT], qnb_ref)
                           * (inv_sqrt_d * LOG2E)).astype(BF16)
        o += W_ATT
        put(head_norm(p[:, o:o + W_ATT], knb_ref), kb_ref, kb16_ref, True)
        o += W_ATT
        put(p[:, o:o + W_ATT], vb_ref, vb16_ref, False)
        o += W_ATT
        qc_ref[rows, :] = p[:, o:o + W_C] * (DK_C ** -0.5)
        o += W_C
        f_raw = p[:, o:o + W_C]
        o += W_C
        log_sig = jnp.minimum(f_raw, 0.0) - jnp.log(1.0 + jnp.exp(-jnp.abs(f_raw)))
        b = log_1m_lb + log_sig
        lf_ref[rows, :] = jnp.maximum(log_lb, b) + jnp.log(1.0 + jnp.exp(-jnp.abs(log_lb - b)))
        kc_ref[rows, :] = (1.0 - lb) * (1.0 / (1.0 + jnp.exp(f_raw)))
        ic_ref[rows, :] = p[:, o:o + W_C].astype(BF16)
        o += W_C
        gc_ref[rows, :] = p[:, o:o + W_C]


def _proj(x2d, nm, w16, qna, kna, qnb, knb, lower_bounds, *, layer, tm, time_minor, carried=()):
    m, d = x2d.shape
    assert m % tm == 0 and len(carried) == (2 if time_minor and layer > 0 else 0)
    full = _resident
    row = lambda w, dt: (pl.BlockSpec((tm, w), lambda i: (i, 0)), jax.ShapeDtypeStruct((m, w), dt))
    col = lambda w, dt: (pl.BlockSpec((w, tm), lambda i: (0, i)), jax.ShapeDtypeStruct((w, m), dt))
    layers = lambda n, w: pl.BlockSpec((n, w, tm), lambda i: (0, 0, i))
    layered = lambda w, dt: (layers(layer + 1, w), jax.ShapeDtypeStruct((layer + 1, w, m), dt))
    kv = col if time_minor else row
    kv_a = layered if time_minor else row
    att_a = [row(W_ATT, BF16), kv_a(W_ATT, F32), kv_a(W_ATT, F32), kv(W_ATT, BF16), row(W_ATT, BF16)]
    att_b = [row(W_ATT, BF16), kv(W_ATT, F32), kv(W_ATT, F32), kv(W_ATT, BF16), row(W_ATT, BF16)]
    outs = att_a + att_b + [row(W_C, F32)] * 3 + [row(W_C, BF16), row(W_C, F32)]
    return pl.pallas_call(
        functools.partial(_proj_kernel, layer=layer, time_minor=time_minor),
        grid=(m // tm,),
        in_specs=[row(d, F32)[0], full(nm), _resident_layer(w16, layer), full(qna), full(kna),
                  full(qnb), full(knb),
                  full(lower_bounds)] + [layers(layer, W_ATT)] * len(carried),
        out_specs=[spec for spec, _ in outs],
        out_shape=[shape for _, shape in outs],
        compiler_params=_cparams(1),
        name="proj",
    )(x2d, nm, w16, qna, kna, qnb, knb, lower_bounds, *carried)


def _head_masks(width):
    lane_head = _iota((1, width), 1) // D_HEAD
    return [lane_head == h for h in range(N_HEADS)]


def _stack_heads(x, head_lane):
    return jnp.concatenate([jnp.where(m, x, jnp.zeros_like(x)) for m in head_lane], axis=0)


def _stack_heads_t(x_t, n):
    row_head = _iota((W_ATT, 1), 0) // D_HEAD
    return jnp.concatenate(
        [jnp.where(row_head == h, x_t, jnp.zeros_like(x_t)) for h in range(N_HEADS)], axis=1)


def _stack_groups(x, head_lane, gq):
    return jnp.concatenate(
        [_stack_heads(x[g:g + gq], head_lane) for g in range(0, x.shape[0], gq)], axis=0)


def _sb_block(z, weigh_values, mask, tri, carry_ref, acc_ref, gq):
    tk = z.shape[1]
    t = jnp.log(1.0 + jnp.exp2(jnp.abs(z) * -math.log2(math.e)))
    log_1m = jnp.minimum(-z, 0.0) - t
    log_beta = log_1m + z
    if mask is not None:
        log_1m = jnp.where(mask, log_1m, 0.0)
    hi = log_1m.astype(BF16)
    lo = (log_1m - hi.astype(F32)).astype(BF16)
    later = _dot(jnp.concatenate([hi, lo], axis=1), tri)
    carry = carry_ref[...]
    w = jnp.exp(log_beta + later + carry[:, :tk])
    if mask is not None:
        w = jnp.where(mask, w, 0.0)
    w = w.astype(BF16)
    blocks = [w[n * gq:(n + 1) * gq] for n in range(z.shape[0] // gq)]
    w_heads = jnp.concatenate(
        [jnp.concatenate(blocks[g:g + N_HEADS], axis=1) for g in range(0, len(blocks), N_HEADS)],
        axis=0)
    acc_ref[...] += weigh_values(w_heads)
    carry_ref[...] = carry + jnp.sum(log_1m, axis=-1, keepdims=True)


def _sb_live(carry_ref):
    return (jnp.max(carry_ref[...]) > -SB_UNDERFLOW).astype(jnp.int32)


def _strict_upper(n):
    return jnp.where(_iota((2 * n, n), 0) % n > _iota((2 * n, n), 1), 1.0, 0.0).astype(BF16)


def _sb_prompt_kernel(q_ref, kt_ref, v_ref, o_ref, carry_ref, acc_ref, *, tq):
    i = pl.program_id(0)
    head_lane = _head_masks(W_ATT)
    gq = SB_TK
    n_groups = tq // gq
    q_stack = _stack_groups(q_ref[...], head_lane, gq)
    tri = _strict_upper(SB_TK)
    carry_ref[...] = jnp.zeros_like(carry_ref)
    acc_ref[...] = jnp.zeros_like(acc_ref)

    def add_block(j, first_group, mask):
        r0 = pl.multiple_of(j * SB_TK, SB_TK)
        s0 = first_group * N_HEADS * gq
        z = _dot(q_stack[s0:], kt_ref[:, pl.ds(r0, SB_TK)])
        v_stack = _stack_heads(v_ref[pl.ds(r0, SB_TK), :], head_lane)
        _sb_block(z, lambda w: _dot(w, v_stack), mask, tri,
                  carry_ref.at[pl.ds(s0, z.shape[0])],
                  acc_ref.at[pl.ds(first_group * gq, (n_groups - first_group) * gq)], gq)

    for g in reversed(range(n_groups)):
        shape = ((n_groups - g) * N_HEADS * gq, SB_TK)
        row = _iota(shape, 0)
        add_block(i * n_groups + g, g, jnp.logical_or(_iota(shape, 1) < row % gq, row >= N_HEADS * gq))
    add_block(jnp.maximum(i * n_groups - 1, 0), 0,
              jnp.broadcast_to(i > 0, (n_groups * N_HEADS * gq, SB_TK)))

    def cond(c):
        j, live = c
        return jnp.logical_and(j >= 0, live > 0)

    def body(c):
        j, _ = c
        add_block(j, 0, None)
        return j - 1, _sb_live(carry_ref)

    lax.while_loop(cond, body, (i * n_groups - 2, _sb_live(carry_ref)))
    o_ref[...] = acc_ref[...].astype(o_ref.dtype)


def _sb_prompt(q16, kt16, v16, *, tq):
    t = q16.shape[0]
    assert t % tq == 0 and tq % SB_TK == 0
    return pl.pallas_call(
        functools.partial(_sb_prompt_kernel, tq=tq),
        grid=(t // tq,),
        in_specs=[pl.BlockSpec((tq, W_ATT), lambda i: (i, 0)),
                  _resident(kt16), _resident(v16)],
        out_specs=pl.BlockSpec((tq, W_ATT), lambda i: (i, 0)),
        out_shape=jax.ShapeDtypeStruct((t, W_ATT), BF16),
        scratch_shapes=[pltpu.VMEM((N_HEADS * tq, SB_TK), F32), pltpu.VMEM((tq, W_ATT), F32)],
        compiler_params=_cparams(1),
        name="sb_prompt",
    )(q16, kt16, v16)


def _sb_sample_kernel(q_ref, kn_ref, vn_ref, ckt_ref, cvt_ref, o_ref, carry_ref, acc_ref, *, ts, past):
    head_lane = _head_masks(W_ATT)
    q_stack = _stack_heads(q_ref[...], head_lane)
    carry_ref[...] = jnp.zeros_like(carry_ref)
    acc_ref[...] = jnp.zeros_like(acc_ref)
    shape = (N_HEADS * ts, ts)
    mask = _iota(shape, 1) < _iota(shape, 0) % ts
    v_new = _stack_heads(vn_ref[...], head_lane)
    _sb_block(_dot_nt(q_stack, kn_ref[...]), lambda w: _dot(w, v_new), mask, _strict_upper(ts),
              carry_ref, acc_ref, ts)
    tri = _strict_upper(SB_TK)

    def cond(c):
        j, live = c
        return jnp.logical_and(j >= 0, live > 0)

    def body(c):
        j, _ = c
        r0 = pl.multiple_of(j * SB_TK, SB_TK)
        z = _dot(q_stack, ckt_ref[0, 0, :, pl.ds(r0, SB_TK)].astype(BF16))
        vt_stack = _stack_heads_t(cvt_ref[0, 0, :, pl.ds(r0, SB_TK)].astype(BF16), SB_TK)
        _sb_block(z, lambda w: _dot_nt(w, vt_stack), None, tri, carry_ref, acc_ref, ts)
        return j - 1, _sb_live(carry_ref)

    lax.while_loop(cond, body, (past // SB_TK - 1, _sb_live(carry_ref)))
    o_ref[...] = acc_ref[...].astype(o_ref.dtype)


def _sb_sample(q16, kn16, vn16, cache_kt, cache_vt, *, layer, nb, ts):
    past = cache_kt.shape[3]
    assert past % SB_TK == 0
    row = pl.BlockSpec((ts, W_ATT), lambda b: (b, 0))
    cache = pl.BlockSpec((1, 1, W_ATT, past), lambda b: (layer, b, 0, 0))
    return pl.pallas_call(
        functools.partial(_sb_sample_kernel, ts=ts, past=past),
        grid=(nb,),
        in_specs=[row, row, row, cache, cache],
        out_specs=row,
        out_shape=jax.ShapeDtypeStruct((nb * ts, W_ATT), BF16),
        scratch_shapes=[pltpu.VMEM((N_HEADS * ts, SB_TK), F32), pltpu.VMEM((ts, W_ATT), F32)],
        compiler_params=_cparams(1),
        name="sb_sample",
    )(q16, kn16, vn16, cache_kt, cache_vt)


def _rel_bias(table_ref, h, dist):
    idx = jnp.clip(dist, -REL_CLIP, REL_CLIP) + REL_CLIP

    def body(r, b):
        return jnp.where(idx == r, table_ref[h, r], b)

    return lax.fori_loop(0, N_REL, body, jnp.zeros(dist.shape, F32), unroll=8)


def _row_reduce(x, combine, reduce):
    width = x.shape[1]
    acc = x[:, :128]
    for c in range(128, width, 128):
        acc = combine(acc, x[:, c:c + 128])
    return reduce(acc, axis=-1, keepdims=True)


def _toeplitz_bias(table_ref, h, rows, back, width):
    lane = _iota((8, width), 1)
    by_diag = _rel_bias(table_ref, h, back + rows - lane) * LOG2E
    tiled = jnp.concatenate([by_diag] * (rows // 8), axis=0)
    return pltpu.roll(tiled, width - rows, 1, stride=1, stride_axis=0)


BAND_TQ = 128
BAND_BACK = N_PREV_CHUNKS * CHUNK
BAND_WIN = BAND_BACK + BAND_TQ


def _band_prompt_kernel(table_ref, q_ref, *rest, tq):
    n_kb = (len(rest) - 2) // 2
    k_refs, v_refs, (o_ref, bias_ref) = rest[:n_kb], rest[n_kb:2 * n_kb], rest[2 * n_kb:]
    i = pl.program_id(0)
    rows = _iota((BAND_TQ, BAND_WIN), 0)
    cols = _iota((BAND_TQ, BAND_WIN), 1)

    @pl.when(i == 0)
    def _():
        q_chunk = rows // CHUNK + N_PREV_CHUNKS
        k_chunk = cols // CHUNK
        in_band = jnp.logical_and(k_chunk >= q_chunk - N_PREV_CHUNKS, k_chunk <= q_chunk)
        for h in range(N_HEADS):
            bias = _toeplitz_bias(table_ref, h, BAND_TQ, BAND_BACK, BAND_WIN + BAND_TQ)
            bias_ref[h] = jnp.where(in_band, bias[:, :BAND_WIN], NEG_BIG)

    q = q_ref[...]
    head_lane = _head_masks(W_ATT)
    k_t = jnp.concatenate([r[...] for r in k_refs], axis=1)
    vcat = jnp.concatenate([r[...] for r in v_refs], axis=0)
    parts = [p * BAND_TQ for p in range(tq // BAND_TQ)]
    starts = [r0 + (n_kb - 1) * tq - BAND_BACK for r0 in parts]
    scores = []
    for p in range(len(parts)):
        qp = q[parts[p]:parts[p] + BAND_TQ]
        k_win = k_t[:, starts[p]:starts[p] + BAND_WIN]
        scores.append([_dot(jnp.where(head_lane[h], qp, jnp.zeros_like(qp)), k_win)
                       for h in range(N_HEADS)])
    for p in range(len(parts)):
        exists = (i - (n_kb - 1)) * tq + starts[p] + cols >= 0
        weights = []
        inv_den = []
        for h, s in enumerate(scores[p]):
            s = jnp.where(exists, s + bias_ref[h], NEG_BIG)
            e = jnp.exp2(s - _row_reduce(s, jnp.maximum, jnp.max))
            weights.append(e.astype(BF16))
            inv_den.append(1.0 / _row_reduce(e, jnp.add, jnp.sum))
        v_win = vcat[starts[p]:starts[p] + BAND_WIN]
        out = jnp.zeros((BAND_TQ, W_ATT), F32)
        for h in range(N_HEADS):
            out = jnp.where(head_lane[h], _dot(weights[h], v_win) * inv_den[h], out)
        o_ref[parts[p]:parts[p] + BAND_TQ, :] = out.astype(o_ref.dtype)


def _band_prompt(q16, kt16, v16, table, *, tq):
    t = q16.shape[0]
    assert t % tq == 0 and tq % BAND_TQ == 0
    n_kb = 1 + pl.cdiv(BAND_BACK, tq)
    blk = lambda back: pl.BlockSpec((tq, W_ATT), lambda i: (jnp.maximum(i - back, 0), 0))
    blk_t = lambda back: pl.BlockSpec((W_ATT, tq), lambda i: (0, jnp.maximum(i - back, 0)))
    oldest_first = list(reversed(range(n_kb)))
    return pl.pallas_call(
        functools.partial(_band_prompt_kernel, tq=tq),
        grid=(t // tq,),
        in_specs=[pl.BlockSpec(memory_space=pltpu.SMEM), blk(0)]
                 + [blk_t(b) for b in oldest_first] + [blk(b) for b in oldest_first],
        out_specs=blk(0),
        out_shape=jax.ShapeDtypeStruct((t, W_ATT), BF16),
        scratch_shapes=[pltpu.VMEM((N_HEADS, BAND_TQ, BAND_WIN), F32)],
        compiler_params=_cparams(1),
        name="band_prompt",
    )(table, q16, *([kt16] * n_kb), *([v16] * n_kb))


def _band_sample_kernel(table_ref, q_ref, kn_ref, vn_ref, ckt_ref, cvt_ref, o_ref, bias_ref, *, ts, past):
    lb = ckt_ref.shape[3]

    @pl.when(pl.program_id(0) == 0)
    def _():
        for h in range(N_HEADS):
            bias_ref[h] = _toeplitz_bias(table_ref, h, ts, lb, bias_ref.shape[2])

    q = q_ref[...]
    head_lane = _head_masks(W_ATT)
    kc_t = ckt_ref[0, 0].astype(BF16)
    vc_t = cvt_ref[0, 0].astype(BF16)
    kn = kn_ref[...]
    vn = vn_ref[...]
    q_pos_c = past + _iota((ts, lb), 0)
    k_pos_c = past - lb + _iota((ts, lb), 1)
    q_pos_n = past + _iota((ts, ts), 0)
    k_pos_n = past + _iota((ts, ts), 1)

    def allowed(q_pos, k_pos):
        qc, kc_ = q_pos // CHUNK, k_pos // CHUNK
        return jnp.logical_and(kc_ >= qc - N_PREV_CHUNKS, kc_ <= qc)

    ok_c = allowed(q_pos_c, k_pos_c)
    ok_n = allowed(q_pos_n, k_pos_n)
    out = jnp.zeros((ts, W_ATT), F32)
    for h in range(N_HEADS):
        qh = jnp.where(head_lane[h], q, jnp.zeros_like(q))
        bias = bias_ref[h]
        s_c = jnp.where(ok_c, _dot(qh, kc_t) + bias[:, :lb], NEG_BIG)
        s_n = jnp.where(ok_n, _dot_nt(qh, kn) + bias[:, lb:lb + ts], NEG_BIG)
        m = jnp.maximum(jnp.max(s_c, axis=-1, keepdims=True), jnp.max(s_n, axis=-1, keepdims=True))
        e_c = jnp.exp2(s_c - m)
        e_n = jnp.exp2(s_n - m)
        den = jnp.sum(e_c, axis=-1, keepdims=True) + jnp.sum(e_n, axis=-1, keepdims=True)
        pv = (_dot_nt(e_c.astype(BF16), vc_t) + _dot(e_n.astype(BF16), vn)) / den
        out = jnp.where(head_lane[h], pv, out)
    o_ref[...] = out.astype(o_ref.dtype)


def _band_sample(q16, kn16, vn16, cache_kt, cache_vt, table, *, layer, nb, ts, past):
    lb = cache_kt.shape[3]
    row = pl.BlockSpec((ts, W_ATT), lambda b: (b, 0))
    cache = pl.BlockSpec((1, 1, W_ATT, lb), lambda b: (layer, b, 0, 0))
    return pl.pallas_call(
        functools.partial(_band_sample_kernel, ts=ts, past=past),
        grid=(nb,),
        in_specs=[pl.BlockSpec(memory_space=pltpu.SMEM), row, row, row, cache, cache],
        out_specs=row,
        out_shape=jax.ShapeDtypeStruct((nb * ts, W_ATT), BF16),
        scratch_shapes=[pltpu.VMEM((N_HEADS, ts, pl.cdiv(lb + 2 * ts, 128) * 128), F32)],
        compiler_params=_cparams(1),
        name="band_sample",
    )(table, q16, kn16, vn16, cache_kt, cache_vt)


def _hgrn_spans(q, k, lf, v16, st_ref):
    tr = q.shape[0]
    heads = [slice(h * DK_C, (h + 1) * DK_C) for h in range(N_HEADS)]
    tri = jnp.where(_iota((tr, tr), 0) >= _iota((tr, tr), 1), 1.0, 0.0).astype(BF16)
    f1, f2, f3 = _split3(lf)
    g = (_dot(tri, f1) + _dot(tri, f2) + _dot(tri, f3)) * math.log2(math.e)

    g_end = g[tr - 1:tr, :]
    qt = (q * jnp.exp2(g)).astype(BF16)
    kd = (k * jnp.exp2(g_end - g)).astype(BF16)
    dec = jnp.exp2(g_end)
    states = [st_ref[h] for h in range(N_HEADS)]
    o_heads = [_dot_nt(qt[:, hs], st.astype(BF16)) for hs, st in zip(heads, states)]
    grown = [_dot_tn(v16[:, hs], kd[:, hs]) for hs in heads]
    for h, hs in enumerate(heads):
        st_ref[h] = states[h] * dec[:, hs] + grown[h]
    o = jnp.concatenate(o_heads, axis=1)

    sizes = []
    h = HGRN_DIAG
    while h < tr:
        sizes.append(h)
        h *= 2
    split = lambda x, h: x.reshape(tr // (2 * h), 2 * h, W_C)
    operands = []
    for h in sizes:
        g3, q3, k3, v3 = split(g, h), split(q, h), split(k, h), split(v16, h)
        c = g3[:, h - 1:h, :]
        qe = (q3[:, h:] * jnp.exp2(g3[:, h:] - c)).reshape(tr // 2, W_C).astype(BF16)
        ke = (k3[:, :h] * jnp.exp2(c - g3[:, :h])).reshape(tr // 2, W_C).astype(BF16)
        operands.append((qe, ke, v3[:, :h].reshape(tr // 2, W_C)))
    pair = [[_dot_nt(qe[:, hs], ke[:, hs]) for hs in heads] for qe, ke, _ in operands]
    span_of_row = _iota((tr // 2, tr // 2), 0)
    span_of_col = _iota((tr // 2, tr // 2), 1)
    for n, h in enumerate(sizes):
        if 2 * h < tr:
            same_span = (span_of_row // h) == (span_of_col // h)
            pair[n] = [jnp.where(same_span, p, 0.0) for p in pair[n]]
    added = [jnp.concatenate([_dot(p.astype(BF16), vl[:, hs]) for p, hs in zip(pair[n], heads)],
                             axis=1) for n, (_, _, vl) in enumerate(operands)]
    for n, h in enumerate(sizes):
        o3 = split(o, h)
        upper = o3[:, h:] + added[n].reshape(tr // (2 * h), h, W_C)
        o = jnp.concatenate([o3[:, :h], upper], axis=1).reshape(tr, W_C)
    return o, g


def _hgrn_diagonal(g2, q2, k2, v16):
    n = HGRN_DIAG
    lane = _iota((n, DK_C), 1)
    weights = [[jnp.zeros((n, DK_C), F32) for _ in range(N_HEADS)] for _ in range(2)]
    for s in range(2):
        g8, q8, k8 = g2[s * n:(s + 1) * n], q2[s * n:(s + 1) * n], k2[s * n:(s + 1) * n]
        for j in range(n):
            p = q8 * (k8[j:j + 1, :] * jnp.exp2(g8 - g8[j:j + 1, :]))
            for h in range(N_HEADS):
                a = jnp.sum(p[:, h * DK_C:(h + 1) * DK_C], axis=-1, keepdims=True)
                weights[s][h] = jnp.where(lane == s * n + j, a, weights[s][h])
    row = _iota((2 * n, DK_C), 0)
    col = _iota((2 * n, DK_C), 1)
    keep = jnp.logical_and(row >= col, row // n == col // n)
    out = []
    for h in range(N_HEADS):
        a = jnp.where(keep, jnp.concatenate([weights[0][h], weights[1][h]], axis=0), 0.0)
        out.append(_dot(a[:, :2 * n].astype(BF16), v16[:, h * DK_C:(h + 1) * DK_C]))
    return jnp.concatenate(out, axis=1)


def _hgrn_block(q, k, lf, v16, st_ref):
    o, g = _hgrn_spans(q, k, lf, v16, st_ref)
    pairs = [slice(r0, r0 + 2 * HGRN_DIAG) for r0 in range(0, q.shape[0], 2 * HGRN_DIAG)]
    return o + jnp.concatenate([_hgrn_diagonal(g[s], q[s], k[s], v16[s]) for s in pairs], axis=0)


def _hgrn_kernel(q_ref, k_ref, lf_ref, v_ref, s0_ref, o_ref, sout_ref, st_ref):
    r = pl.program_id(1)

    @pl.when(r == 0)
    def _():
        for h in range(N_HEADS):
            st_ref[h] = s0_ref[0, h].T

    o_ref[...] = _hgrn_block(q_ref[...], k_ref[...], lf_ref[...], v_ref[...], st_ref)

    @pl.when(r == pl.num_programs(1) - 1)
    def _():
        for h in range(N_HEADS):
            sout_ref[0, h] = st_ref[h].T


def _hgrn(qc, kc, lf, ic, s0, *, nb, t, tr):
    assert t % tr == 0 and tr % (2 * HGRN_DIAG) == 0 and tr & (tr - 1) == 0
    nr = t // tr
    row = pl.BlockSpec((tr, W_C), lambda b, r: (b * nr + r, 0))
    state = pl.BlockSpec((1, N_HEADS, DK_C, DK_C), lambda b, r: (b, 0, 0, 0))
    return pl.pallas_call(
        _hgrn_kernel,
        grid=(nb, nr),
        in_specs=[row, row, row, row, state],
        out_specs=[row, state],
        out_shape=[jax.ShapeDtypeStruct((nb * t, W_C), F32),
                   jax.ShapeDtypeStruct((nb, N_HEADS, DK_C, DK_C), F32)],
        scratch_shapes=[pltpu.VMEM((N_HEADS, DK_C, DK_C), F32)],
        compiler_params=_cparams(2),
        name="hgrn",
    )(qc, kc, lf, ic, s0)


def _merge_block(x, oa, ob, oc, gc, on, wo_ref, nf, wu_ref, wd_ref):
    parts = []
    for h in range(N_HEADS):
        och = oc[:, h * DK_C:(h + 1) * DK_C]
        ms = jnp.mean(och * och, axis=-1, keepdims=True)
        parts.append(och * lax.rsqrt(ms + EPS) * on)
    ocn = jnp.concatenate(parts, axis=-1) * (gc * (1.0 / (1.0 + jnp.exp(-gc))))
    mixed = jnp.concatenate([oa, ob, ocn.astype(BF16)], axis=-1)
    h_res = x + _dot(mixed, wo_ref[0])
    ms = jnp.mean(h_res * h_res, axis=-1, keepdims=True)
    hn = (h_res * lax.rsqrt(ms + EPS) * nf).astype(BF16)
    y = h_res
    for c in range(0, wu_ref.shape[2], FFN_CHUNK):
        u = jnp.maximum(_dot(hn, wu_ref[0, :, c:c + FFN_CHUNK]), 0.0)
        y = y + _dot((u * u).astype(BF16), wd_ref[0, c:c + FFN_CHUNK, :])
    return y


def _merge_kernel(x_ref, oa_ref, ob_ref, oc_ref, gc_ref, on_ref, wo_ref, nf_ref, wu_ref, wd_ref,
                  y_ref):
    y_ref[...] = _merge_block(x_ref[...], oa_ref[...], ob_ref[...], oc_ref[...], gc_ref[...],
                              on_ref[...], wo_ref, nf_ref[...], wu_ref, wd_ref)


def _merge(x2d, oa, ob, oc, gc, onorm, wo16, nf, wu16, wd16, *, layer, tm):
    m, d = x2d.shape
    assert m % tm == 0
    row = lambda w: pl.BlockSpec((tm, w), lambda i: (i, 0))
    full = _resident
    return pl.pallas_call(
        _merge_kernel,
        grid=(m // tm,),
        in_specs=[row(d), row(W_ATT), row(W_ATT), row(W_C), row(W_C),
                  full(onorm), _resident_layer(wo16, layer), full(nf),
                  _resident_layer(wu16, layer), _resident_layer(wd16, layer)],
        out_specs=row(d),
        out_shape=jax.ShapeDtypeStruct((m, d), F32),
        compiler_params=_cparams(1),
        name="merge_ffn",
    )(x2d, oa, ob, oc, gc, onorm, wo16, nf, wu16, wd16)


def kernel(x_prompt, x_sample, cache_a_k, cache_a_v, cache_b_k, cache_b_v, state_c, norm_mix, w_in, qnorm_a, knorm_a, qnorm_b, knorm_b, rel_bias_b, lower_bounds, onorm_c, w_o, norm_ffn, w_up, w_down):
    depth = w_in.shape[0]
    bp, tp, d = x_prompt.shape
    bs, ts, _ = x_sample.shape
    past = cache_a_k.shape[2]
    band_rows_p = min(N_PREV_CHUNKS * CHUNK, tp)
    assert bp == 1

    xp = x_prompt.reshape(bp * tp, d)
    xs = x_sample.reshape(bs * ts, d)
    time_minor = lambda c: jnp.transpose(c, (0, 1, 3, 4, 2)).reshape(depth, bs, W_ATT, c.shape[2])
    cak, cav, cbk, cbv = (time_minor(c) for c in (cache_a_k, cache_a_v, cache_b_k, cache_b_v))
    heads_of = lambda a_t, n: jnp.transpose(
        a_t.reshape(N_HEADS, D_HEAD, n, a_t.shape[1] // n), (2, 3, 0, 1))
    lbounds = lower_bounds.astype(F32)
    w16, wo16, wu16, wd16 = (w.astype(BF16) for w in (w_in, w_o, w_up, w_down))
    zero_state = jnp.zeros((bp, N_HEADS, DK_C, DK_C), F32)

    tm_p = min(512, tp)
    tm_merge = min(1024, tp)
    tq_band = min(1024, tp)
    tr_p = min(256, tp)

    outs_p = [[] for _ in range(5)]
    outs_s = [[] for _ in range(5)]
    carried = ()
    for l in range(depth):
        nm = norm_mix[l].reshape(1, d)
        nf = norm_ffn[l].reshape(1, d)
        tile_h = lambda g: jnp.tile(g.reshape(1, D_HEAD), (1, N_HEADS))
        qna, kna, qnb, knb = (tile_h(g[l]) for g in (qnorm_a, knorm_a, qnorm_b, knorm_b))
        onorm = onorm_c[l].reshape(1, DK_C)
        table = rel_bias_b[l].astype(F32)

        (qa, ka_all, va_all, ka16_t, va16, qb, kb_t, vb_t, kb16_t, vb16, qc, kc, lf, ic, gc) = _proj(
            xp, nm, w16, qna, kna, qnb, knb, lbounds, layer=l, tm=tm_p, time_minor=True,
            carried=carried)
        carried = (ka_all, va_all)
        oa = _sb_prompt(qa, ka16_t, va16, tq=min(SB_TQ, tp))
        ob = _band_prompt(qb, kb16_t, vb16, table, tq=tq_band)
        oc, s_p = _hgrn(qc, kc, lf, ic, zero_state, nb=bp, t=tp, tr=tr_p)
        xp = _merge(xp, oa, ob, oc, gc, onorm, wo16, nf, wu16, wd16, layer=l, tm=tm_merge)
        outs_p[2].append(heads_of(kb_t, bp)[:, tp - band_rows_p:])
        outs_p[3].append(heads_of(vb_t, bp)[:, tp - band_rows_p:])
        outs_p[4].append(s_p)

        head = lambda a, n: a.reshape(n, -1, N_HEADS, D_HEAD)
        (qa, ka, va, ka16, va16, qb, kb, vb, kb16, vb16, qc, kc, lf, ic, gc) = _proj(
            xs, nm, w16, qna, kna, qnb, knb, lbounds, layer=l, tm=bs * ts, time_minor=False)
        oa = _sb_sample(qa, ka16, va16, cak, cav, layer=l, nb=bs, ts=ts)
        ob = _band_sample(qb, kb16, vb16, cbk, cbv, table, layer=l, nb=bs, ts=ts, past=past)
        oc, s_s = _hgrn(qc, kc, lf, ic, state_c[l].astype(F32), nb=bs, t=ts, tr=ts)
        xs = _merge(xs, oa, ob, oc, gc, onorm, wo16, nf, wu16, wd16, layer=l, tm=bs * ts)
        outs_s[0].append(head(ka, bs))
        outs_s[1].append(head(va, bs))
        outs_s[2].append(head(kb, bs))
        outs_s[3].append(head(vb, bs))
        outs_s[4].append(s_s)

    stack = lambda xs_: jnp.stack(xs_)
    all_heads_of = lambda a: jnp.transpose(
        a.reshape(depth, N_HEADS, D_HEAD, bp, tp), (0, 3, 4, 1, 2))
    ka_all, va_all = carried
    return (xp.reshape(bp, tp, d), xs.reshape(bs, ts, d),
            all_heads_of(ka_all), all_heads_of(va_all),
            stack(outs_p[2]), stack(outs_p[3]), stack(outs_p[4]),
            stack(outs_s[0]), stack(outs_s[1]), stack(outs_s[2]), stack(outs_s[3]), stack(outs_s[4]))
```

```python
import functools
import math

import jax
import jax.numpy as jnp
from jax import lax
from jax.experimental import pallas as pl
from jax.experimental.pallas import tpu as pltpu

F32 = jnp.float32
BF16 = jnp.bfloat16

D_HEAD = 64
N_HEADS = 4
W_ATT = N_HEADS * D_HEAD
DK_C = 128
W_C = N_HEADS * DK_C
CHUNK = 64
N_PREV_CHUNKS = 8
REL_CLIP = 128
N_REL = 2 * REL_CLIP + 1
EPS = 1e-6
NEG_BIG = -1e30
LB_FLOOR = 1e-30
LOG2E = math.log2(math.e)
SB_UNDERFLOW = 88.0
PROJ_PART = 256
FFN_CHUNK = 1024
SB_TK = 128
SB_TQ = 256
HGRN_DIAG = 8
VMEM_LIMIT = 60 * 1024 * 1024


def _cparams(n_axes):
    return pltpu.CompilerParams(dimension_semantics=("arbitrary",) * n_axes,
                                vmem_limit_bytes=VMEM_LIMIT)


def _resident(a):
    zeros = (0,) * a.ndim
    return pl.BlockSpec(a.shape, lambda *_: zeros, pipeline_mode=pl.Buffered(1))


def _resident_layer(a, layer):
    index = (layer,) + (0,) * (a.ndim - 1)
    return pl.BlockSpec((1,) + a.shape[1:], lambda *_: index, pipeline_mode=pl.Buffered(1))


def _split3(x):
    h1 = x.astype(BF16)
    r1 = x - h1.astype(F32)
    h2 = r1.astype(BF16)
    h3 = (r1 - h2.astype(F32)).astype(BF16)
    return h1, h2, h3


def _dot(a, b):
    return jnp.dot(a, b, preferred_element_type=F32)


def _dot_nt(a, b):
    return lax.dot_general(a, b, (((1,), (1,)), ((), ())), preferred_element_type=F32)


def _dot_tn(a, b):
    return lax.dot_general(a, b, (((0,), (0,)), ((), ())), preferred_element_type=F32)


def _iota(shape, dim):
    return lax.broadcasted_iota(jnp.int32, shape, dim)


def _proj_kernel(x_ref, nm_ref, w_ref, qna_ref, kna_ref, qnb_ref, knb_ref, lb_ref, *rest,
                 layer, time_minor, n_carried, n_cast):
    carried_refs = rest[:n_carried]
    cast_in_refs = rest[n_carried:n_carried + n_cast]
    (qa_ref, ka_ref, va_ref, ka16_ref, va16_ref, qb_ref, kb_ref, vb_ref, kb16_ref, vb16_ref,
     qc_ref, kc_ref, lf_ref, ic_ref, gc_ref) = rest[n_carried + n_cast:n_carried + n_cast + 15]
    cast_out_refs = rest[n_carried + n_cast + 15:]
    for f32_ref, b16_ref in zip(cast_in_refs, cast_out_refs):
        b16_ref[...] = f32_ref[...].astype(BF16)
    if time_minor:
        for earlier_ref, all_ref in zip(carried_refs, (ka_ref, va_ref)):
            all_ref[0:layer] = earlier_ref[...]
        ka_ref, va_ref = ka_ref.at[layer], va_ref.at[layer]
    x = x_ref[...]
    ms = jnp.mean(x * x, axis=-1, keepdims=True)
    xn = (x * lax.rsqrt(ms + EPS) * nm_ref[...]).astype(BF16)

    tm = x.shape[0]
    part = PROJ_PART if tm % PROJ_PART == 0 else tm
    projected = [_dot(xn[r0:r0 + part], w_ref[0]) for r0 in range(0, tm, part)]

    same_head = (_iota((W_ATT, W_ATT), 0) // D_HEAD) == (_iota((W_ATT, W_ATT), 1) // D_HEAD)
    seg_mean = jnp.where(same_head, 1.0 / D_HEAD, 0.0).astype(BF16)
    inv_sqrt_d = 1.0 / math.sqrt(D_HEAD)

    lbr = lb_ref[...]
    e = jnp.exp(lbr - jnp.max(lbr, axis=0, keepdims=True))
    sm = e / jnp.sum(e, axis=0, keepdims=True)
    lb = jnp.sum(sm[0:layer + 1], axis=0, keepdims=True) - sm[0:1]
    log_lb = jnp.log(jnp.maximum(lb, LB_FLOOR))
    log_1m_lb = jnp.log1p(-lb)

    def head_norm(p, g_ref):
        m = _dot((p * p).astype(BF16), seg_mean)
        return p * lax.rsqrt(m + EPS) * g_ref[...]

    for n, p in enumerate(projected):
        rows = slice(n * part, (n + 1) * part)

        def put(x, f32_ref, b16_ref, b16_time_minor):
            x_t = x.T if time_minor else None
            if time_minor:
                f32_ref[:, rows] = x_t
            else:
                f32_ref[rows, :] = x
            if time_minor and b16_time_minor:
                b16_ref[:, rows] = x_t.astype(BF16)
            else:
                b16_ref[rows, :] = x.astype(BF16)

        o = 0
        qa_ref[rows, :] = (head_norm(p[:, o:o + W_ATT], qna_ref) * inv_sqrt_d).astype(BF16)
        o += W_ATT
        put(head_norm(p[:, o:o + W_ATT], kna_ref), ka_ref, ka16_ref, True)
        o += W_ATT
        put(p[:, o:o + W_ATT], va_ref, va16_ref, False)
        o += W_ATT
        qb_ref[rows, :] = (head_norm(p[:, o:o + W_ATT], qnb_ref)
                           * (inv_sqrt_d * LOG2E)).astype(BF16)
        o += W_ATT
        put(head_norm(p[:, o:o + W_ATT], knb_ref), kb_ref, kb16_ref, True)
        o += W_ATT
        put(p[:, o:o + W_ATT], vb_ref, vb16_ref, False)
        o += W_ATT
        qc_ref[rows, :] = p[:, o:o + W_C] * (DK_C ** -0.5)
        o += W_C
        f_raw = p[:, o:o + W_C]
        o += W_C
        log_sig = jnp.minimum(f_raw, 0.0) - jnp.log(1.0 + jnp.exp(-jnp.abs(f_raw)))
        b = log_1m_lb + log_sig
        lf_ref[rows, :] = jnp.maximum(log_lb, b) + jnp.log(1.0 + jnp.exp(-jnp.abs(log_lb - b)))
        kc_ref[rows, :] = (1.0 - lb) * (1.0 / (1.0 + jnp.exp(f_raw)))
        ic_ref[rows, :] = p[:, o:o + W_C].astype(BF16)
        o += W_C
        gc_ref[rows, :] = p[:, o:o + W_C]


def _proj(x2d, nm, w16, qna, kna, qnb, knb, lower_bounds, *, layer, tm, time_minor, carried=(),
          cast=()):
    m, d = x2d.shape
    steps = m // tm
    assert m % tm == 0 and len(carried) == (2 if time_minor and layer > 0 else 0)
    assert all(a.shape[1] % (16 * steps) == 0 for a in cast)
    slab_in = lambda a: pl.BlockSpec((1, a.shape[1] // steps, a.shape[2]), lambda i: (layer, i, 0))
    slab_out = lambda a: (pl.BlockSpec((1, a.shape[1] // steps, a.shape[2]), lambda i: (0, i, 0)),
                          jax.ShapeDtypeStruct((1,) + a.shape[1:], BF16))
    full = _resident
    row = lambda w, dt: (pl.BlockSpec((tm, w), lambda i: (i, 0)), jax.ShapeDtypeStruct((m, w), dt))
    col = lambda w, dt: (pl.BlockSpec((w, tm), lambda i: (0, i)), jax.ShapeDtypeStruct((w, m), dt))
    layers = lambda n, w: pl.BlockSpec((n, w, tm), lambda i: (0, 0, i))
    layered = lambda w, dt: (layers(layer + 1, w), jax.ShapeDtypeStruct((layer + 1, w, m), dt))
    kv = col if time_minor else row
    kv_a = layered if time_minor else row
    att_a = [row(W_ATT, BF16), kv_a(W_ATT, F32), kv_a(W_ATT, F32), kv(W_ATT, BF16), row(W_ATT, BF16)]
    att_b = [row(W_ATT, BF16), kv(W_ATT, F32), kv(W_ATT, F32), kv(W_ATT, BF16), row(W_ATT, BF16)]
    outs = att_a + att_b + [row(W_C, F32)] * 3 + [row(W_C, BF16), row(W_C, F32)]
    outs += [slab_out(a) for a in cast]
    return pl.pallas_call(
        functools.partial(_proj_kernel, layer=layer, time_minor=time_minor,
                          n_carried=len(carried), n_cast=len(cast)),
        grid=(steps,),
        in_specs=[row(d, F32)[0], full(nm), _resident_layer(w16, layer), full(qna), full(kna),
                  full(qnb), full(knb), full(lower_bounds)]
                 + [layers(layer, W_ATT)] * len(carried) + [slab_in(a) for a in cast],
        out_specs=[spec for spec, _ in outs],
        out_shape=[shape for _, shape in outs],
        compiler_params=_cparams(1),
        name="proj",
    )(x2d, nm, w16, qna, kna, qnb, knb, lower_bounds, *carried, *cast)


def _head_masks(width):
    lane_head = _iota((1, width), 1) // D_HEAD
    return [lane_head == h for h in range(N_HEADS)]


def _stack_heads(x, head_lane):
    return jnp.concatenate([jnp.where(m, x, jnp.zeros_like(x)) for m in head_lane], axis=0)


def _stack_heads_t(x_t, n):
    row_head = _iota((W_ATT, 1), 0) // D_HEAD
    return jnp.concatenate(
        [jnp.where(row_head == h, x_t, jnp.zeros_like(x_t)) for h in range(N_HEADS)], axis=1)


def _stack_groups(x, head_lane, gq):
    return jnp.concatenate(
        [_stack_heads(x[g:g + gq], head_lane) for g in range(0, x.shape[0], gq)], axis=0)


def _sb_block(z, weigh_values, mask, tri, carry_ref, acc_ref, gq):
    tk = z.shape[1]
    t = jnp.log(1.0 + jnp.exp2(jnp.abs(z) * -math.log2(math.e)))
    log_1m = jnp.minimum(-z, 0.0) - t
    log_beta = log_1m + z
    if mask is not None:
        log_1m = jnp.where(mask, log_1m, 0.0)
    hi = log_1m.astype(BF16)
    lo = (log_1m - hi.astype(F32)).astype(BF16)
    later = _dot(jnp.concatenate([hi, lo], axis=1), tri)
    carry = carry_ref[...]
    w = jnp.exp(log_beta + later + carry[:, :tk])
    if mask is not None:
        w = jnp.where(mask, w, 0.0)
    w = w.astype(BF16)
    blocks = [w[n * gq:(n + 1) * gq] for n in range(z.shape[0] // gq)]
    w_heads = jnp.concatenate(
        [jnp.concatenate(blocks[g:g + N_HEADS], axis=1) for g in range(0, len(blocks), N_HEADS)],
        axis=0)
    acc_ref[...] += weigh_values(w_heads)
    carry_ref[...] = carry + jnp.sum(log_1m, axis=-1, keepdims=True)


def _sb_live(carry_ref):
    return (jnp.max(carry_ref[...]) > -SB_UNDERFLOW).astype(jnp.int32)


def _strict_upper(n):
    return jnp.where(_iota((2 * n, n), 0) % n > _iota((2 * n, n), 1), 1.0, 0.0).astype(BF16)


def _sb_prompt_kernel(q_ref, kt_ref, v_ref, o_ref, carry_ref, acc_ref, *, tq):
    i = pl.program_id(0)
    head_lane = _head_masks(W_ATT)
    gq = SB_TK
    n_groups = tq // gq
    q_stack = _stack_groups(q_ref[...], head_lane, gq)
    tri = _strict_upper(SB_TK)
    carry_ref[...] = jnp.zeros_like(carry_ref)
    acc_ref[...] = jnp.zeros_like(acc_ref)

    def add_block(j, first_group, mask):
        r0 = pl.multiple_of(j * SB_TK, SB_TK)
        s0 = first_group * N_HEADS * gq
        z = _dot(q_stack[s0:], kt_ref[:, pl.ds(r0, SB_TK)])
        v_stack = _stack_heads(v_ref[pl.ds(r0, SB_TK), :], head_lane)
        _sb_block(z, lambda w: _dot(w, v_stack), mask, tri,
                  carry_ref.at[pl.ds(s0, z.shape[0])],
                  acc_ref.at[pl.ds(first_group * gq, (n_groups - first_group) * gq)], gq)

    for g in reversed(range(n_groups)):
        shape = ((n_groups - g) * N_HEADS * gq, SB_TK)
        row = _iota(shape, 0)
        add_block(i * n_groups + g, g, jnp.logical_or(_iota(shape, 1) < row % gq, row >= N_HEADS * gq))
    add_block(jnp.maximum(i * n_groups - 1, 0), 0,
              jnp.broadcast_to(i > 0, (n_groups * N_HEADS * gq, SB_TK)))

    def cond(c):
        j, live = c
        return jnp.logical_and(j >= 0, live > 0)

    def body(c):
        j, _ = c
        add_block(j, 0, None)
        return j - 1, _sb_live(carry_ref)

    lax.while_loop(cond, body, (i * n_groups - 2, _sb_live(carry_ref)))
    o_ref[...] = acc_ref[...].astype(o_ref.dtype)


def _sb_prompt(q16, kt16, v16, *, tq):
    t = q16.shape[0]
    assert t % tq == 0 and tq % SB_TK == 0
    return pl.pallas_call(
        functools.partial(_sb_prompt_kernel, tq=tq),
        grid=(t // tq,),
        in_specs=[pl.BlockSpec((tq, W_ATT), lambda i: (i, 0)),
                  _resident(kt16), _resident(v16)],
        out_specs=pl.BlockSpec((tq, W_ATT), lambda i: (i, 0)),
        out_shape=jax.ShapeDtypeStruct((t, W_ATT), BF16),
        scratch_shapes=[pltpu.VMEM((N_HEADS * tq, SB_TK), F32), pltpu.VMEM((tq, W_ATT), F32)],
        compiler_params=_cparams(1),
        name="sb_prompt",
    )(q16, kt16, v16)


def _sb_sample_kernel(q_ref, kn_ref, vn_ref, ckt_ref, cvt_ref, o_ref, carry_ref, acc_ref, *, ts, past):
    head_lane = _head_masks(W_ATT)
    q_stack = _stack_heads(q_ref[...], head_lane)
    carry_ref[...] = jnp.zeros_like(carry_ref)
    acc_ref[...] = jnp.zeros_like(acc_ref)
    shape = (N_HEADS * ts, ts)
    mask = _iota(shape, 1) < _iota(shape, 0) % ts
    v_new = _stack_heads(vn_ref[...], head_lane)
    _sb_block(_dot_nt(q_stack, kn_ref[...]), lambda w: _dot(w, v_new), mask, _strict_upper(ts),
              carry_ref, acc_ref, ts)
    tri = _strict_upper(SB_TK)

    def cond(c):
        j, live = c
        return jnp.logical_and(j >= 0, live > 0)

    def body(c):
        j, _ = c
        r0 = pl.multiple_of(j * SB_TK, SB_TK)
        z = _dot(q_stack, ckt_ref[0, 0, :, pl.ds(r0, SB_TK)].astype(BF16))
        vt_stack = _stack_heads_t(cvt_ref[0, 0, :, pl.ds(r0, SB_TK)].astype(BF16), SB_TK)
        _sb_block(z, lambda w: _dot_nt(w, vt_stack), None, tri, carry_ref, acc_ref, ts)
        return j - 1, _sb_live(carry_ref)

    lax.while_loop(cond, body, (past // SB_TK - 1, _sb_live(carry_ref)))
    o_ref[...] = acc_ref[...].astype(o_ref.dtype)


def _sb_sample(q16, kn16, vn16, cache_kt, cache_vt, *, layer, nb, ts):
    past = cache_kt.shape[3]
    assert past % SB_TK == 0
    row = pl.BlockSpec((ts, W_ATT), lambda b: (b, 0))
    cache = pl.BlockSpec((1, 1, W_ATT, past), lambda b: (layer, b, 0, 0))
    return pl.pallas_call(
        functools.partial(_sb_sample_kernel, ts=ts, past=past),
        grid=(nb,),
        in_specs=[row, row, row, cache, cache],
        out_specs=row,
        out_shape=jax.ShapeDtypeStruct((nb * ts, W_ATT), BF16),
        scratch_shapes=[pltpu.VMEM((N_HEADS * ts, SB_TK), F32), pltpu.VMEM((ts, W_ATT), F32)],
        compiler_params=_cparams(1),
        name="sb_sample",
    )(q16, kn16, vn16, cache_kt, cache_vt)


def _rel_bias(table_ref, h, dist):
    idx = jnp.clip(dist, -REL_CLIP, REL_CLIP) + REL_CLIP

    def body(r, b):
        return jnp.where(idx == r, table_ref[h, r], b)

    return lax.fori_loop(0, N_REL, body, jnp.zeros(dist.shape, F32), unroll=8)


def _row_reduce(x, combine, reduce):
    width = x.shape[1]
    acc = x[:, :128]
    for c in range(128, width, 128):
        acc = combine(acc, x[:, c:c + 128])
    return reduce(acc, axis=-1, keepdims=True)


def _toeplitz_bias(table_ref, h, rows, back, width):
    lane = _iota((8, width), 1)
    by_diag = _rel_bias(table_ref, h, back + rows - lane) * LOG2E
    tiled = jnp.concatenate([by_diag] * (rows // 8), axis=0)
    return pltpu.roll(tiled, width - rows, 1, stride=1, stride_axis=0)


BAND_TQ = 128
BAND_BACK = N_PREV_CHUNKS * CHUNK
BAND_WIN = BAND_BACK + BAND_TQ


def _band_prompt_kernel(table_ref, q_ref, *rest, tq):
    n_kb = (len(rest) - 2) // 2
    k_refs, v_refs, (o_ref, bias_ref) = rest[:n_kb], rest[n_kb:2 * n_kb], rest[2 * n_kb:]
    i = pl.program_id(0)
    rows = _iota((BAND_TQ, BAND_WIN), 0)
    cols = _iota((BAND_TQ, BAND_WIN), 1)

    @pl.when(i == 0)
    def _():
        q_chunk = rows // CHUNK + N_PREV_CHUNKS
        k_chunk = cols // CHUNK
        in_band = jnp.logical_and(k_chunk >= q_chunk - N_PREV_CHUNKS, k_chunk <= q_chunk)
        for h in range(N_HEADS):
            bias = _toeplitz_bias(table_ref, h, BAND_TQ, BAND_BACK, BAND_WIN + BAND_TQ)
            bias_ref[h] = jnp.where(in_band, bias[:, :BAND_WIN], NEG_BIG)

    q = q_ref[...]
    head_lane = _head_masks(W_ATT)
    k_t = jnp.concatenate([r[...] for r in k_refs], axis=1)
    vcat = jnp.concatenate([r[...] for r in v_refs], axis=0)
    parts = [p * BAND_TQ for p in range(tq // BAND_TQ)]
    starts = [r0 + (n_kb - 1) * tq - BAND_BACK for r0 in parts]
    scores = []
    for p in range(len(parts)):
        qp = q[parts[p]:parts[p] + BAND_TQ]
        k_win = k_t[:, starts[p]:starts[p] + BAND_WIN]
        scores.append([_dot(jnp.where(head_lane[h], qp, jnp.zeros_like(qp)), k_win)
                       for h in range(N_HEADS)])
    for p in range(len(parts)):
        exists = (i - (n_kb - 1)) * tq + starts[p] + cols >= 0
        weights = []
        inv_den = []
        for h, s in enumerate(scores[p]):
            s = jnp.where(exists, s + bias_ref[h], NEG_BIG)
            e = jnp.exp2(s - _row_reduce(s, jnp.maximum, jnp.max))
            weights.append(e.astype(BF16))
            inv_den.append(1.0 / _row_reduce(e, jnp.add, jnp.sum))
        v_win = vcat[starts[p]:starts[p] + BAND_WIN]
        out = jnp.zeros((BAND_TQ, W_ATT), F32)
        for h in range(N_HEADS):
            out = jnp.where(head_lane[h], _dot(weights[h], v_win) * inv_den[h], out)
        o_ref[parts[p]:parts[p] + BAND_TQ, :] = out.astype(o_ref.dtype)


def _band_prompt(q16, kt16, v16, table, *, tq):
    t = q16.shape[0]
    assert t % tq == 0 and tq % BAND_TQ == 0
    n_kb = 1 + pl.cdiv(BAND_BACK, tq)
    blk = lambda back: pl.BlockSpec((tq, W_ATT), lambda i: (jnp.maximum(i - back, 0), 0))
    blk_t = lambda back: pl.BlockSpec((W_ATT, tq), lambda i: (0, jnp.maximum(i - back, 0)))
    oldest_first = list(reversed(range(n_kb)))
    return pl.pallas_call(
        functools.partial(_band_prompt_kernel, tq=tq),
        grid=(t // tq,),
        in_specs=[pl.BlockSpec(memory_space=pltpu.SMEM), blk(0)]
                 + [blk_t(b) for b in oldest_first] + [blk(b) for b in oldest_first],
        out_specs=blk(0),
        out_shape=jax.ShapeDtypeStruct((t, W_ATT), BF16),
        scratch_shapes=[pltpu.VMEM((N_HEADS, BAND_TQ, BAND_WIN), F32)],
        compiler_params=_cparams(1),
        name="band_prompt",
    )(table, q16, *([kt16] * n_kb), *([v16] * n_kb))


def _band_sample_kernel(table_ref, q_ref, kn_ref, vn_ref, ckt_ref, cvt_ref, o_ref, bias_ref, *, ts, past):
    lb = ckt_ref.shape[3]

    @pl.when(pl.program_id(0) == 0)
    def _():
        for h in range(N_HEADS):
            bias_ref[h] = _toeplitz_bias(table_ref, h, ts, lb, bias_ref.shape[2])

    q = q_ref[...]
    head_lane = _head_masks(W_ATT)
    kc_t = ckt_ref[0, 0].astype(BF16)
    vc_t = cvt_ref[0, 0].astype(BF16)
    kn = kn_ref[...]
    vn = vn_ref[...]
    q_pos_c = past + _iota((ts, lb), 0)
    k_pos_c = past - lb + _iota((ts, lb), 1)
    q_pos_n = past + _iota((ts, ts), 0)
    k_pos_n = past + _iota((ts, ts), 1)

    def allowed(q_pos, k_pos):
        qc, kc_ = q_pos // CHUNK, k_pos // CHUNK
        return jnp.logical_and(kc_ >= qc - N_PREV_CHUNKS, kc_ <= qc)

    ok_c = allowed(q_pos_c, k_pos_c)
    ok_n = allowed(q_pos_n, k_pos_n)
    out = jnp.zeros((ts, W_ATT), F32)
    for h in range(N_HEADS):
        qh = jnp.where(head_lane[h], q, jnp.zeros_like(q))
        bias = bias_ref[h]
        s_c = jnp.where(ok_c, _dot(qh, kc_t) + bias[:, :lb], NEG_BIG)
        s_n = jnp.where(ok_n, _dot_nt(qh, kn) + bias[:, lb:lb + ts], NEG_BIG)
        m = jnp.maximum(jnp.max(s_c, axis=-1, keepdims=True), jnp.max(s_n, axis=-1, keepdims=True))
        e_c = jnp.exp2(s_c - m)
        e_n = jnp.exp2(s_n - m)
        den = jnp.sum(e_c, axis=-1, keepdims=True) + jnp.sum(e_n, axis=-1, keepdims=True)
        pv = (_dot_nt(e_c.astype(BF16), vc_t) + _dot(e_n.astype(BF16), vn)) / den
        out = jnp.where(head_lane[h], pv, out)
    o_ref[...] = out.astype(o_ref.dtype)


def _band_sample(q16, kn16, vn16, cache_kt, cache_vt, table, *, layer, nb, ts, past):
    lb = cache_kt.shape[3]
    row = pl.BlockSpec((ts, W_ATT), lambda b: (b, 0))
    cache = pl.BlockSpec((1, 1, W_ATT, lb), lambda b: (layer, b, 0, 0))
    return pl.pallas_call(
        functools.partial(_band_sample_kernel, ts=ts, past=past),
        grid=(nb,),
        in_specs=[pl.BlockSpec(memory_space=pltpu.SMEM), row, row, row, cache, cache],
        out_specs=row,
        out_shape=jax.ShapeDtypeStruct((nb * ts, W_ATT), BF16),
        scratch_shapes=[pltpu.VMEM((N_HEADS, ts, pl.cdiv(lb + 2 * ts, 128) * 128), F32)],
        compiler_params=_cparams(1),
        name="band_sample",
    )(table, q16, kn16, vn16, cache_kt, cache_vt)


def _hgrn_spans(q, k, lf, v16, st_ref):
    tr = q.shape[0]
    heads = [slice(h * DK_C, (h + 1) * DK_C) for h in range(N_HEADS)]
    tri = jnp.where(_iota((tr, tr), 0) >= _iota((tr, tr), 1), 1.0, 0.0).astype(BF16)
    f1, f2, f3 = _split3(lf)
    g = (_dot(tri, f1) + _dot(tri, f2) + _dot(tri, f3)) * math.log2(math.e)

    g_end = g[tr - 1:tr, :]
    qt = (q * jnp.exp2(g)).astype(BF16)
    kd = (k * jnp.exp2(g_end - g)).astype(BF16)
    dec = jnp.exp2(g_end)
    states = [st_ref[h] for h in range(N_HEADS)]
    o_heads = [_dot_nt(qt[:, hs], st.astype(BF16)) for hs, st in zip(heads, states)]
    grown = [_dot_tn(v16[:, hs], kd[:, hs]) for hs in heads]
    for h, hs in enumerate(heads):
        st_ref[h] = states[h] * dec[:, hs] + grown[h]
    o = jnp.concatenate(o_heads, axis=1)

    sizes = []
    h = HGRN_DIAG
    while h < tr:
        sizes.append(h)
        h *= 2
    split = lambda x, h: x.reshape(tr // (2 * h), 2 * h, W_C)
    operands = []
    for h in sizes:
        g3, q3, k3, v3 = split(g, h), split(q, h), split(k, h), split(v16, h)
        c = g3[:, h - 1:h, :]
        qe = (q3[:, h:] * jnp.exp2(g3[:, h:] - c)).reshape(tr // 2, W_C).astype(BF16)
        ke = (k3[:, :h] * jnp.exp2(c - g3[:, :h])).reshape(tr // 2, W_C).astype(BF16)
        operands.append((qe, ke, v3[:, :h].reshape(tr // 2, W_C)))
    pair = [[_dot_nt(qe[:, hs], ke[:, hs]) for hs in heads] for qe, ke, _ in operands]
    span_of_row = _iota((tr // 2, tr // 2), 0)
    span_of_col = _iota((tr // 2, tr // 2), 1)
    for n, h in enumerate(sizes):
        if 2 * h < tr:
            same_span = (span_of_row // h) == (span_of_col // h)
            pair[n] = [jnp.where(same_span, p, 0.0) for p in pair[n]]
    added = [jnp.concatenate([_dot(p.astype(BF16), vl[:, hs]) for p, hs in zip(pair[n], heads)],
                             axis=1) for n, (_, _, vl) in enumerate(operands)]
    for n, h in enumerate(sizes):
        o3 = split(o, h)
        upper = o3[:, h:] + added[n].reshape(tr // (2 * h), h, W_C)
        o = jnp.concatenate([o3[:, :h], upper], axis=1).reshape(tr, W_C)
    return o, g


def _hgrn_diagonal(g2, q2, k2, v16):
    n = HGRN_DIAG
    lane = _iota((n, DK_C), 1)
    weights = [[jnp.zeros((n, DK_C), F32) for _ in range(N_HEADS)] for _ in range(2)]
    for s in range(2):
        g8, q8, k8 = g2[s * n:(s + 1) * n], q2[s * n:(s + 1) * n], k2[s * n:(s + 1) * n]
        for j in range(n):
            p = q8 * (k8[j:j + 1, :] * jnp.exp2(g8 - g8[j:j + 1, :]))
            for h in range(N_HEADS):
                a = jnp.sum(p[:, h * DK_C:(h + 1) * DK_C], axis=-1, keepdims=True)
                weights[s][h] = jnp.where(lane == s * n + j, a, weights[s][h])
    row = _iota((2 * n, DK_C), 0)
    col = _iota((2 * n, DK_C), 1)
    keep = jnp.logical_and(row >= col, row // n == col // n)
    out = []
    for h in range(N_HEADS):
        a = jnp.where(keep, jnp.concatenate([weights[0][h], weights[1][h]], axis=0), 0.0)
        out.append(_dot(a[:, :2 * n].astype(BF16), v16[:, h * DK_C:(h + 1) * DK_C]))
    return jnp.concatenate(out, axis=1)


def _hgrn_block(q, k, lf, v16, st_ref):
    o, g = _hgrn_spans(q, k, lf, v16, st_ref)
    pairs = [slice(r0, r0 + 2 * HGRN_DIAG) for r0 in range(0, q.shape[0], 2 * HGRN_DIAG)]
    return o + jnp.concatenate([_hgrn_diagonal(g[s], q[s], k[s], v16[s]) for s in pairs], axis=0)


def _hgrn_kernel(q_ref, k_ref, lf_ref, v_ref, s0_ref, o_ref, sout_ref, st_ref):
    r = pl.program_id(1)

    @pl.when(r == 0)
    def _():
        for h in range(N_HEADS):
            st_ref[h] = s0_ref[0, h].T

    o_ref[...] = _hgrn_block(q_ref[...], k_ref[...], lf_ref[...], v_ref[...], st_ref)

    @pl.when(r == pl.num_programs(1) - 1)
    def _():
        for h in range(N_HEADS):
            sout_ref[0, h] = st_ref[h].T


def _hgrn(qc, kc, lf, ic, s0, *, nb, t, tr):
    assert t % tr == 0 and tr % (2 * HGRN_DIAG) == 0 and tr & (tr - 1) == 0
    nr = t // tr
    row = pl.BlockSpec((tr, W_C), lambda b, r: (b * nr + r, 0))
    state = pl.BlockSpec((1, N_HEADS, DK_C, DK_C), lambda b, r: (b, 0, 0, 0))
    return pl.pallas_call(
        _hgrn_kernel,
        grid=(nb, nr),
        in_specs=[row, row, row, row, state],
        out_specs=[row, state],
        out_shape=[jax.ShapeDtypeStruct((nb * t, W_C), F32),
                   jax.ShapeDtypeStruct((nb, N_HEADS, DK_C, DK_C), F32)],
        scratch_shapes=[pltpu.VMEM((N_HEADS, DK_C, DK_C), F32)],
        compiler_params=_cparams(2),
        name="hgrn",
    )(qc, kc, lf, ic, s0)


def _merge_block(x, oa, ob, oc, gc, on, wo_ref, nf, wu_ref, wd_ref):
    parts = []
    for h in range(N_HEADS):
        och = oc[:, h * DK_C:(h + 1) * DK_C]
        ms = jnp.mean(och * och, axis=-1, keepdims=True)
        parts.append(och * lax.rsqrt(ms + EPS) * on)
    ocn = jnp.concatenate(parts, axis=-1) * (gc * (1.0 / (1.0 + jnp.exp(-gc))))
    mixed = jnp.concatenate([oa, ob, ocn.astype(BF16)], axis=-1)
    h_res = x + _dot(mixed, wo_ref[0])
    ms = jnp.mean(h_res * h_res, axis=-1, keepdims=True)
    hn = (h_res * lax.rsqrt(ms + EPS) * nf).astype(BF16)
    y = h_res
    for c in range(0, wu_ref.shape[2], FFN_CHUNK):
        u = jnp.maximum(_dot(hn, wu_ref[0, :, c:c + FFN_CHUNK]), 0.0)
        y = y + _dot((u * u).astype(BF16), wd_ref[0, c:c + FFN_CHUNK, :])
    return y


def _merge_kernel(x_ref, oa_ref, ob_ref, oc_ref, gc_ref, on_ref, wo_ref, nf_ref, wu_ref, wd_ref,
                  y_ref):
    y_ref[...] = _merge_block(x_ref[...], oa_ref[...], ob_ref[...], oc_ref[...], gc_ref[...],
                              on_ref[...], wo_ref, nf_ref[...], wu_ref, wd_ref)


def _merge(x2d, oa, ob, oc, gc, onorm, wo16, nf, wu16, wd16, *, layer, tm):
    m, d = x2d.shape
    assert m % tm == 0
    row = lambda w: pl.BlockSpec((tm, w), lambda i: (i, 0))
    full = _resident
    return pl.pallas_call(
        _merge_kernel,
        grid=(m // tm,),
        in_specs=[row(d), row(W_ATT), row(W_ATT), row(W_C), row(W_C),
                  full(onorm), _resident_layer(wo16, layer), full(nf),
                  _resident_layer(wu16, layer), _resident_layer(wd16, layer)],
        out_specs=row(d),
        out_shape=jax.ShapeDtypeStruct((m, d), F32),
        compiler_params=_cparams(1),
        name="merge_ffn",
    )(x2d, oa, ob, oc, gc, onorm, wo16, nf, wu16, wd16)


def kernel(x_prompt, x_sample, cache_a_k, cache_a_v, cache_b_k, cache_b_v, state_c, norm_mix, w_in, qnorm_a, knorm_a, qnorm_b, knorm_b, rel_bias_b, lower_bounds, onorm_c, w_o, norm_ffn, w_up, w_down):
    depth = w_in.shape[0]
    bp, tp, d = x_prompt.shape
    bs, ts, _ = x_sample.shape
    past = cache_a_k.shape[2]
    band_rows_p = min(N_PREV_CHUNKS * CHUNK, tp)
    assert bp == 1

    xp = x_prompt.reshape(bp * tp, d)
    xs = x_sample.reshape(bs * ts, d)
    time_minor = lambda c: jnp.transpose(c, (0, 1, 3, 4, 2)).reshape(depth, bs, W_ATT, c.shape[2])
    cak, cav, cbk, cbv = (time_minor(c) for c in (cache_a_k, cache_a_v, cache_b_k, cache_b_v))
    heads_of = lambda a_t, n: jnp.transpose(
        a_t.reshape(N_HEADS, D_HEAD, n, a_t.shape[1] // n), (2, 3, 0, 1))
    lbounds = lower_bounds.astype(F32)
    w16 = w_in.astype(BF16)
    zero_state = jnp.zeros((bp, N_HEADS, DK_C, DK_C), F32)

    tm_p = min(512, tp)
    tm_merge = min(1024, tp)
    tq_band = min(1024, tp)
    tr_p = min(256, tp)

    outs_p = [[] for _ in range(5)]
    outs_s = [[] for _ in range(5)]
    carried = ()
    for l in range(depth):
        nm = norm_mix[l].reshape(1, d)
        nf = norm_ffn[l].reshape(1, d)
        tile_h = lambda g: jnp.tile(g.reshape(1, D_HEAD), (1, N_HEADS))
        qna, kna, qnb, knb = (tile_h(g[l]) for g in (qnorm_a, knorm_a, qnorm_b, knorm_b))
        onorm = onorm_c[l].reshape(1, DK_C)
        table = rel_bias_b[l].astype(F32)

        (qa, ka_all, va_all, ka16_t, va16, qb, kb_t, vb_t, kb16_t, vb16, qc, kc, lf, ic, gc,
         wo16, wu16, wd16) = _proj(
            xp, nm, w16, qna, kna, qnb, knb, lbounds, layer=l, tm=tm_p, time_minor=True,
            carried=carried, cast=(w_o, w_up, w_down))
        carried = (ka_all, va_all)
        oa = _sb_prompt(qa, ka16_t, va16, tq=min(SB_TQ, tp))
        ob = _band_prompt(qb, kb16_t, vb16, table, tq=tq_band)
        oc, s_p = _hgrn(qc, kc, lf, ic, zero_state, nb=bp, t=tp, tr=tr_p)
        xp = _merge(xp, oa, ob, oc, gc, onorm, wo16, nf, wu16, wd16, layer=0, tm=tm_merge)
        outs_p[2].append(heads_of(kb_t, bp)[:, tp - band_rows_p:])
        outs_p[3].append(heads_of(vb_t, bp)[:, tp - band_rows_p:])
        outs_p[4].append(s_p)

        head = lambda a, n: a.reshape(n, -1, N_HEADS, D_HEAD)
        (qa, ka, va, ka16, va16, qb, kb, vb, kb16, vb16, qc, kc, lf, ic, gc) = _proj(
            xs, nm, w16, qna, kna, qnb, knb, lbounds, layer=l, tm=bs * ts, time_minor=False)
        oa = _sb_sample(qa, ka16, va16, cak, cav, layer=l, nb=bs, ts=ts)
        ob = _band_sample(qb, kb16, vb16, cbk, cbv, table, layer=l, nb=bs, ts=ts, past=past)
        oc, s_s = _hgrn(qc, kc, lf, ic, state_c[l].astype(F32), nb=bs, t=ts, tr=ts)
        xs = _merge(xs, oa, ob, oc, gc, onorm, wo16, nf, wu16, wd16, layer=0, tm=bs * ts)
        outs_s[0].append(head(ka, bs))
        outs_s[1].append(head(va, bs))
        outs_s[2].append(head(kb, bs))
        outs_s[3].append(head(vb, bs))
        outs_s[4].append(s_s)

    stack = lambda xs_: jnp.stack(xs_)
    all_heads_of = lambda a: jnp.transpose(
        a.reshape(depth, N_HEADS, D_HEAD, bp, tp), (0, 3, 4, 1, 2))
    ka_all, va_all = carried
    return (xp.reshape(bp, tp, d), xs.reshape(bs, ts, d),
            all_heads_of(ka_all), all_heads_of(va_all),
            stack(outs_p[2]), stack(outs_p[3]), stack(outs_p[4]),
            stack(outs_s[0]), stack(outs_s[1]), stack(outs_s[2]), stack(outs_s[3]), stack(outs_s[4]))
```

```python
import functools
import math

import jax
import jax.numpy as jnp
from jax import lax
from jax.experimental import pallas as pl
from jax.experimental.pallas import tpu as pltpu

F32 = jnp.float32
BF16 = jnp.bfloat16

D_HEAD = 64
N_HEADS = 4
W_ATT = N_HEADS * D_HEAD
DK_C = 128
W_C = N_HEADS * DK_C
CHUNK = 64
N_PREV_CHUNKS = 8
REL_CLIP = 128
N_REL = 2 * REL_CLIP + 1
EPS = 1e-6
NEG_BIG = -1e30
LB_FLOOR = 1e-30
LOG2E = math.log2(math.e)
SB_UNDERFLOW = 88.0
PROJ_PART = 256
FFN_CHUNK = 1024
SB_TK = 128
SB_TQ = 256
HGRN_DIAG = 8
VMEM_LIMIT = 60 * 1024 * 1024


def _cparams(n_axes):
    return pltpu.CompilerParams(dimension_semantics=("arbitrary",) * n_axes,
                                vmem_limit_bytes=VMEM_LIMIT)


def _resident(a):
    zeros = (0,) * a.ndim
    return pl.BlockSpec(a.shape, lambda *_: zeros, pipeline_mode=pl.Buffered(1))


def _resident_layer(a, layer):
    index = (layer,) + (0,) * (a.ndim - 1)
    return pl.BlockSpec((1,) + a.shape[1:], lambda *_: index, pipeline_mode=pl.Buffered(1))


def _split3(x):
    h1 = x.astype(BF16)
    r1 = x - h1.astype(F32)
    h2 = r1.astype(BF16)
    h3 = (r1 - h2.astype(F32)).astype(BF16)
    return h1, h2, h3


def _dot(a, b):
    return jnp.dot(a, b, preferred_element_type=F32)


def _dot_nt(a, b):
    return lax.dot_general(a, b, (((1,), (1,)), ((), ())), preferred_element_type=F32)


def _dot_tn(a, b):
    return lax.dot_general(a, b, (((0,), (0,)), ((), ())), preferred_element_type=F32)


def _iota(shape, dim):
    return lax.broadcasted_iota(jnp.int32, shape, dim)


def _proj_kernel(x_ref, nm_ref, w_ref, qna_ref, kna_ref, qnb_ref, knb_ref, lb_ref, *rest,
                 layer, time_minor, n_carried, n_cast):
    carried_refs = rest[:n_carried]
    cast_in_refs = rest[n_carried:n_carried + n_cast]
    (qa_ref, ka_ref, va_ref, ka16_ref, va16_ref, qb_ref, kb_ref, vb_ref, kb16_ref, vb16_ref,
     qc_ref, kc_ref, lf_ref, ic_ref, gc_ref) = rest[n_carried + n_cast:n_carried + n_cast + 15]
    cast_out_refs = rest[n_carried + n_cast + 15:]
    for f32_ref, b16_ref in zip(cast_in_refs, cast_out_refs):
        b16_ref[...] = f32_ref[...].astype(BF16)
    if time_minor:
        for earlier_ref, all_ref in zip(carried_refs, (ka_ref, va_ref)):
            all_ref[0:layer] = earlier_ref[...]
        ka_ref, va_ref = ka_ref.at[layer], va_ref.at[layer]
    x = x_ref[...]
    ms = jnp.mean(x * x, axis=-1, keepdims=True)
    xn = (x * lax.rsqrt(ms + EPS) * nm_ref[...]).astype(BF16)

    tm = x.shape[0]
    part = PROJ_PART if tm % PROJ_PART == 0 else tm
    projected = [_dot(xn[r0:r0 + part], w_ref[0]) for r0 in range(0, tm, part)]

    same_head = (_iota((W_ATT, W_ATT), 0) // D_HEAD) == (_iota((W_ATT, W_ATT), 1) // D_HEAD)
    seg_mean = jnp.where(same_head, 1.0 / D_HEAD, 0.0).astype(BF16)
    inv_sqrt_d = 1.0 / math.sqrt(D_HEAD)

    lbr = lb_ref[...]
    e = jnp.exp(lbr - jnp.max(lbr, axis=0, keepdims=True))
    sm = e / jnp.sum(e, axis=0, keepdims=True)
    lb = jnp.sum(sm[0:layer + 1], axis=0, keepdims=True) - sm[0:1]
    log_lb = jnp.log(jnp.maximum(lb, LB_FLOOR))
    log_1m_lb = jnp.log1p(-lb)

    def head_norm(p, g_ref):
        m = _dot((p * p).astype(BF16), seg_mean)
        return p * lax.rsqrt(m + EPS) * g_ref[...]

    for n, p in enumerate(projected):
        rows = slice(n * part, (n + 1) * part)

        def put(x, f32_ref, b16_ref, b16_time_minor):
            x_t = x.T if time_minor else None
            if time_minor:
                f32_ref[:, rows] = x_t
            else:
                f32_ref[rows, :] = x
            if time_minor and b16_time_minor:
                b16_ref[:, rows] = x_t.astype(BF16)
            else:
                b16_ref[rows, :] = x.astype(BF16)

        o = 0
        qa_ref[rows, :] = (head_norm(p[:, o:o + W_ATT], qna_ref) * inv_sqrt_d).astype(BF16)
        o += W_ATT
        put(head_norm(p[:, o:o + W_ATT], kna_ref), ka_ref, ka16_ref, True)
        o += W_ATT
        put(p[:, o:o + W_ATT], va_ref, va16_ref, False)
        o += W_ATT
        qb_ref[rows, :] = (head_norm(p[:, o:o + W_ATT], qnb_ref)
                           * (inv_sqrt_d * LOG2E)).astype(BF16)
        o += W_ATT
        put(head_norm(p[:, o:o + W_ATT], knb_ref), kb_ref, kb16_ref, True)
        o += W_ATT
        put(p[:, o:o + W_ATT], vb_ref, vb16_ref, False)
        o += W_ATT
        qc_ref[rows, :] = p[:, o:o + W_C] * (DK_C ** -0.5)
        o += W_C
        f_raw = p[:, o:o + W_C]
        o += W_C
        log_sig = jnp.minimum(f_raw, 0.0) - jnp.log(1.0 + jnp.exp(-jnp.abs(f_raw)))
        b = log_1m_lb + log_sig
        lf_ref[rows, :] = jnp.maximum(log_lb, b) + jnp.log(1.0 + jnp.exp(-jnp.abs(log_lb - b)))
        kc_ref[rows, :] = (1.0 - lb) * (1.0 / (1.0 + jnp.exp(f_raw)))
        ic_ref[rows, :] = p[:, o:o + W_C].astype(BF16)
        o += W_C
        gc_ref[rows, :] = p[:, o:o + W_C]


def _proj(x2d, nm, w16, qna, kna, qnb, knb, lower_bounds, *, layer, w_layer, tm, time_minor,
          carried=(), cast=()):
    m, d = x2d.shape
    steps = m // tm
    assert m % tm == 0 and len(carried) == (2 if time_minor and layer > 0 else 0)
    assert all(a.shape[1] % (16 * steps) == 0 for a, _ in cast)
    slab_in = lambda a, n: pl.BlockSpec((1, a.shape[1] // steps, a.shape[2]), lambda i: (n, i, 0))
    slab_out = lambda a: (pl.BlockSpec((1, a.shape[1] // steps, a.shape[2]), lambda i: (0, i, 0)),
                          jax.ShapeDtypeStruct((1,) + a.shape[1:], BF16))
    full = _resident
    row = lambda w, dt: (pl.BlockSpec((tm, w), lambda i: (i, 0)), jax.ShapeDtypeStruct((m, w), dt))
    col = lambda w, dt: (pl.BlockSpec((w, tm), lambda i: (0, i)), jax.ShapeDtypeStruct((w, m), dt))
    layers = lambda n, w: pl.BlockSpec((n, w, tm), lambda i: (0, 0, i))
    layered = lambda w, dt: (layers(layer + 1, w), jax.ShapeDtypeStruct((layer + 1, w, m), dt))
    kv = col if time_minor else row
    kv_a = layered if time_minor else row
    att_a = [row(W_ATT, BF16), kv_a(W_ATT, F32), kv_a(W_ATT, F32), kv(W_ATT, BF16), row(W_ATT, BF16)]
    att_b = [row(W_ATT, BF16), kv(W_ATT, F32), kv(W_ATT, F32), kv(W_ATT, BF16), row(W_ATT, BF16)]
    outs = att_a + att_b + [row(W_C, F32)] * 3 + [row(W_C, BF16), row(W_C, F32)]
    outs += [slab_out(a) for a, _ in cast]
    return pl.pallas_call(
        functools.partial(_proj_kernel, layer=layer, time_minor=time_minor,
                          n_carried=len(carried), n_cast=len(cast)),
        grid=(steps,),
        in_specs=[row(d, F32)[0], full(nm), _resident_layer(w16, w_layer), full(qna), full(kna),
                  full(qnb), full(knb), full(lower_bounds)]
                 + [layers(layer, W_ATT)] * len(carried) + [slab_in(a, n) for a, n in cast],
        out_specs=[spec for spec, _ in outs],
        out_shape=[shape for _, shape in outs],
        compiler_params=_cparams(1),
        name="proj",
    )(x2d, nm, w16, qna, kna, qnb, knb, lower_bounds, *carried, *(a for a, _ in cast))


def _head_masks(width):
    lane_head = _iota((1, width), 1) // D_HEAD
    return [lane_head == h for h in range(N_HEADS)]


def _stack_heads(x, head_lane):
    return jnp.concatenate([jnp.where(m, x, jnp.zeros_like(x)) for m in head_lane], axis=0)


def _stack_heads_t(x_t, n):
    row_head = _iota((W_ATT, 1), 0) // D_HEAD
    return jnp.concatenate(
        [jnp.where(row_head == h, x_t, jnp.zeros_like(x_t)) for h in range(N_HEADS)], axis=1)


def _stack_groups(x, head_lane, gq):
    return jnp.concatenate(
        [_stack_heads(x[g:g + gq], head_lane) for g in range(0, x.shape[0], gq)], axis=0)


def _sb_block(z, weigh_values, mask, tri, carry_ref, acc_ref, gq):
    tk = z.shape[1]
    t = jnp.log(1.0 + jnp.exp2(jnp.abs(z) * -math.log2(math.e)))
    log_1m = jnp.minimum(-z, 0.0) - t
    log_beta = log_1m + z
    if mask is not None:
        log_1m = jnp.where(mask, log_1m, 0.0)
    hi = log_1m.astype(BF16)
    lo = (log_1m - hi.astype(F32)).astype(BF16)
    later = _dot(jnp.concatenate([hi, lo], axis=1), tri)
    carry = carry_ref[...]
    w = jnp.exp(log_beta + later + carry[:, :tk])
    if mask is not None:
        w = jnp.where(mask, w, 0.0)
    w = w.astype(BF16)
    blocks = [w[n * gq:(n + 1) * gq] for n in range(z.shape[0] // gq)]
    w_heads = jnp.concatenate(
        [jnp.concatenate(blocks[g:g + N_HEADS], axis=1) for g in range(0, len(blocks), N_HEADS)],
        axis=0)
    acc_ref[...] += weigh_values(w_heads)
    carry_ref[...] = carry + jnp.sum(log_1m, axis=-1, keepdims=True)


def _sb_live(carry_ref):
    return (jnp.max(carry_ref[...]) > -SB_UNDERFLOW).astype(jnp.int32)


def _strict_upper(n):
    return jnp.where(_iota((2 * n, n), 0) % n > _iota((2 * n, n), 1), 1.0, 0.0).astype(BF16)


def _sb_prompt_kernel(q_ref, kt_ref, v_ref, o_ref, carry_ref, acc_ref, *, tq):
    i = pl.program_id(0)
    head_lane = _head_masks(W_ATT)
    gq = SB_TK
    n_groups = tq // gq
    q_stack = _stack_groups(q_ref[...], head_lane, gq)
    tri = _strict_upper(SB_TK)
    carry_ref[...] = jnp.zeros_like(carry_ref)
    acc_ref[...] = jnp.zeros_like(acc_ref)

    def add_block(j, first_group, mask):
        r0 = pl.multiple_of(j * SB_TK, SB_TK)
        s0 = first_group * N_HEADS * gq
        z = _dot(q_stack[s0:], kt_ref[:, pl.ds(r0, SB_TK)])
        v_stack = _stack_heads(v_ref[pl.ds(r0, SB_TK), :], head_lane)
        _sb_block(z, lambda w: _dot(w, v_stack), mask, tri,
                  carry_ref.at[pl.ds(s0, z.shape[0])],
                  acc_ref.at[pl.ds(first_group * gq, (n_groups - first_group) * gq)], gq)

    for g in reversed(range(n_groups)):
        shape = ((n_groups - g) * N_HEADS * gq, SB_TK)
        row = _iota(shape, 0)
        add_block(i * n_groups + g, g, jnp.logical_or(_iota(shape, 1) < row % gq, row >= N_HEADS * gq))
    add_block(jnp.maximum(i * n_groups - 1, 0), 0,
              jnp.broadcast_to(i > 0, (n_groups * N_HEADS * gq, SB_TK)))

    def cond(c):
        j, live = c
        return jnp.logical_and(j >= 0, live > 0)

    def body(c):
        j, _ = c
        add_block(j, 0, None)
        return j - 1, _sb_live(carry_ref)

    lax.while_loop(cond, body, (i * n_groups - 2, _sb_live(carry_ref)))
    o_ref[...] = acc_ref[...].astype(o_ref.dtype)


def _sb_prompt(q16, kt16, v16, *, tq):
    t = q16.shape[0]
    assert t % tq == 0 and tq % SB_TK == 0
    return pl.pallas_call(
        functools.partial(_sb_prompt_kernel, tq=tq),
        grid=(t // tq,),
        in_specs=[pl.BlockSpec((tq, W_ATT), lambda i: (i, 0)),
                  _resident(kt16), _resident(v16)],
        out_specs=pl.BlockSpec((tq, W_ATT), lambda i: (i, 0)),
        out_shape=jax.ShapeDtypeStruct((t, W_ATT), BF16),
        scratch_shapes=[pltpu.VMEM((N_HEADS * tq, SB_TK), F32), pltpu.VMEM((tq, W_ATT), F32)],
        compiler_params=_cparams(1),
        name="sb_prompt",
    )(q16, kt16, v16)


def _sb_sample_kernel(q_ref, kn_ref, vn_ref, ckt_ref, cvt_ref, o_ref, carry_ref, acc_ref, *, ts, past):
    head_lane = _head_masks(W_ATT)
    q_stack = _stack_heads(q_ref[...], head_lane)
    carry_ref[...] = jnp.zeros_like(carry_ref)
    acc_ref[...] = jnp.zeros_like(acc_ref)
    shape = (N_HEADS * ts, ts)
    mask = _iota(shape, 1) < _iota(shape, 0) % ts
    v_new = _stack_heads(vn_ref[...], head_lane)
    _sb_block(_dot_nt(q_stack, kn_ref[...]), lambda w: _dot(w, v_new), mask, _strict_upper(ts),
              carry_ref, acc_ref, ts)
    tri = _strict_upper(SB_TK)

    def cond(c):
        j, live = c
        return jnp.logical_and(j >= 0, live > 0)

    def body(c):
        j, _ = c
        r0 = pl.multiple_of(j * SB_TK, SB_TK)
        z = _dot(q_stack, ckt_ref[0, 0, :, pl.ds(r0, SB_TK)].astype(BF16))
        vt_stack = _stack_heads_t(cvt_ref[0, 0, :, pl.ds(r0, SB_TK)].astype(BF16), SB_TK)
        _sb_block(z, lambda w: _dot_nt(w, vt_stack), None, tri, carry_ref, acc_ref, ts)
        return j - 1, _sb_live(carry_ref)

    lax.while_loop(cond, body, (past // SB_TK - 1, _sb_live(carry_ref)))
    o_ref[...] = acc_ref[...].astype(o_ref.dtype)


def _sb_sample(q16, kn16, vn16, cache_kt, cache_vt, *, layer, nb, ts):
    past = cache_kt.shape[3]
    assert past % SB_TK == 0
    row = pl.BlockSpec((ts, W_ATT), lambda b: (b, 0))
    cache = pl.BlockSpec((1, 1, W_ATT, past), lambda b: (layer, b, 0, 0))
    return pl.pallas_call(
        functools.partial(_sb_sample_kernel, ts=ts, past=past),
        grid=(nb,),
        in_specs=[row, row, row, cache, cache],
        out_specs=row,
        out_shape=jax.ShapeDtypeStruct((nb * ts, W_ATT), BF16),
        scratch_shapes=[pltpu.VMEM((N_HEADS * ts, SB_TK), F32), pltpu.VMEM((ts, W_ATT), F32)],
        compiler_params=_cparams(1),
        name="sb_sample",
    )(q16, kn16, vn16, cache_kt, cache_vt)


def _rel_bias(table_ref, h, dist):
    idx = jnp.clip(dist, -REL_CLIP, REL_CLIP) + REL_CLIP

    def body(r, b):
        return jnp.where(idx == r, table_ref[h, r], b)

    return lax.fori_loop(0, N_REL, body, jnp.zeros(dist.shape, F32), unroll=8)


def _row_reduce(x, combine, reduce):
    width = x.shape[1]
    acc = x[:, :128]
    for c in range(128, width, 128):
        acc = combine(acc, x[:, c:c + 128])
    return reduce(acc, axis=-1, keepdims=True)


def _toeplitz_bias(table_ref, h, rows, back, width):
    lane = _iota((8, width), 1)
    by_diag = _rel_bias(table_ref, h, back + rows - lane) * LOG2E
    tiled = jnp.concatenate([by_diag] * (rows // 8), axis=0)
    return pltpu.roll(tiled, width - rows, 1, stride=1, stride_axis=0)


BAND_TQ = 128
BAND_BACK = N_PREV_CHUNKS * CHUNK
BAND_WIN = BAND_BACK + BAND_TQ


def _band_prompt_kernel(table_ref, q_ref, *rest, tq):
    n_kb = (len(rest) - 2) // 2
    k_refs, v_refs, (o_ref, bias_ref) = rest[:n_kb], rest[n_kb:2 * n_kb], rest[2 * n_kb:]
    i = pl.program_id(0)
    rows = _iota((BAND_TQ, BAND_WIN), 0)
    cols = _iota((BAND_TQ, BAND_WIN), 1)

    @pl.when(i == 0)
    def _():
        q_chunk = rows // CHUNK + N_PREV_CHUNKS
        k_chunk = cols // CHUNK
        in_band = jnp.logical_and(k_chunk >= q_chunk - N_PREV_CHUNKS, k_chunk <= q_chunk)
        for h in range(N_HEADS):
            bias = _toeplitz_bias(table_ref, h, BAND_TQ, BAND_BACK, BAND_WIN + BAND_TQ)
            bias_ref[h] = jnp.where(in_band, bias[:, :BAND_WIN], NEG_BIG)

    q = q_ref[...]
    head_lane = _head_masks(W_ATT)
    k_t = jnp.concatenate([r[...] for r in k_refs], axis=1)
    vcat = jnp.concatenate([r[...] for r in v_refs], axis=0)
    parts = [p * BAND_TQ for p in range(tq // BAND_TQ)]
    starts = [r0 + (n_kb - 1) * tq - BAND_BACK for r0 in parts]
    scores = []
    for p in range(len(parts)):
        qp = q[parts[p]:parts[p] + BAND_TQ]
        k_win = k_t[:, starts[p]:starts[p] + BAND_WIN]
        scores.append([_dot(jnp.where(head_lane[h], qp, jnp.zeros_like(qp)), k_win)
                       for h in range(N_HEADS)])
    for p in range(len(parts)):
        exists = (i - (n_kb - 1)) * tq + starts[p] + cols >= 0
        weights = []
        inv_den = []
        for h, s in enumerate(scores[p]):
            s = jnp.where(exists, s + bias_ref[h], NEG_BIG)
            e = jnp.exp2(s - _row_reduce(s, jnp.maximum, jnp.max))
            weights.append(e.astype(BF16))
            inv_den.append(1.0 / _row_reduce(e, jnp.add, jnp.sum))
        v_win = vcat[starts[p]:starts[p] + BAND_WIN]
        out = jnp.zeros((BAND_TQ, W_ATT), F32)
        for h in range(N_HEADS):
            out = jnp.where(head_lane[h], _dot(weights[h], v_win) * inv_den[h], out)
        o_ref[parts[p]:parts[p] + BAND_TQ, :] = out.astype(o_ref.dtype)


def _band_prompt(q16, kt16, v16, table, *, tq):
    t = q16.shape[0]
    assert t % tq == 0 and tq % BAND_TQ == 0
    n_kb = 1 + pl.cdiv(BAND_BACK, tq)
    blk = lambda back: pl.BlockSpec((tq, W_ATT), lambda i: (jnp.maximum(i - back, 0), 0))
    blk_t = lambda back: pl.BlockSpec((W_ATT, tq), lambda i: (0, jnp.maximum(i - back, 0)))
    oldest_first = list(reversed(range(n_kb)))
    return pl.pallas_call(
        functools.partial(_band_prompt_kernel, tq=tq),
        grid=(t // tq,),
        in_specs=[pl.BlockSpec(memory_space=pltpu.SMEM), blk(0)]
                 + [blk_t(b) for b in oldest_first] + [blk(b) for b in oldest_first],
        out_specs=blk(0),
        out_shape=jax.ShapeDtypeStruct((t, W_ATT), BF16),
        scratch_shapes=[pltpu.VMEM((N_HEADS, BAND_TQ, BAND_WIN), F32)],
        compiler_params=_cparams(1),
        name="band_prompt",
    )(table, q16, *([kt16] * n_kb), *([v16] * n_kb))


def _band_sample_kernel(table_ref, q_ref, kn_ref, vn_ref, ckt_ref, cvt_ref, o_ref, bias_ref, *, ts, past):
    lb = ckt_ref.shape[3]

    @pl.when(pl.program_id(0) == 0)
    def _():
        for h in range(N_HEADS):
            bias_ref[h] = _toeplitz_bias(table_ref, h, ts, lb, bias_ref.shape[2])

    q = q_ref[...]
    head_lane = _head_masks(W_ATT)
    kc_t = ckt_ref[0, 0].astype(BF16)
    vc_t = cvt_ref[0, 0].astype(BF16)
    kn = kn_ref[...]
    vn = vn_ref[...]
    q_pos_c = past + _iota((ts, lb), 0)
    k_pos_c = past - lb + _iota((ts, lb), 1)
    q_pos_n = past + _iota((ts, ts), 0)
    k_pos_n = past + _iota((ts, ts), 1)

    def allowed(q_pos, k_pos):
        qc, kc_ = q_pos // CHUNK, k_pos // CHUNK
        return jnp.logical_and(kc_ >= qc - N_PREV_CHUNKS, kc_ <= qc)

    ok_c = allowed(q_pos_c, k_pos_c)
    ok_n = allowed(q_pos_n, k_pos_n)
    out = jnp.zeros((ts, W_ATT), F32)
    for h in range(N_HEADS):
        qh = jnp.where(head_lane[h], q, jnp.zeros_like(q))
        bias = bias_ref[h]
        s_c = jnp.where(ok_c, _dot(qh, kc_t) + bias[:, :lb], NEG_BIG)
        s_n = jnp.where(ok_n, _dot_nt(qh, kn) + bias[:, lb:lb + ts], NEG_BIG)
        m = jnp.maximum(jnp.max(s_c, axis=-1, keepdims=True), jnp.max(s_n, axis=-1, keepdims=True))
        e_c = jnp.exp2(s_c - m)
        e_n = jnp.exp2(s_n - m)
        den = jnp.sum(e_c, axis=-1, keepdims=True) + jnp.sum(e_n, axis=-1, keepdims=True)
        pv = (_dot_nt(e_c.astype(BF16), vc_t) + _dot(e_n.astype(BF16), vn)) / den
        out = jnp.where(head_lane[h], pv, out)
    o_ref[...] = out.astype(o_ref.dtype)


def _band_sample(q16, kn16, vn16, cache_kt, cache_vt, table, *, layer, nb, ts, past):
    lb = cache_kt.shape[3]
    row = pl.BlockSpec((ts, W_ATT), lambda b: (b, 0))
    cache = pl.BlockSpec((1, 1, W_ATT, lb), lambda b: (layer, b, 0, 0))
    return pl.pallas_call(
        functools.partial(_band_sample_kernel, ts=ts, past=past),
        grid=(nb,),
        in_specs=[pl.BlockSpec(memory_space=pltpu.SMEM), row, row, row, cache, cache],
        out_specs=row,
        out_shape=jax.ShapeDtypeStruct((nb * ts, W_ATT), BF16),
        scratch_shapes=[pltpu.VMEM((N_HEADS, ts, pl.cdiv(lb + 2 * ts, 128) * 128), F32)],
        compiler_params=_cparams(1),
        name="band_sample",
    )(table, q16, kn16, vn16, cache_kt, cache_vt)


def _hgrn_spans(q, k, lf, v16, st_ref):
    tr = q.shape[0]
    heads = [slice(h * DK_C, (h + 1) * DK_C) for h in range(N_HEADS)]
    tri = jnp.where(_iota((tr, tr), 0) >= _iota((tr, tr), 1), 1.0, 0.0).astype(BF16)
    f1, f2, f3 = _split3(lf)
    g = (_dot(tri, f1) + _dot(tri, f2) + _dot(tri, f3)) * math.log2(math.e)

    g_end = g[tr - 1:tr, :]
    qt = (q * jnp.exp2(g)).astype(BF16)
    kd = (k * jnp.exp2(g_end - g)).astype(BF16)
    dec = jnp.exp2(g_end)
    states = [st_ref[h] for h in range(N_HEADS)]
    o_heads = [_dot_nt(qt[:, hs], st.astype(BF16)) for hs, st in zip(heads, states)]
    grown = [_dot_tn(v16[:, hs], kd[:, hs]) for hs in heads]
    for h, hs in enumerate(heads):
        st_ref[h] = states[h] * dec[:, hs] + grown[h]
    o = jnp.concatenate(o_heads, axis=1)

    sizes = []
    h = HGRN_DIAG
    while h < tr:
        sizes.append(h)
        h *= 2
    split = lambda x, h: x.reshape(tr // (2 * h), 2 * h, W_C)
    operands = []
    for h in sizes:
        g3, q3, k3, v3 = split(g, h), split(q, h), split(k, h), split(v16, h)
        c = g3[:, h - 1:h, :]
        qe = (q3[:, h:] * jnp.exp2(g3[:, h:] - c)).reshape(tr // 2, W_C).astype(BF16)
        ke = (k3[:, :h] * jnp.exp2(c - g3[:, :h])).reshape(tr // 2, W_C).astype(BF16)
        operands.append((qe, ke, v3[:, :h].reshape(tr // 2, W_C)))
    pair = [[_dot_nt(qe[:, hs], ke[:, hs]) for hs in heads] for qe, ke, _ in operands]
    span_of_row = _iota((tr // 2, tr // 2), 0)
    span_of_col = _iota((tr // 2, tr // 2), 1)
    for n, h in enumerate(sizes):
        if 2 * h < tr:
            same_span = (span_of_row // h) == (span_of_col // h)
            pair[n] = [jnp.where(same_span, p, 0.0) for p in pair[n]]
    added = [jnp.concatenate([_dot(p.astype(BF16), vl[:, hs]) for p, hs in zip(pair[n], heads)],
                             axis=1) for n, (_, _, vl) in enumerate(operands)]
    for n, h in enumerate(sizes):
        o3 = split(o, h)
        upper = o3[:, h:] + added[n].reshape(tr // (2 * h), h, W_C)
        o = jnp.concatenate([o3[:, :h], upper], axis=1).reshape(tr, W_C)
    return o, g


def _hgrn_diagonal(g2, q2, k2, v16):
    n = HGRN_DIAG
    lane = _iota((n, DK_C), 1)
    weights = [[jnp.zeros((n, DK_C), F32) for _ in range(N_HEADS)] for _ in range(2)]
    for s in range(2):
        g8, q8, k8 = g2[s * n:(s + 1) * n], q2[s * n:(s + 1) * n], k2[s * n:(s + 1) * n]
        for j in range(n):
            p = q8 * (k8[j:j + 1, :] * jnp.exp2(g8 - g8[j:j + 1, :]))
            for h in range(N_HEADS):
                a = jnp.sum(p[:, h * DK_C:(h + 1) * DK_C], axis=-1, keepdims=True)
                weights[s][h] = jnp.where(lane == s * n + j, a, weights[s][h])
    row = _iota((2 * n, DK_C), 0)
    col = _iota((2 * n, DK_C), 1)
    keep = jnp.logical_and(row >= col, row // n == col // n)
    out = []
    for h in range(N_HEADS):
        a = jnp.where(keep, jnp.concatenate([weights[0][h], weights[1][h]], axis=0), 0.0)
        out.append(_dot(a[:, :2 * n].astype(BF16), v16[:, h * DK_C:(h + 1) * DK_C]))
    return jnp.concatenate(out, axis=1)


def _hgrn_block(q, k, lf, v16, st_ref):
    o, g = _hgrn_spans(q, k, lf, v16, st_ref)
    pairs = [slice(r0, r0 + 2 * HGRN_DIAG) for r0 in range(0, q.shape[0], 2 * HGRN_DIAG)]
    return o + jnp.concatenate([_hgrn_diagonal(g[s], q[s], k[s], v16[s]) for s in pairs], axis=0)


def _hgrn_kernel(q_ref, k_ref, lf_ref, v_ref, s0_ref, o_ref, sout_ref, st_ref):
    r = pl.program_id(1)

    @pl.when(r == 0)
    def _():
        for h in range(N_HEADS):
            st_ref[h] = s0_ref[0, h].T

    o_ref[...] = _hgrn_block(q_ref[...], k_ref[...], lf_ref[...], v_ref[...], st_ref)

    @pl.when(r == pl.num_programs(1) - 1)
    def _():
        for h in range(N_HEADS):
            sout_ref[0, h] = st_ref[h].T


def _hgrn(qc, kc, lf, ic, s0, *, nb, t, tr):
    assert t % tr == 0 and tr % (2 * HGRN_DIAG) == 0 and tr & (tr - 1) == 0
    nr = t // tr
    row = pl.BlockSpec((tr, W_C), lambda b, r: (b * nr + r, 0))
    state = pl.BlockSpec((1, N_HEADS, DK_C, DK_C), lambda b, r: (b, 0, 0, 0))
    return pl.pallas_call(
        _hgrn_kernel,
        grid=(nb, nr),
        in_specs=[row, row, row, row, state],
        out_specs=[row, state],
        out_shape=[jax.ShapeDtypeStruct((nb * t, W_C), F32),
                   jax.ShapeDtypeStruct((nb, N_HEADS, DK_C, DK_C), F32)],
        scratch_shapes=[pltpu.VMEM((N_HEADS, DK_C, DK_C), F32)],
        compiler_params=_cparams(2),
        name="hgrn",
    )(qc, kc, lf, ic, s0)


def _merge_block(x, oa, ob, oc, gc, on, wo_ref, nf, wu_ref, wd_ref):
    parts = []
    for h in range(N_HEADS):
        och = oc[:, h * DK_C:(h + 1) * DK_C]
        ms = jnp.mean(och * och, axis=-1, keepdims=True)
        parts.append(och * lax.rsqrt(ms + EPS) * on)
    ocn = jnp.concatenate(parts, axis=-1) * (gc * (1.0 / (1.0 + jnp.exp(-gc))))
    mixed = jnp.concatenate([oa, ob, ocn.astype(BF16)], axis=-1)
    h_res = x + _dot(mixed, wo_ref[0])
    ms = jnp.mean(h_res * h_res, axis=-1, keepdims=True)
    hn = (h_res * lax.rsqrt(ms + EPS) * nf).astype(BF16)
    y = h_res
    for c in range(0, wu_ref.shape[2], FFN_CHUNK):
        u = jnp.maximum(_dot(hn, wu_ref[0, :, c:c + FFN_CHUNK]), 0.0)
        y = y + _dot((u * u).astype(BF16), wd_ref[0, c:c + FFN_CHUNK, :])
    return y


def _merge_kernel(x_ref, oa_ref, ob_ref, oc_ref, gc_ref, on_ref, wo_ref, nf_ref, wu_ref, wd_ref,
                  y_ref):
    y_ref[...] = _merge_block(x_ref[...], oa_ref[...], ob_ref[...], oc_ref[...], gc_ref[...],
                              on_ref[...], wo_ref, nf_ref[...], wu_ref, wd_ref)


def _merge(x2d, oa, ob, oc, gc, onorm, wo16, nf, wu16, wd16, *, layer, tm):
    m, d = x2d.shape
    assert m % tm == 0
    row = lambda w: pl.BlockSpec((tm, w), lambda i: (i, 0))
    full = _resident
    return pl.pallas_call(
        _merge_kernel,
        grid=(m // tm,),
        in_specs=[row(d), row(W_ATT), row(W_ATT), row(W_C), row(W_C),
                  full(onorm), _resident_layer(wo16, layer), full(nf),
                  _resident_layer(wu16, layer), _resident_layer(wd16, layer)],
        out_specs=row(d),
        out_shape=jax.ShapeDtypeStruct((m, d), F32),
        compiler_params=_cparams(1),
        name="merge_ffn",
    )(x2d, oa, ob, oc, gc, onorm, wo16, nf, wu16, wd16)


def kernel(x_prompt, x_sample, cache_a_k, cache_a_v, cache_b_k, cache_b_v, state_c, norm_mix, w_in, qnorm_a, knorm_a, qnorm_b, knorm_b, rel_bias_b, lower_bounds, onorm_c, w_o, norm_ffn, w_up, w_down):
    depth = w_in.shape[0]
    bp, tp, d = x_prompt.shape
    bs, ts, _ = x_sample.shape
    past = cache_a_k.shape[2]
    band_rows_p = min(N_PREV_CHUNKS * CHUNK, tp)
    assert bp == 1

    xp = x_prompt.reshape(bp * tp, d)
    xs = x_sample.reshape(bs * ts, d)
    time_minor = lambda c: jnp.transpose(c, (0, 1, 3, 4, 2)).reshape(depth, bs, W_ATT, c.shape[2])
    cak, cav, cbk, cbv = (time_minor(c) for c in (cache_a_k, cache_a_v, cache_b_k, cache_b_v))
    heads_of = lambda a_t, n: jnp.transpose(
        a_t.reshape(N_HEADS, D_HEAD, n, a_t.shape[1] // n), (2, 3, 0, 1))
    lbounds = lower_bounds.astype(F32)
    w16 = w_in[0:1].astype(BF16)
    zero_state = jnp.zeros((bp, N_HEADS, DK_C, DK_C), F32)

    tm_p = min(512, tp)
    tm_merge = min(1024, tp)
    tq_band = min(1024, tp)
    tr_p = min(256, tp)

    outs_p = [[] for _ in range(5)]
    outs_s = [[] for _ in range(5)]
    carried = ()
    for l in range(depth):
        nm = norm_mix[l].reshape(1, d)
        nf = norm_ffn[l].reshape(1, d)
        tile_h = lambda g: jnp.tile(g.reshape(1, D_HEAD), (1, N_HEADS))
        qna, kna, qnb, knb = (tile_h(g[l]) for g in (qnorm_a, knorm_a, qnorm_b, knorm_b))
        onorm = onorm_c[l].reshape(1, DK_C)
        table = rel_bias_b[l].astype(F32)

        cast = [(w_o, l), (w_up, l), (w_down, l)] + ([(w_in, l + 1)] if l + 1 < depth else [])
        (qa, ka_all, va_all, ka16_t, va16, qb, kb_t, vb_t, kb16_t, vb16, qc, kc, lf, ic, gc,
         wo16, wu16, wd16, *w16_next) = _proj(
            xp, nm, w16, qna, kna, qnb, knb, lbounds, layer=l, w_layer=0, tm=tm_p, time_minor=True,
            carried=carried, cast=cast)
        carried = (ka_all, va_all)
        oa = _sb_prompt(qa, ka16_t, va16, tq=min(SB_TQ, tp))
        ob = _band_prompt(qb, kb16_t, vb16, table, tq=tq_band)
        oc, s_p = _hgrn(qc, kc, lf, ic, zero_state, nb=bp, t=tp, tr=tr_p)
        xp = _merge(xp, oa, ob, oc, gc, onorm, wo16, nf, wu16, wd16, layer=0, tm=tm_merge)
        outs_p[2].append(heads_of(kb_t, bp)[:, tp - band_rows_p:])
        outs_p[3].append(heads_of(vb_t, bp)[:, tp - band_rows_p:])
        outs_p[4].append(s_p)

        head = lambda a, n: a.reshape(n, -1, N_HEADS, D_HEAD)
        (qa, ka, va, ka16, va16, qb, kb, vb, kb16, vb16, qc, kc, lf, ic, gc) = _proj(
            xs, nm, w16, qna, kna, qnb, knb, lbounds, layer=l, w_layer=0, tm=bs * ts,
            time_minor=False)
        oa = _sb_sample(qa, ka16, va16, cak, cav, layer=l, nb=bs, ts=ts)
        ob = _band_sample(qb, kb16, vb16, cbk, cbv, table, layer=l, nb=bs, ts=ts, past=past)
        oc, s_s = _hgrn(qc, kc, lf, ic, state_c[l].astype(F32), nb=bs, t=ts, tr=ts)
        xs = _merge(xs, oa, ob, oc, gc, onorm, wo16, nf, wu16, wd16, layer=0, tm=bs * ts)
        outs_s[0].append(head(ka, bs))
        outs_s[1].append(head(va, bs))
        outs_s[2].append(head(kb, bs))
        outs_s[3].append(head(vb, bs))
        outs_s[4].append(s_s)
        w16 = w16_next[0] if w16_next else None

    stack = lambda xs_: jnp.stack(xs_)
    all_heads_of = lambda a: jnp.transpose(
        a.reshape(depth, N_HEADS, D_HEAD, bp, tp), (0, 3, 4, 1, 2))
    ka_all, va_all = carried
    return (xp.reshape(bp, tp, d), xs.reshape(bs, ts, d),
            all_heads_of(ka_all), all_heads_of(va_all),
            stack(outs_p[2]), stack(outs_p[3]), stack(outs_p[4]),
            stack(outs_s[0]), stack(outs_s[1]), stack(outs_s[2]), stack(outs_s[3]), stack(outs_s[4]))
```

```python
import functools
import math

import jax
import jax.numpy as jnp
from jax import lax
from jax.experimental import pallas as pl
from jax.experimental.pallas import tpu as pltpu

F32 = jnp.float32
BF16 = jnp.bfloat16

D_HEAD = 64
N_HEADS = 4
W_ATT = N_HEADS * D_HEAD
DK_C = 128
W_C = N_HEADS * DK_C
CHUNK = 64
N_PREV_CHUNKS = 8
REL_CLIP = 128
N_REL = 2 * REL_CLIP + 1
EPS = 1e-6
NEG_BIG = -1e30
LB_FLOOR = 1e-30
LOG2E = math.log2(math.e)
SB_UNDERFLOW = 88.0
PROJ_PART = 256
FFN_CHUNK = 1024
SB_TK = 128
SB_TQ = 256
HGRN_DIAG = 8
VMEM_LIMIT = 60 * 1024 * 1024


def _cparams(n_axes):
    return pltpu.CompilerParams(dimension_semantics=("arbitrary",) * n_axes,
                                vmem_limit_bytes=VMEM_LIMIT)


def _resident(a):
    zeros = (0,) * a.ndim
    return pl.BlockSpec(a.shape, lambda *_: zeros, pipeline_mode=pl.Buffered(1))


def _resident_layer(a, layer):
    index = (layer,) + (0,) * (a.ndim - 1)
    return pl.BlockSpec((1,) + a.shape[1:], lambda *_: index, pipeline_mode=pl.Buffered(1))


def _split3(x):
    h1 = x.astype(BF16)
    r1 = x - h1.astype(F32)
    h2 = r1.astype(BF16)
    h3 = (r1 - h2.astype(F32)).astype(BF16)
    return h1, h2, h3


def _dot(a, b):
    return jnp.dot(a, b, preferred_element_type=F32)


def _dot_nt(a, b):
    return lax.dot_general(a, b, (((1,), (1,)), ((), ())), preferred_element_type=F32)


def _dot_tn(a, b):
    return lax.dot_general(a, b, (((0,), (0,)), ((), ())), preferred_element_type=F32)


def _iota(shape, dim):
    return lax.broadcasted_iota(jnp.int32, shape, dim)


def _proj_kernel(x_ref, nm_ref, w_ref, qna_ref, kna_ref, qnb_ref, knb_ref, lb_ref, *rest,
                 layer, time_minor, n_carried, n_cast):
    carried_refs = rest[:n_carried]
    cast_in_refs = rest[n_carried:n_carried + n_cast]
    (qa_ref, ka_ref, va_ref, ka16_ref, va16_ref, qb_ref, kb_ref, vb_ref, kb16_ref, vb16_ref,
     qc_ref, kc_ref, lf_ref, ic_ref, gc_ref) = rest[n_carried + n_cast:n_carried + n_cast + 15]
    cast_out_refs = rest[n_carried + n_cast + 15:]
    for f32_ref, b16_ref in zip(cast_in_refs, cast_out_refs):
        b16_ref[...] = f32_ref[...].astype(BF16)
    if time_minor:
        for earlier_ref, all_ref in zip(carried_refs, (ka_ref, va_ref)):
            all_ref[0:layer] = earlier_ref[...]
        ka_ref, va_ref = ka_ref.at[layer], va_ref.at[layer]
    x = x_ref[...]
    ms = jnp.mean(x * x, axis=-1, keepdims=True)
    xn = (x * lax.rsqrt(ms + EPS) * nm_ref[...]).astype(BF16)

    tm = x.shape[0]
    part = PROJ_PART if tm % PROJ_PART == 0 else tm
    projected = [_dot(xn[r0:r0 + part], w_ref[0]) for r0 in range(0, tm, part)]

    same_head = (_iota((W_ATT, W_ATT), 0) // D_HEAD) == (_iota((W_ATT, W_ATT), 1) // D_HEAD)
    seg_mean = jnp.where(same_head, 1.0 / D_HEAD, 0.0).astype(BF16)
    inv_sqrt_d = 1.0 / math.sqrt(D_HEAD)

    lbr = lb_ref[...]
    e = jnp.exp(lbr - jnp.max(lbr, axis=0, keepdims=True))
    sm = e / jnp.sum(e, axis=0, keepdims=True)
    lb = jnp.sum(sm[0:layer + 1], axis=0, keepdims=True) - sm[0:1]
    log_lb = jnp.log(jnp.maximum(lb, LB_FLOOR))
    log_1m_lb = jnp.log1p(-lb)

    def head_norm(p, g_ref):
        m = _dot((p * p).astype(BF16), seg_mean)
        return p * lax.rsqrt(m + EPS) * g_ref[...]

    for n, p in enumerate(projected):
        rows = slice(n * part, (n + 1) * part)

        def put(x, f32_ref, b16_ref, b16_time_minor):
            x_t = x.T if time_minor else None
            if time_minor:
                f32_ref[:, rows] = x_t
            else:
                f32_ref[rows, :] = x
            if time_minor and b16_time_minor:
                b16_ref[:, rows] = x_t.astype(BF16)
            else:
                b16_ref[rows, :] = x.astype(BF16)

        o = 0
        qa_ref[rows, :] = (head_norm(p[:, o:o + W_ATT], qna_ref) * inv_sqrt_d).astype(BF16)
        o += W_ATT
        put(head_norm(p[:, o:o + W_ATT], kna_ref), ka_ref, ka16_ref, True)
        o += W_ATT
        put(p[:, o:o + W_ATT], va_ref, va16_ref, False)
        o += W_ATT
        qb_ref[rows, :] = (head_norm(p[:, o:o + W_ATT], qnb_ref)
                           * (inv_sqrt_d * LOG2E)).astype(BF16)
        o += W_ATT
        put(head_norm(p[:, o:o + W_ATT], knb_ref), kb_ref, kb16_ref, True)
        o += W_ATT
        put(p[:, o:o + W_ATT], vb_ref, vb16_ref, False)
        o += W_ATT
        qc_ref[rows, :] = p[:, o:o + W_C] * (DK_C ** -0.5)
        o += W_C
        f_raw = p[:, o:o + W_C]
        o += W_C
        log_sig = jnp.minimum(f_raw, 0.0) - jnp.log(1.0 + jnp.exp(-jnp.abs(f_raw)))
        b = log_1m_lb + log_sig
        lf_ref[rows, :] = jnp.maximum(log_lb, b) + jnp.log(1.0 + jnp.exp(-jnp.abs(log_lb - b)))
        kc_ref[rows, :] = (1.0 - lb) * (1.0 / (1.0 + jnp.exp(f_raw)))
        ic_ref[rows, :] = p[:, o:o + W_C].astype(BF16)
        o += W_C
        gc_ref[rows, :] = p[:, o:o + W_C]


def _proj(x2d, nm, w16, qna, kna, qnb, knb, lower_bounds, *, layer, w_layer, tm, time_minor,
          carried=(), cast=()):
    m, d = x2d.shape
    steps = m // tm
    assert m % tm == 0 and len(carried) == (2 if time_minor and layer > 0 else 0)
    assert all(a.shape[1] % (16 * steps) == 0 for a, _ in cast)
    slab_in = lambda a, n: pl.BlockSpec((1, a.shape[1] // steps, a.shape[2]), lambda i: (n, i, 0))
    slab_out = lambda a: (pl.BlockSpec((1, a.shape[1] // steps, a.shape[2]), lambda i: (0, i, 0)),
                          jax.ShapeDtypeStruct((1,) + a.shape[1:], BF16))
    full = _resident
    row = lambda w, dt: (pl.BlockSpec((tm, w), lambda i: (i, 0)), jax.ShapeDtypeStruct((m, w), dt))
    col = lambda w, dt: (pl.BlockSpec((w, tm), lambda i: (0, i)), jax.ShapeDtypeStruct((w, m), dt))
    layers = lambda n, w: pl.BlockSpec((n, w, tm), lambda i: (0, 0, i))
    layered = lambda w, dt: (layers(layer + 1, w), jax.ShapeDtypeStruct((layer + 1, w, m), dt))
    kv = col if time_minor else row
    kv_a = layered if time_minor else row
    att_a = [row(W_ATT, BF16), kv_a(W_ATT, F32), kv_a(W_ATT, F32), kv(W_ATT, BF16), row(W_ATT, BF16)]
    att_b = [row(W_ATT, BF16), kv(W_ATT, F32), kv(W_ATT, F32), kv(W_ATT, BF16), row(W_ATT, BF16)]
    outs = att_a + att_b + [row(W_C, F32)] * 3 + [row(W_C, BF16), row(W_C, F32)]
    outs += [slab_out(a) for a, _ in cast]
    return pl.pallas_call(
        functools.partial(_proj_kernel, layer=layer, time_minor=time_minor,
                          n_carried=len(carried), n_cast=len(cast)),
        grid=(steps,),
        in_specs=[row(d, F32)[0], full(nm), _resident_layer(w16, w_layer), full(qna), full(kna),
                  full(qnb), full(knb), full(lower_bounds)]
                 + [layers(layer, W_ATT)] * len(carried) + [slab_in(a, n) for a, n in cast],
        out_specs=[spec for spec, _ in outs],
        out_shape=[shape for _, shape in outs],
        compiler_params=_cparams(1),
        name="proj",
    )(x2d, nm, w16, qna, kna, qnb, knb, lower_bounds, *carried, *(a for a, _ in cast))


def _head_masks(width):
    lane_head = _iota((1, width), 1) // D_HEAD
    return [lane_head == h for h in range(N_HEADS)]


def _stack_heads(x, head_lane):
    return jnp.concatenate([jnp.where(m, x, jnp.zeros_like(x)) for m in head_lane], axis=0)


def _stack_heads_t(x_t, n):
    row_head = _iota((W_ATT, 1), 0) // D_HEAD
    return jnp.concatenate(
        [jnp.where(row_head == h, x_t, jnp.zeros_like(x_t)) for h in range(N_HEADS)], axis=1)


def _stack_groups(x, head_lane, gq):
    return jnp.concatenate(
        [_stack_heads(x[g:g + gq], head_lane) for g in range(0, x.shape[0], gq)], axis=0)


def _sb_block(z, weigh_values, mask, tri, carry_ref, acc_ref, gq):
    tk = z.shape[1]
    t = jnp.log(1.0 + jnp.exp2(jnp.abs(z) * -math.log2(math.e)))
    log_1m = jnp.minimum(-z, 0.0) - t
    log_beta = log_1m + z
    if mask is not None:
        log_1m = jnp.where(mask, log_1m, 0.0)
    hi = log_1m.astype(BF16)
    lo = (log_1m - hi.astype(F32)).astype(BF16)
    later = _dot(jnp.concatenate([hi, lo], axis=1), tri)
    carry = carry_ref[...]
    w = jnp.exp(log_beta + later + carry[:, :tk])
    if mask is not None:
        w = jnp.where(mask, w, 0.0)
    w = w.astype(BF16)
    blocks = [w[n * gq:(n + 1) * gq] for n in range(z.shape[0] // gq)]
    w_heads = jnp.concatenate(
        [jnp.concatenate(blocks[g:g + N_HEADS], axis=1) for g in range(0, len(blocks), N_HEADS)],
        axis=0)
    acc_ref[...] += weigh_values(w_heads)
    carry_ref[...] = carry + jnp.sum(log_1m, axis=-1, keepdims=True)


def _sb_live(carry_ref):
    return (jnp.max(carry_ref[...]) > -SB_UNDERFLOW).astype(jnp.int32)


def _strict_upper(n):
    return jnp.where(_iota((2 * n, n), 0) % n > _iota((2 * n, n), 1), 1.0, 0.0).astype(BF16)


def _sb_prompt_kernel(q_ref, kt_ref, v_ref, o_ref, carry_ref, acc_ref, *, tq):
    i = pl.program_id(0)
    head_lane = _head_masks(W_ATT)
    gq = SB_TK
    n_groups = tq // gq
    q_stack = _stack_groups(q_ref[...], head_lane, gq)
    tri = _strict_upper(SB_TK)
    carry_ref[...] = jnp.zeros_like(carry_ref)
    acc_ref[...] = jnp.zeros_like(acc_ref)

    def add_block(j, first_group, mask):
        r0 = pl.multiple_of(j * SB_TK, SB_TK)
        s0 = first_group * N_HEADS * gq
        z = _dot(q_stack[s0:], kt_ref[:, pl.ds(r0, SB_TK)])
        v_stack = _stack_heads(v_ref[pl.ds(r0, SB_TK), :], head_lane)
        _sb_block(z, lambda w: _dot(w, v_stack), mask, tri,
                  carry_ref.at[pl.ds(s0, z.shape[0])],
                  acc_ref.at[pl.ds(first_group * gq, (n_groups - first_group) * gq)], gq)

    for g in reversed(range(n_groups)):
        shape = ((n_groups - g) * N_HEADS * gq, SB_TK)
        row = _iota(shape, 0)
        add_block(i * n_groups + g, g, jnp.logical_or(_iota(shape, 1) < row % gq, row >= N_HEADS * gq))
    add_block(jnp.maximum(i * n_groups - 1, 0), 0,
              jnp.broadcast_to(i > 0, (n_groups * N_HEADS * gq, SB_TK)))

    def cond(c):
        j, live = c
        return jnp.logical_and(j >= 0, live > 0)

    def body(c):
        j, _ = c
        add_block(j, 0, None)
        return j - 1, _sb_live(carry_ref)

    lax.while_loop(cond, body, (i * n_groups - 2, _sb_live(carry_ref)))
    o_ref[...] = acc_ref[...].astype(o_ref.dtype)


def _sb_prompt(q16, kt16, v16, *, tq):
    t = q16.shape[0]
    assert t % tq == 0 and tq % SB_TK == 0
    return pl.pallas_call(
        functools.partial(_sb_prompt_kernel, tq=tq),
        grid=(t // tq,),
        in_specs=[pl.BlockSpec((tq, W_ATT), lambda i: (i, 0)),
                  _resident(kt16), _resident(v16)],
        out_specs=pl.BlockSpec((tq, W_ATT), lambda i: (i, 0)),
        out_shape=jax.ShapeDtypeStruct((t, W_ATT), BF16),
        scratch_shapes=[pltpu.VMEM((N_HEADS * tq, SB_TK), F32), pltpu.VMEM((tq, W_ATT), F32)],
        compiler_params=_cparams(1),
        name="sb_prompt",
    )(q16, kt16, v16)


def _sb_sample_kernel(q_ref, kn_ref, vn_ref, ckt_ref, cvt_ref, o_ref, carry_ref, acc_ref, *, ts, past):
    head_lane = _head_masks(W_ATT)
    q_stack = _stack_heads(q_ref[...], head_lane)
    carry_ref[...] = jnp.zeros_like(carry_ref)
    acc_ref[...] = jnp.zeros_like(acc_ref)
    shape = (N_HEADS * ts, ts)
    mask = _iota(shape, 1) < _iota(shape, 0) % ts
    v_new = _stack_heads(vn_ref[...], head_lane)
    _sb_block(_dot_nt(q_stack, kn_ref[...]), lambda w: _dot(w, v_new), mask, _strict_upper(ts),
              carry_ref, acc_ref, ts)
    tri = _strict_upper(SB_TK)

    def cond(c):
        j, live = c
        return jnp.logical_and(j >= 0, live > 0)

    def body(c):
        j, _ = c
        r0 = pl.multiple_of(j * SB_TK, SB_TK)
        z = _dot(q_stack, ckt_ref[0, 0, :, pl.ds(r0, SB_TK)].astype(BF16))
        vt_stack = _stack_heads_t(cvt_ref[0, 0, :, pl.ds(r0, SB_TK)].astype(BF16), SB_TK)
        _sb_block(z, lambda w: _dot_nt(w, vt_stack), None, tri, carry_ref, acc_ref, ts)
        return j - 1, _sb_live(carry_ref)

    lax.while_loop(cond, body, (past // SB_TK - 1, _sb_live(carry_ref)))
    o_ref[...] = acc_ref[...].astype(o_ref.dtype)


def _sb_sample(q16, kn16, vn16, cache_kt, cache_vt, *, layer, nb, ts):
    past = cache_kt.shape[3]
    assert past % SB_TK == 0
    row = pl.BlockSpec((ts, W_ATT), lambda b: (b, 0))
    cache = pl.BlockSpec((1, 1, W_ATT, past), lambda b: (layer, b, 0, 0))
    return pl.pallas_call(
        functools.partial(_sb_sample_kernel, ts=ts, past=past),
        grid=(nb,),
        in_specs=[row, row, row, cache, cache],
        out_specs=row,
        out_shape=jax.ShapeDtypeStruct((nb * ts, W_ATT), BF16),
        scratch_shapes=[pltpu.VMEM((N_HEADS * ts, SB_TK), F32), pltpu.VMEM((ts, W_ATT), F32)],
        compiler_params=_cparams(1),
        name="sb_sample",
    )(q16, kn16, vn16, cache_kt, cache_vt)


def _rel_bias(table_ref, h, dist):
    idx = jnp.clip(dist, -REL_CLIP, REL_CLIP) + REL_CLIP

    def body(r, b):
        return jnp.where(idx == r, table_ref[h, r], b)

    return lax.fori_loop(0, N_REL, body, jnp.zeros(dist.shape, F32), unroll=8)


def _row_reduce(x, combine, reduce):
    width = x.shape[1]
    acc = x[:, :128]
    for c in range(128, width, 128):
        acc = combine(acc, x[:, c:c + 128])
    return reduce(acc, axis=-1, keepdims=True)


def _toeplitz_bias(table_ref, h, rows, back, width):
    lane = _iota((8, width), 1)
    by_diag = _rel_bias(table_ref, h, back + rows - lane) * LOG2E
    tiled = jnp.concatenate([by_diag] * (rows // 8), axis=0)
    return pltpu.roll(tiled, width - rows, 1, stride=1, stride_axis=0)


BAND_TQ = 128
BAND_BACK = N_PREV_CHUNKS * CHUNK
BAND_WIN = BAND_BACK + BAND_TQ


def _band_prompt_kernel(table_ref, q_ref, *rest, tq):
    n_kb = (len(rest) - 2) // 2
    k_refs, v_refs, (o_ref, bias_ref) = rest[:n_kb], rest[n_kb:2 * n_kb], rest[2 * n_kb:]
    i = pl.program_id(0)
    rows = _iota((BAND_TQ, BAND_WIN), 0)
    cols = _iota((BAND_TQ, BAND_WIN), 1)

    @pl.when(i == 0)
    def _():
        q_chunk = rows // CHUNK + N_PREV_CHUNKS
        k_chunk = cols // CHUNK
        in_band = jnp.logical_and(k_chunk >= q_chunk - N_PREV_CHUNKS, k_chunk <= q_chunk)
        for h in range(N_HEADS):
            bias = _toeplitz_bias(table_ref, h, BAND_TQ, BAND_BACK, BAND_WIN + BAND_TQ)
            bias_ref[h] = jnp.where(in_band, bias[:, :BAND_WIN], NEG_BIG)

    q = q_ref[...]
    head_lane = _head_masks(W_ATT)
    k_t = jnp.concatenate([r[...] for r in k_refs], axis=1)
    vcat = jnp.concatenate([r[...] for r in v_refs], axis=0)
    parts = [p * BAND_TQ for p in range(tq // BAND_TQ)]
    starts = [r0 + (n_kb - 1) * tq - BAND_BACK for r0 in parts]
    scores = []
    for p in range(len(parts)):
        qp = q[parts[p]:parts[p] + BAND_TQ]
        k_win = k_t[:, starts[p]:starts[p] + BAND_WIN]
        scores.append([_dot(jnp.where(head_lane[h], qp, jnp.zeros_like(qp)), k_win)
                       for h in range(N_HEADS)])
    for p in range(len(parts)):
        exists = (i - (n_kb - 1)) * tq + starts[p] + cols >= 0
        weights = []
        inv_den = []
        for h, s in enumerate(scores[p]):
            s = jnp.where(exists, s + bias_ref[h], NEG_BIG)
            e = jnp.exp2(s - _row_reduce(s, jnp.maximum, jnp.max))
            weights.append(e.astype(BF16))
            inv_den.append(1.0 / _row_reduce(e, jnp.add, jnp.sum))
        v_win = vcat[starts[p]:starts[p] + BAND_WIN]
        out = jnp.zeros((BAND_TQ, W_ATT), F32)
        for h in range(N_HEADS):
            out = jnp.where(head_lane[h], _dot(weights[h], v_win) * inv_den[h], out)
        o_ref[parts[p]:parts[p] + BAND_TQ, :] = out.astype(o_ref.dtype)


def _band_prompt(q16, kt16, v16, table, *, tq):
    t = q16.shape[0]
    assert t % tq == 0 and tq % BAND_TQ == 0
    n_kb = 1 + pl.cdiv(BAND_BACK, tq)
    blk = lambda back: pl.BlockSpec((tq, W_ATT), lambda i: (jnp.maximum(i - back, 0), 0))
    blk_t = lambda back: pl.BlockSpec((W_ATT, tq), lambda i: (0, jnp.maximum(i - back, 0)))
    oldest_first = list(reversed(range(n_kb)))
    return pl.pallas_call(
        functools.partial(_band_prompt_kernel, tq=tq),
        grid=(t // tq,),
        in_specs=[pl.BlockSpec(memory_space=pltpu.SMEM), blk(0)]
                 + [blk_t(b) for b in oldest_first] + [blk(b) for b in oldest_first],
        out_specs=blk(0),
        out_shape=jax.ShapeDtypeStruct((t, W_ATT), BF16),
        scratch_shapes=[pltpu.VMEM((N_HEADS, BAND_TQ, BAND_WIN), F32)],
        compiler_params=_cparams(1),
        name="band_prompt",
    )(table, q16, *([kt16] * n_kb), *([v16] * n_kb))


def _band_sample_kernel(table_ref, q_ref, kn_ref, vn_ref, ckt_ref, cvt_ref, o_ref, bias_ref, *, ts, past):
    lb = ckt_ref.shape[3]

    @pl.when(pl.program_id(0) == 0)
    def _():
        for h in range(N_HEADS):
            bias_ref[h] = _toeplitz_bias(table_ref, h, ts, lb, bias_ref.shape[2])

    q = q_ref[...]
    head_lane = _head_masks(W_ATT)
    kc_t = ckt_ref[0, 0].astype(BF16)
    vc_t = cvt_ref[0, 0].astype(BF16)
    kn = kn_ref[...]
    vn = vn_ref[...]
    q_pos_c = past + _iota((ts, lb), 0)
    k_pos_c = past - lb + _iota((ts, lb), 1)
    q_pos_n = past + _iota((ts, ts), 0)
    k_pos_n = past + _iota((ts, ts), 1)

    def allowed(q_pos, k_pos):
        qc, kc_ = q_pos // CHUNK, k_pos // CHUNK
        return jnp.logical_and(kc_ >= qc - N_PREV_CHUNKS, kc_ <= qc)

    ok_c = allowed(q_pos_c, k_pos_c)
    ok_n = allowed(q_pos_n, k_pos_n)
    out = jnp.zeros((ts, W_ATT), F32)
    for h in range(N_HEADS):
        qh = jnp.where(head_lane[h], q, jnp.zeros_like(q))
        bias = bias_ref[h]
        s_c = jnp.where(ok_c, _dot(qh, kc_t) + bias[:, :lb], NEG_BIG)
        s_n = jnp.where(ok_n, _dot_nt(qh, kn) + bias[:, lb:lb + ts], NEG_BIG)
        m = jnp.maximum(jnp.max(s_c, axis=-1, keepdims=True), jnp.max(s_n, axis=-1, keepdims=True))
        e_c = jnp.exp2(s_c - m)
        e_n = jnp.exp2(s_n - m)
        den = jnp.sum(e_c, axis=-1, keepdims=True) + jnp.sum(e_n, axis=-1, keepdims=True)
        pv = (_dot_nt(e_c.astype(BF16), vc_t) + _dot(e_n.astype(BF16), vn)) / den
        out = jnp.where(head_lane[h], pv, out)
    o_ref[...] = out.astype(o_ref.dtype)


def _band_sample(q16, kn16, vn16, cache_kt, cache_vt, table, *, layer, nb, ts, past):
    lb = cache_kt.shape[3]
    row = pl.BlockSpec((ts, W_ATT), lambda b: (b, 0))
    cache = pl.BlockSpec((1, 1, W_ATT, lb), lambda b: (layer, b, 0, 0))
    return pl.pallas_call(
        functools.partial(_band_sample_kernel, ts=ts, past=past),
        grid=(nb,),
        in_specs=[pl.BlockSpec(memory_space=pltpu.SMEM), row, row, row, cache, cache],
        out_specs=row,
        out_shape=jax.ShapeDtypeStruct((nb * ts, W_ATT), BF16),
        scratch_shapes=[pltpu.VMEM((N_HEADS, ts, pl.cdiv(lb + 2 * ts, 128) * 128), F32)],
        compiler_params=_cparams(1),
        name="band_sample",
    )(table, q16, kn16, vn16, cache_kt, cache_vt)


def _hgrn_spans(q, k, lf, v16, st_ref):
    tr = q.shape[0]
    heads = [slice(h * DK_C, (h + 1) * DK_C) for h in range(N_HEADS)]
    tri = jnp.where(_iota((tr, tr), 0) >= _iota((tr, tr), 1), 1.0, 0.0).astype(BF16)
    f1, f2, f3 = _split3(lf)
    g = (_dot(tri, f1) + _dot(tri, f2) + _dot(tri, f3)) * math.log2(math.e)

    g_end = g[tr - 1:tr, :]
    qt = (q * jnp.exp2(g)).astype(BF16)
    kd = (k * jnp.exp2(g_end - g)).astype(BF16)
    dec = jnp.exp2(g_end)
    states = [st_ref[h] for h in range(N_HEADS)]
    o_heads = [_dot_nt(qt[:, hs], st.astype(BF16)) for hs, st in zip(heads, states)]
    grown = [_dot_tn(v16[:, hs], kd[:, hs]) for hs in heads]
    for h, hs in enumerate(heads):
        st_ref[h] = states[h] * dec[:, hs] + grown[h]
    o = jnp.concatenate(o_heads, axis=1)

    sizes = []
    h = HGRN_DIAG
    while h < tr:
        sizes.append(h)
        h *= 2
    split = lambda x, h: x.reshape(tr // (2 * h), 2 * h, W_C)
    operands = []
    for h in sizes:
        g3, q3, k3, v3 = split(g, h), split(q, h), split(k, h), split(v16, h)
        c = g3[:, h - 1:h, :]
        qe = (q3[:, h:] * jnp.exp2(g3[:, h:] - c)).reshape(tr // 2, W_C).astype(BF16)
        ke = (k3[:, :h] * jnp.exp2(c - g3[:, :h])).reshape(tr // 2, W_C).astype(BF16)
        operands.append((qe, ke, v3[:, :h].reshape(tr // 2, W_C)))
    pair = [[_dot_nt(qe[:, hs], ke[:, hs]) for hs in heads] for qe, ke, _ in operands]
    span_of_row = _iota((tr // 2, tr // 2), 0)
    span_of_col = _iota((tr // 2, tr // 2), 1)
    for n, h in enumerate(sizes):
        if 2 * h < tr:
            same_span = (span_of_row // h) == (span_of_col // h)
            pair[n] = [jnp.where(same_span, p, 0.0) for p in pair[n]]
    added = [jnp.concatenate([_dot(p.astype(BF16), vl[:, hs]) for p, hs in zip(pair[n], heads)],
                             axis=1) for n, (_, _, vl) in enumerate(operands)]
    for n, h in enumerate(sizes):
        o3 = split(o, h)
        upper = o3[:, h:] + added[n].reshape(tr // (2 * h), h, W_C)
        o = jnp.concatenate([o3[:, :h], upper], axis=1).reshape(tr, W_C)
    return o, g


def _hgrn_diagonal(g2, q2, k2, v16):
    n = HGRN_DIAG
    lane = _iota((n, DK_C), 1)
    weights = [[jnp.zeros((n, DK_C), F32) for _ in range(N_HEADS)] for _ in range(2)]
    for s in range(2):
        g8, q8, k8 = g2[s * n:(s + 1) * n], q2[s * n:(s + 1) * n], k2[s * n:(s + 1) * n]
        for j in range(n):
            p = q8 * (k8[j:j + 1, :] * jnp.exp2(g8 - g8[j:j + 1, :]))
            for h in range(N_HEADS):
                a = jnp.sum(p[:, h * DK_C:(h + 1) * DK_C], axis=-1, keepdims=True)
                weights[s][h] = jnp.where(lane == s * n + j, a, weights[s][h])
    row = _iota((2 * n, DK_C), 0)
    col = _iota((2 * n, DK_C), 1)
    keep = jnp.logical_and(row >= col, row // n == col // n)
    out = []
    for h in range(N_HEADS):
        a = jnp.where(keep, jnp.concatenate([weights[0][h], weights[1][h]], axis=0), 0.0)
        out.append(_dot(a[:, :2 * n].astype(BF16), v16[:, h * DK_C:(h + 1) * DK_C]))
    return jnp.concatenate(out, axis=1)


def _hgrn_block(q, k, lf, v16, st_ref):
    o, g = _hgrn_spans(q, k, lf, v16, st_ref)
    pairs = [slice(r0, r0 + 2 * HGRN_DIAG) for r0 in range(0, q.shape[0], 2 * HGRN_DIAG)]
    return o + jnp.concatenate([_hgrn_diagonal(g[s], q[s], k[s], v16[s]) for s in pairs], axis=0)


def _hgrn_kernel(q_ref, k_ref, lf_ref, v_ref, s0_ref, o_ref, sout_ref, st_ref):
    r = pl.program_id(1)

    @pl.when(r == 0)
    def _():
        for h in range(N_HEADS):
            st_ref[h] = s0_ref[0, h].T

    o_ref[...] = _hgrn_block(q_ref[...], k_ref[...], lf_ref[...], v_ref[...], st_ref)

    @pl.when(r == pl.num_programs(1) - 1)
    def _():
        for h in range(N_HEADS):
            sout_ref[0, h] = st_ref[h].T


def _hgrn(qc, kc, lf, ic, s0, *, nb, t, tr):
    assert t % tr == 0 and tr % (2 * HGRN_DIAG) == 0 and tr & (tr - 1) == 0
    nr = t // tr
    row = pl.BlockSpec((tr, W_C), lambda b, r: (b * nr + r, 0))
    state = pl.BlockSpec((1, N_HEADS, DK_C, DK_C), lambda b, r: (b, 0, 0, 0))
    return pl.pallas_call(
        _hgrn_kernel,
        grid=(nb, nr),
        in_specs=[row, row, row, row, state],
        out_specs=[row, state],
        out_shape=[jax.ShapeDtypeStruct((nb * t, W_C), F32),
                   jax.ShapeDtypeStruct((nb, N_HEADS, DK_C, DK_C), F32)],
        scratch_shapes=[pltpu.VMEM((N_HEADS, DK_C, DK_C), F32)],
        compiler_params=_cparams(2),
        name="hgrn",
    )(qc, kc, lf, ic, s0)


def _mix_block(x, oa, ob, oc, gc, on, wo_ref, nf):
    parts = []
    for h in range(N_HEADS):
        och = oc[:, h * DK_C:(h + 1) * DK_C]
        ms = jnp.mean(och * och, axis=-1, keepdims=True)
        parts.append(och * lax.rsqrt(ms + EPS) * on)
    ocn = jnp.concatenate(parts, axis=-1) * (gc * (1.0 / (1.0 + jnp.exp(-gc))))
    mixed = jnp.concatenate([oa, ob, ocn.astype(BF16)], axis=-1)
    h_res = x + _dot(mixed, wo_ref[0])
    ms = jnp.mean(h_res * h_res, axis=-1, keepdims=True)
    return h_res, (h_res * lax.rsqrt(ms + EPS) * nf).astype(BF16)


def _merge_block(x, oa, ob, oc, gc, on, wo_ref, nf, wu_ref, wd_ref):
    y, hn = _mix_block(x, oa, ob, oc, gc, on, wo_ref, nf)
    for c in range(0, wu_ref.shape[2], FFN_CHUNK):
        u = jnp.maximum(_dot(hn, wu_ref[0, :, c:c + FFN_CHUNK]), 0.0)
        y = y + _dot((u * u).astype(BF16), wd_ref[0, c:c + FFN_CHUNK, :])
    return y


def _merge_kernel(x_ref, oa_ref, ob_ref, oc_ref, gc_ref, on_ref, wo_ref, nf_ref, wu_ref, wd_ref,
                  y_ref):
    y_ref[...] = _merge_block(x_ref[...], oa_ref[...], ob_ref[...], oc_ref[...], gc_ref[...],
                              on_ref[...], wo_ref, nf_ref[...], wu_ref, wd_ref)


def _merge(x2d, oa, ob, oc, gc, onorm, wo16, nf, wu16, wd16, *, layer, tm):
    m, d = x2d.shape
    assert m % tm == 0
    row = lambda w: pl.BlockSpec((tm, w), lambda i: (i, 0))
    full = _resident
    return pl.pallas_call(
        _merge_kernel,
        grid=(m // tm,),
        in_specs=[row(d), row(W_ATT), row(W_ATT), row(W_C), row(W_C),
                  full(onorm), _resident_layer(wo16, layer), full(nf),
                  _resident_layer(wu16, layer), _resident_layer(wd16, layer)],
        out_specs=row(d),
        out_shape=jax.ShapeDtypeStruct((m, d), F32),
        compiler_params=_cparams(1),
        name="merge_ffn",
    )(x2d, oa, ob, oc, gc, onorm, wo16, nf, wu16, wd16)


def _merge_streamed_kernel(x_ref, oa_ref, ob_ref, oc_ref, gc_ref, on_ref, wo_ref, nf_ref,
                           wu_ref, wd_ref, y_ref, hn_ref):
    @pl.when(pl.program_id(0) == 0)
    def _():
        y_ref[...], hn_ref[...] = _mix_block(x_ref[...], oa_ref[...], ob_ref[...], oc_ref[...],
                                             gc_ref[...], on_ref[...], wo_ref, nf_ref[...])

    u = jnp.maximum(_dot(hn_ref[...], wu_ref[0]), 0.0)
    y_ref[...] += _dot((u * u).astype(BF16), wd_ref[0])


def _merge_streamed(x2d, oa, ob, oc, gc, onorm, wo16, nf, wu16, wd16, *, layer):
    m, d = x2d.shape
    d_ff = wu16.shape[2]
    assert d_ff % FFN_CHUNK == 0
    whole = lambda a: pl.BlockSpec(a.shape, lambda c: (0,) * a.ndim)
    return pl.pallas_call(
        _merge_streamed_kernel,
        grid=(d_ff // FFN_CHUNK,),
        in_specs=[whole(x2d), whole(oa), whole(ob), whole(oc), whole(gc), whole(onorm),
                  _resident_layer(wo16, layer), whole(nf),
                  pl.BlockSpec((1, d, FFN_CHUNK), lambda c: (layer, 0, c)),
                  pl.BlockSpec((1, FFN_CHUNK, d), lambda c: (layer, c, 0))],
        out_specs=pl.BlockSpec((m, d), lambda c: (0, 0)),
        out_shape=jax.ShapeDtypeStruct((m, d), F32),
        scratch_shapes=[pltpu.VMEM((m, d), BF16)],
        compiler_params=_cparams(1),
        name="merge_ffn_streamed",
    )(x2d, oa, ob, oc, gc, onorm, wo16, nf, wu16, wd16)


def kernel(x_prompt, x_sample, cache_a_k, cache_a_v, cache_b_k, cache_b_v, state_c, norm_mix, w_in, qnorm_a, knorm_a, qnorm_b, knorm_b, rel_bias_b, lower_bounds, onorm_c, w_o, norm_ffn, w_up, w_down):
    depth = w_in.shape[0]
    bp, tp, d = x_prompt.shape
    bs, ts, _ = x_sample.shape
    past = cache_a_k.shape[2]
    band_rows_p = min(N_PREV_CHUNKS * CHUNK, tp)
    assert bp == 1

    xp = x_prompt.reshape(bp * tp, d)
    xs = x_sample.reshape(bs * ts, d)
    time_minor = lambda c: jnp.transpose(c, (0, 1, 3, 4, 2)).reshape(depth, bs, W_ATT, c.shape[2])
    cak, cav, cbk, cbv = (time_minor(c) for c in (cache_a_k, cache_a_v, cache_b_k, cache_b_v))
    heads_of = lambda a_t, n: jnp.transpose(
        a_t.reshape(N_HEADS, D_HEAD, n, a_t.shape[1] // n), (2, 3, 0, 1))
    lbounds = lower_bounds.astype(F32)
    w16 = w_in[0:1].astype(BF16)
    zero_state = jnp.zeros((bp, N_HEADS, DK_C, DK_C), F32)

    tm_p = min(512, tp)
    tm_merge = min(1024, tp)
    tq_band = min(1024, tp)
    tr_p = min(256, tp)

    outs_p = [[] for _ in range(5)]
    outs_s = [[] for _ in range(5)]
    carried = ()
    for l in range(depth):
        nm = norm_mix[l].reshape(1, d)
        nf = norm_ffn[l].reshape(1, d)
        tile_h = lambda g: jnp.tile(g.reshape(1, D_HEAD), (1, N_HEADS))
        qna, kna, qnb, knb = (tile_h(g[l]) for g in (qnorm_a, knorm_a, qnorm_b, knorm_b))
        onorm = onorm_c[l].reshape(1, DK_C)
        table = rel_bias_b[l].astype(F32)

        cast = [(w_o, l), (w_up, l), (w_down, l)] + ([(w_in, l + 1)] if l + 1 < depth else [])
        (qa, ka_all, va_all, ka16_t, va16, qb, kb_t, vb_t, kb16_t, vb16, qc, kc, lf, ic, gc,
         wo16, wu16, wd16, *w16_next) = _proj(
            xp, nm, w16, qna, kna, qnb, knb, lbounds, layer=l, w_layer=0, tm=tm_p, time_minor=True,
            carried=carried, cast=cast)
        carried = (ka_all, va_all)
        oa = _sb_prompt(qa, ka16_t, va16, tq=min(SB_TQ, tp))
        ob = _band_prompt(qb, kb16_t, vb16, table, tq=tq_band)
        oc, s_p = _hgrn(qc, kc, lf, ic, zero_state, nb=bp, t=tp, tr=tr_p)
        xp = _merge(xp, oa, ob, oc, gc, onorm, wo16, nf, wu16, wd16, layer=0, tm=tm_merge)
        outs_p[2].append(heads_of(kb_t, bp)[:, tp - band_rows_p:])
        outs_p[3].append(heads_of(vb_t, bp)[:, tp - band_rows_p:])
        outs_p[4].append(s_p)

        head = lambda a, n: a.reshape(n, -1, N_HEADS, D_HEAD)
        (qa, ka, va, ka16, va16, qb, kb, vb, kb16, vb16, qc, kc, lf, ic, gc) = _proj(
            xs, nm, w16, qna, kna, qnb, knb, lbounds, layer=l, w_layer=0, tm=bs * ts,
            time_minor=False)
        oa = _sb_sample(qa, ka16, va16, cak, cav, layer=l, nb=bs, ts=ts)
        ob = _band_sample(qb, kb16, vb16, cbk, cbv, table, layer=l, nb=bs, ts=ts, past=past)
        oc, s_s = _hgrn(qc, kc, lf, ic, state_c[l].astype(F32), nb=bs, t=ts, tr=ts)
        xs = _merge_streamed(xs, oa, ob, oc, gc, onorm, wo16, nf, wu16, wd16, layer=0)
        outs_s[0].append(head(ka, bs))
        outs_s[1].append(head(va, bs))
        outs_s[2].append(head(kb, bs))
        outs_s[3].append(head(vb, bs))
        outs_s[4].append(s_s)
        w16 = w16_next[0] if w16_next else None

    stack = lambda xs_: jnp.stack(xs_)
    all_heads_of = lambda a: jnp.transpose(
        a.reshape(depth, N_HEADS, D_HEAD, bp, tp), (0, 3, 4, 1, 2))
    ka_all, va_all = carried
    return (xp.reshape(bp, tp, d), xs.reshape(bs, ts, d),
            all_heads_of(ka_all), all_heads_of(va_all),
            stack(outs_p[2]), stack(outs_p[3]), stack(outs_p[4]),
            stack(outs_s[0]), stack(outs_s[1]), stack(outs_s[2]), stack(outs_s[3]), stack(outs_s[4]))
```
